```python
import math
import jax
import jax.numpy as jnp
from jax import lax
import numpy as np

D_MODEL = 1024
BATCH = 8
SEQ = 2048
DEPTH = 4

GRID_W = 64
CTX_LEN = 256
NORM_EPS = 1e-6
N_MOD = 6

CONV_CH = 512
CONV_W = 3
MLA_HEADS = 8
Q_RANK = 256
KV_RANK = 128
QK_NOPE = 64
QK_ROPE = 32
V_HEAD = 64
MLA_SCALE = (QK_NOPE + QK_ROPE) ** -0.5
ROPE_BASE = 10000.0
Q_BLOCK = 128
EVEN_IN = 3 * CONV_CH + Q_RANK + KV_RANK + QK_ROPE
EVEN_MIX = CONV_CH + MLA_HEADS * V_HEAD
EVEN_SPLITS = (CONV_CH, 2 * CONV_CH, 3 * CONV_CH, 3 * CONV_CH + Q_RANK, 3 * CONV_CH + Q_RANK + KV_RANK)

GDN_HEADS = 8
GDN_DK = 128
GDN_DV = 128
GDN_CONV_W = 3
GDN_CHUNK = 64
GDN_QK = GDN_HEADS * GDN_DK
GDN_V = GDN_HEADS * GDN_DV
ODD_IN = 2 * GDN_QK + 2 * GDN_V + 4 * GDN_HEADS
ODD_SPLITS = (2 * GDN_QK + GDN_V, 2 * GDN_QK + 2 * GDN_V, 2 * GDN_QK + 2 * GDN_V + 2 * GDN_HEADS)

FFN_DENSE = 2816
N_EXPERTS = 8
TOP_K = 2
FFN_EXPERT = 3584

kernel_name = "hybrid_conv_mla_gdn_moe_diffusion_trunk"


def rmsnorm(x, g):
    xf = x.astype(jnp.float32)
    y = xf * lax.rsqrt(jnp.mean(xf * xf, axis=-1, keepdims=True) + NORM_EPS)
    return (y * g.astype(jnp.float32)).astype(x.dtype)


def l2norm(x):
    return x * lax.rsqrt(jnp.sum(x * x, axis=-1, keepdims=True) + NORM_EPS)


def adaln_params(cond, mod_w, mod_b):
    m = (jax.nn.silu(cond) @ mod_w + mod_b)[..., None, :]
    return jnp.split(m, N_MOD, axis=-1)


def modulate(h, shift, scale):
    return h * (1.0 + scale) + shift


def dwconv_centred(x, w):
    k_w = w.shape[0]
    pad = k_w // 2
    L = x.shape[1]
    xp = jnp.pad(x, ((0, 0), (pad, pad), (0, 0)))
    y = xp[:, 0:L] * w[0]
    for j in range(1, k_w):
        y = y + xp[:, j:j + L] * w[j]
    return y


def axial_rope_tables(row, col, dtype):
    n = QK_ROPE // 4
    inv = ROPE_BASE ** (-jnp.arange(n, dtype=jnp.float32) / n)
    ang = jnp.concatenate([row.astype(jnp.float32)[:, None] * inv, col.astype(jnp.float32)[:, None] * inv], axis=-1)
    return jnp.cos(ang).astype(dtype), jnp.sin(ang).astype(dtype)


def apply_axial_rope(x, cos, sin):
    q = x.shape[-1] // 4
    x1, x2, x3, x4 = jnp.split(x, 4, axis=-1)
    a = jnp.concatenate([x1, x3], axis=-1)
    b = jnp.concatenate([x2, x4], axis=-1)
    ra = a * cos - b * sin
    rb = b * cos + a * sin
    return jnp.concatenate([ra[..., :q], rb[..., :q], ra[..., q:], rb[..., q:]], axis=-1)


def mla_attend(q_nope, q_rope, k_nope, k_rope, v):
    s = jnp.einsum("bqhd,bkhd->bhqk", q_nope, k_nope) + jnp.einsum("bqhr,bkr->bhqk", q_rope, k_rope)
    p = jax.nn.softmax(s.astype(jnp.float32) * MLA_SCALE, axis=-1).astype(v.dtype)
    return jnp.einsum("bhqk,bkhd->bqhd", p, v)


def latent_attention(q_nope, q_rope, k_nope, k_rope, v):
    b, L = q_nope.shape[:2]
    nb = L // Q_BLOCK

    def blocks(t):
        return jnp.moveaxis(t.reshape(b, nb, Q_BLOCK, *t.shape[2:]), 1, 0)

    out = lax.map(lambda qs: mla_attend(qs[0], qs[1], k_nope, k_rope, v), (blocks(q_nope), blocks(q_rope)))
    return jnp.moveaxis(out, 0, 1).reshape(b, L, MLA_HEADS * V_HEAD)


def unit_lower_inverse(lm):
    eye = jnp.eye(lm.shape[-1], dtype=lm.dtype)
    p = -lm
    t = eye + p
    for _ in range((lm.shape[-1] - 1).bit_length() - 1):
        p = p @ p
        t = t @ (eye + p)
    return t


def gated_delta_chunked(q, k, v, g, beta, state):
    b, L, h, _ = k.shape
    dv = v.shape[-1]
    n = L // GDN_CHUNK

    def chunks(t):
        return jnp.swapaxes(t.reshape(b, n, GDN_CHUNK, *t.shape[2:]), 2, 3)

    q, k, v, g, beta = (chunks(t) for t in (q, k, v, g, beta))
    gcum = jnp.cumsum(g, axis=-1)
    incl = jnp.tril(jnp.ones((GDN_CHUNK, GDN_CHUNK), dtype=bool))
    decay = jnp.exp(jnp.where(incl, gcum[..., :, None] - gcum[..., None, :], -jnp.inf))
    kb = k * beta[..., None]
    strict = jnp.tril(jnp.einsum("bnhid,bnhjd->bnhij", kb, k) * decay, -1)
    tinv = unit_lower_inverse(strict)
    u = tinv @ (v * beta[..., None])
    w = tinv @ (kb * jnp.exp(gcum)[..., None])
    attn = jnp.einsum("bnhid,bnhjd->bnhij", q, k) * decay
    qg = q * jnp.exp(gcum)[..., None]
    kd = k * jnp.exp(gcum[..., -1:] - gcum)[..., None]
    gend = jnp.exp(gcum[..., -1])[..., None, None]

    def step(s, xs):
        u_i, w_i, a_i, qg_i, kd_i, ge_i = xs
        v_new = u_i - w_i @ s
        o_i = qg_i @ s + a_i @ v_new
        s = s * ge_i + jnp.swapaxes(kd_i, -1, -2) @ v_new
        return s, o_i

    xs = tuple(jnp.moveaxis(t, 1, 0) for t in (u, w, attn, qg, kd, gend))
    state, o = lax.scan(step, state, xs)
    o = jnp.swapaxes(jnp.moveaxis(o, 0, 1), 2, 3).reshape(b, L, h, dv)
    return o, state


def scan_direction(q, k, v, g, beta, state, reverse):
    if reverse:
        o, s = gated_delta_chunked(*(jnp.flip(t, 1) for t in (q, k, v, g, beta)), state)
        return jnp.flip(o, 1), s
    return gated_delta_chunked(q, k, v, g, beta, state)


def swiglu(h, w1, w2):
    gate, up = jnp.split(h @ w1, 2, axis=-1)
    return (jax.nn.silu(gate) * up) @ w2


def moe_swiglu(h, router_w, w1, w2):
    logits = (h @ router_w).astype(jnp.float32)
    top_v, top_i = lax.top_k(logits, TOP_K)
    top_w = jax.nn.softmax(top_v, axis=-1)
    gates = jnp.sum(jax.nn.one_hot(top_i, N_EXPERTS, dtype=jnp.float32) * top_w[..., None], axis=-2).astype(h.dtype)
    y = gates[..., 0:1] * swiglu(h, w1[0], w2[0])
    for e in range(1, N_EXPERTS):
        y = y + gates[..., e:e + 1] * swiglu(h, w1[e], w2[e])
    return y


def even_mixer(hc, hx, cos, sin, w_in, conv_w, q_norm_g, w_uq, kv_norm_g, w_ukv, w_out, ctx_out):
    def project(h):
        b, L = h.shape[:2]
        gb, gc, u, cq, ckv, kr = jnp.split(h @ w_in, EVEN_SPLITS, axis=-1)
        conv_y = gb * dwconv_centred(gc * u, conv_w)
        q = (rmsnorm(cq, q_norm_g) @ w_uq).reshape(b, L, MLA_HEADS, QK_NOPE + QK_ROPE)
        kv = (rmsnorm(ckv, kv_norm_g) @ w_ukv).reshape(b, L, MLA_HEADS, QK_NOPE + V_HEAD)
        return conv_y, q[..., :QK_NOPE], q[..., QK_NOPE:], kv[..., :QK_NOPE], kr, kv[..., QK_NOPE:]

    cy_c, qn_c, qr_c, kn_c, kr_c, v_c = project(hc)
    cy_x, qn_x, qr_x, kn_x, kr_x, v_x = project(hx)
    qr_x = apply_axial_rope(qr_x, cos[:, None, :], sin[:, None, :])
    kr_x = apply_axial_rope(kr_x, cos, sin)
    k_nope = jnp.concatenate([kn_c, kn_x], axis=1)
    k_rope = jnp.concatenate([kr_c, kr_x], axis=1)
    v = jnp.concatenate([v_c, v_x], axis=1)
    att_x = latent_attention(qn_x, qr_x, k_nope, k_rope, v)
    out_x = jnp.concatenate([cy_x, att_x], axis=-1) @ w_out
    out_c = None
    if ctx_out:
        b, lc = hc.shape[:2]
        att_c = mla_attend(qn_c, qr_c, kn_c, kr_c, v_c).reshape(b, lc, MLA_HEADS * V_HEAD)
        out_c = jnp.concatenate([cy_c, att_c], axis=-1) @ w_out
    return out_c, out_x


def odd_mixer(hc, hx, w_in, qkv_conv_w, a_log, dt_bias, o_norm_g, w_out, ctx_out):
    def project(h):
        b, L = h.shape[:2]
        qkv, gate, a, bt = jnp.split(h @ w_in, ODD_SPLITS, axis=-1)
        qkv = jax.nn.silu(dwconv_centred(qkv, qkv_conv_w)).astype(jnp.float32)
        q, k, v = jnp.split(qkv, (GDN_QK, 2 * GDN_QK), axis=-1)
        q = l2norm(q.reshape(b, L, GDN_HEADS, GDN_DK)) * (GDN_DK ** -0.5)
        k = l2norm(k.reshape(b, L, GDN_HEADS, GDN_DK))
        v = v.reshape(b, L, GDN_HEADS, GDN_DV)
        a = a.astype(jnp.float32).reshape(b, L, 2, GDN_HEADS)
        g = -jnp.exp(a_log.astype(jnp.float32)) * jax.nn.softplus(a + dt_bias.astype(jnp.float32))
        beta = jax.nn.sigmoid(bt.astype(jnp.float32).reshape(b, L, 2, GDN_HEADS))
        return q, k, v, g, beta, gate.reshape(b, L, GDN_HEADS, GDN_DV)

    def output(o, gate, dtype):
        b, L = o.shape[:2]
        y = rmsnorm(o, o_norm_g) * jax.nn.silu(gate.astype(jnp.float32))
        return y.reshape(b, L, GDN_V).astype(dtype) @ w_out

    qc, kc, vc, gcc, bc, zc = project(hc)
    qx, kx, vx, gx, bx, zx = project(hx)
    s0 = jnp.zeros((hx.shape[0], GDN_HEADS, GDN_DK, GDN_DV), jnp.float32)
    oc_f, sc_f = scan_direction(qc, kc, vc, gcc[:, :, 0], bc[:, :, 0], s0, False)
    oc_b, sc_b = scan_direction(qc, kc, vc, gcc[:, :, 1], bc[:, :, 1], s0, True)
    ox_f, _ = scan_direction(qx, kx, vx, gx[:, :, 0], bx[:, :, 0], sc_f, False)
    ox_b, _ = scan_direction(qx, kx, vx, gx[:, :, 1], bx[:, :, 1], sc_b, True)
    out_x = output(ox_f + ox_b, zx, hx.dtype)
    out_c = output(oc_f + oc_b, zc, hc.dtype) if ctx_out else None
    return out_c, out_x


def trunk_layer(x, ctx, c, c_ctx, mod_w, mod_b, norm1_g, norm2_g, mixer, ffn, ctx_out):
    sx1, ax1, gx1, sx2, ax2, gx2 = adaln_params(c, mod_w, mod_b)
    sc1, ac1, gc1, sc2, ac2, gc2 = adaln_params(c_ctx, mod_w, mod_b)
    hx = modulate(rmsnorm(x, norm1_g), sx1, ax1)
    hc = modulate(rmsnorm(ctx, norm1_g), sc1, ac1)
    oc, ox = mixer(hc, hx)
    x = x + gx1 * ox
    hx = modulate(rmsnorm(x, norm2_g), sx2, ax2)
    if ctx_out:
        ctx = ctx + gc1 * oc
        hc = modulate(rmsnorm(ctx, norm2_g), sc2, ac2)
        lc = ctx.shape[1]
        y = ffn(jnp.concatenate([hc, hx], axis=1))
        ctx = ctx + gc2 * y[:, :lc]
        x = x + gx2 * y[:, lc:]
    else:
        x = x + gx2 * ffn(hx)
    return x, ctx


def even_layer(x, ctx, c, c_ctx, cos, sin, mod_w, mod_b, norm1_g, w_in, conv_w, q_norm_g, w_uq, kv_norm_g,
               w_ukv, w_out, norm2_g, ffn_w1, ffn_w2, ctx_out):
    mixer = lambda hc, hx: even_mixer(hc, hx, cos, sin, w_in, conv_w, q_norm_g, w_uq, kv_norm_g, w_ukv, w_out, ctx_out)
    ffn = lambda h: swiglu(h, ffn_w1, ffn_w2)
    return trunk_layer(x, ctx, c, c_ctx, mod_w, mod_b, norm1_g, norm2_g, mixer, ffn, ctx_out)


def odd_layer(x, ctx, c, c_ctx, mod_w, mod_b, norm1_g, w_in, qkv_conv_w, a_log, dt_bias, o_norm_g, w_out,
              norm2_g, router_w, moe_w1, moe_w2, ctx_out):
    mixer = lambda hc, hx: odd_mixer(hc, hx, w_in, qkv_conv_w, a_log, dt_bias, o_norm_g, w_out, ctx_out)
    ffn = lambda h: moe_swiglu(h, router_w, moe_w1, moe_w2)
    return trunk_layer(x, ctx, c, c_ctx, mod_w, mod_b, norm1_g, norm2_g, mixer, ffn, ctx_out)


def setup_inputs(seed: int = 0) -> dict:
    base = jax.random.key(seed)
    counter = iter(range(100000))

    def nk():
        return jax.random.fold_in(base, next(counter))

    def normal(shape, scale):
        return scale * jax.random.normal(nk(), shape, jnp.float32)

    def gain(n):
        return 1.0 + normal((n,), 0.02)

    d = D_MODEL
    inp = {}
    inp["x"] = normal((BATCH, SEQ, d), 1.0)
    inp["c"] = normal((BATCH, d), 1.0)
    inp["ctx"] = normal((BATCH, CTX_LEN, d), 1.0)
    inp["c_ctx"] = normal((d,), 1.0)
    for i in range(DEPTH):
        p = "l%d_" % i
        inp[p + "mod_w"] = normal((d, N_MOD * d), 0.5 * d ** -0.5)
        inp[p + "mod_b"] = normal((N_MOD * d,), 0.02)
        inp[p + "norm1_g"] = gain(d)
        if i % 2 == 0:
            inp[p + "w_in"] = normal((d, EVEN_IN), d ** -0.5)
            inp[p + "conv_w"] = normal((CONV_W, CONV_CH), CONV_W ** -0.5)
            inp[p + "q_norm_g"] = gain(Q_RANK)
            inp[p + "w_uq"] = normal((Q_RANK, MLA_HEADS * (QK_NOPE + QK_ROPE)), Q_RANK ** -0.5)
            inp[p + "kv_norm_g"] = gain(KV_RANK)
            inp[p + "w_ukv"] = normal((KV_RANK, MLA_HEADS * (QK_NOPE + V_HEAD)), KV_RANK ** -0.5)
            inp[p + "w_out"] = normal((EVEN_MIX, d), EVEN_MIX ** -0.5)
            inp[p + "norm2_g"] = gain(d)
            inp[p + "ffn_w1"] = normal((d, 2 * FFN_DENSE), d ** -0.5)
            inp[p + "ffn_w2"] = normal((FFN_DENSE, d), FFN_DENSE ** -0.5)
        else:
            inp[p + "w_in"] = normal((d, ODD_IN), d ** -0.5)
            inp[p + "qkv_conv_w"] = normal((GDN_CONV_W, 2 * GDN_QK + GDN_V), GDN_CONV_W ** -0.5)
            inp[p + "a_log"] = jnp.log(jax.random.uniform(nk(), (2, GDN_HEADS), jnp.float32, 1.0, 16.0))
            dt = jnp.exp(jax.random.uniform(nk(), (2, GDN_HEADS), jnp.float32, math.log(1e-3), math.log(1e-1)))
            inp[p + "dt_bias"] = dt + jnp.log(-jnp.expm1(-dt))
            inp[p + "o_norm_g"] = gain(GDN_DV)
            inp[p + "w_out"] = normal((GDN_V, d), GDN_V ** -0.5)
            inp[p + "norm2_g"] = gain(d)
            inp[p + "router_w"] = normal((d, N_EXPERTS), d ** -0.5)
            inp[p + "moe_w1"] = normal((N_EXPERTS, d, 2 * FFN_EXPERT), d ** -0.5)
            inp[p + "moe_w2"] = normal((N_EXPERTS, FFN_EXPERT, d), FFN_EXPERT ** -0.5)
    inp["final_norm_g"] = gain(d)
    return inp


def reference(x, c, ctx, c_ctx,
              l0_mod_w, l0_mod_b, l0_norm1_g, l0_w_in, l0_conv_w, l0_q_norm_g, l0_w_uq, l0_kv_norm_g, l0_w_ukv,
              l0_w_out, l0_norm2_g, l0_ffn_w1, l0_ffn_w2,
              l1_mod_w, l1_mod_b, l1_norm1_g, l1_w_in, l1_qkv_conv_w, l1_a_log, l1_dt_bias, l1_o_norm_g,
              l1_w_out, l1_norm2_g, l1_router_w, l1_moe_w1, l1_moe_w2,
              l2_mod_w, l2_mod_b, l2_norm1_g, l2_w_in, l2_conv_w, l2_q_norm_g, l2_w_uq, l2_kv_norm_g, l2_w_ukv,
              l2_w_out, l2_norm2_g, l2_ffn_w1, l2_ffn_w2,
              l3_mod_w, l3_mod_b, l3_norm1_g, l3_w_in, l3_qkv_conv_w, l3_a_log, l3_dt_bias, l3_o_norm_g,
              l3_w_out, l3_norm2_g, l3_router_w, l3_moe_w1, l3_moe_w2,
              final_norm_g):
    rows = x.shape[1] // GRID_W
    row = jnp.repeat(jnp.arange(rows), GRID_W)
    col = jnp.tile(jnp.arange(GRID_W), rows)
    cos, sin = axial_rope_tables(row, col, x.dtype)
    layers = (
        (l0_mod_w, l0_mod_b, l0_norm1_g, l0_w_in, l0_conv_w, l0_q_norm_g, l0_w_uq, l0_kv_norm_g, l0_w_ukv,
         l0_w_out, l0_norm2_g, l0_ffn_w1, l0_ffn_w2),
        (l1_mod_w, l1_mod_b, l1_norm1_g, l1_w_in, l1_qkv_conv_w, l1_a_log, l1_dt_bias, l1_o_norm_g,
         l1_w_out, l1_norm2_g, l1_router_w, l1_moe_w1, l1_moe_w2),
        (l2_mod_w, l2_mod_b, l2_norm1_g, l2_w_in, l2_conv_w, l2_q_norm_g, l2_w_uq, l2_kv_norm_g, l2_w_ukv,
         l2_w_out, l2_norm2_g, l2_ffn_w1, l2_ffn_w2),
        (l3_mod_w, l3_mod_b, l3_norm1_g, l3_w_in, l3_qkv_conv_w, l3_a_log, l3_dt_bias, l3_o_norm_g,
         l3_w_out, l3_norm2_g, l3_router_w, l3_moe_w1, l3_moe_w2),
    )
    for i in range(DEPTH):
        ctx_out = i < DEPTH - 1
        if i % 2 == 0:
            x, ctx = even_layer(x, ctx, c, c_ctx, cos, sin, *layers[i], ctx_out=ctx_out)
        else:
            x, ctx = odd_layer(x, ctx, c, c_ctx, *layers[i], ctx_out=ctx_out)
    return rmsnorm(x, final_norm_g)
```

```python
import functools
import math

import jax
import jax.numpy as jnp
from jax import lax
from jax.experimental import pallas as pl
from jax.experimental.pallas import tpu as pltpu

F32 = jnp.float32
BF16 = jnp.bfloat16

NORM_EPS = 1e-6
N_MOD = 6
GRID_W = 64
ROPE_BASE = 10000.0

CONV_CH = 512
MLA_HEADS = 8
Q_RANK = 256
KV_RANK = 128
QK_NOPE = 64
QK_ROPE = 32
V_HEAD = 64
MLA_SCALE = (QK_NOPE + QK_ROPE) ** -0.5

GDN_HEADS = 8
GDN_DK = 128
GDN_CHUNK = 64
GDN_QK = GDN_HEADS * GDN_DK

N_EXPERTS = 8

LANES = 128
ROW_TILE = 256
GDN_STEP = 256
EXPERT_TILE = 512
EXPERT_FCHUNK = 512
VMEM_LIMIT_BYTES = 56 * 1024 * 1024


def _cparams(*sem):
    return pltpu.CompilerParams(dimension_semantics=sem, vmem_limit_bytes=VMEM_LIMIT_BYTES)


def _silu(x):
    return x * (1.0 / (1.0 + jnp.exp(-x)))


def _rms(x, g):
    return x * lax.rsqrt(jnp.mean(x * x, axis=-1, keepdims=True) + NORM_EPS) * g


def _dot(a, b):
    return jnp.dot(a, b, preferred_element_type=F32)


def _dot_nt(a, b):
    return lax.dot_general(a, b, (((1,), (1,)), ((), ())), preferred_element_type=F32)


def _dot_tn(a, b):
    return lax.dot_general(a, b, (((0,), (0,)), ((), ())), preferred_element_type=F32)


def _split3(x):
    x1 = x.astype(BF16)
    r1 = x - x1.astype(F32)
    x2 = r1.astype(BF16)
    x3 = (r1 - x2.astype(F32)).astype(BF16)
    return x1, x2, x3


def _dot_f32(a, b):
    a1, a2, a3 = _split3(a)
    b1, b2, b3 = _split3(b)
    acc = _dot(a1, b3) + _dot(a2, b2) + _dot(a3, b1)
    acc = acc + _dot(a1, b2) + _dot(a2, b1)
    return acc + _dot(a1, b1)


def _dot_f32_exact_rhs(a, m_bf16):
    a1, a2, a3 = _split3(a)
    return _dot(a3, m_bf16) + _dot(a2, m_bf16) + _dot(a1, m_bf16)


def _mod_row_fn(n_ctx_tiles, tiles_per_seq, batch):
    def row(i):
        return jnp.where(i < n_ctx_tiles, batch, (i - n_ctx_tiles) // tiles_per_seq)
    return row


def _mods_kernel(c_ref, w_ref, b_ref, o_ref):
    o_ref[...] = _dot_f32(_silu(c_ref[...]), w_ref[...]) + b_ref[...]


def _mods(cc, mod_w, mod_b):
    rows, d = cc.shape
    n = mod_w.shape[1]
    out = pl.pallas_call(
        _mods_kernel,
        grid=(n // d,),
        in_specs=[pl.BlockSpec((rows, d), lambda j: (0, 0)),
                  pl.BlockSpec((d, d), lambda j: (0, j)),
                  pl.BlockSpec((1, d), lambda j: (0, j))],
        out_specs=pl.BlockSpec((rows, d), lambda j: (0, j)),
        out_shape=jax.ShapeDtypeStruct((rows, n), F32),
        compiler_params=_cparams("arbitrary"),
        name="adaln_mods",
    )(cc, mod_w, mod_b.reshape(1, n))
    return out.reshape(rows, 1, n)


def _mod_spec(d, k, row, off=0):
    return pl.BlockSpec((1, 1, d), lambda i: (row(i + off), 0, k))


def _even_proj_kernel(x_ref, g_ref, sh_ref, sc_ref, win_ref, qg_ref, wq_ref, kvg_ref, wkv_ref, cos_ref, sin_ref,
                      gbu_ref, q_ref, k_ref, v_ref):
    hn = _rms(x_ref[...], g_ref[...]) * (1.0 + sc_ref[0]) + sh_ref[0]
    hb = hn.astype(BF16)
    n_gbu = 3 * CONV_CH
    for c0 in range(0, n_gbu, CONV_CH):
        gbu_ref[:, c0:c0 + CONV_CH] = _dot(hb, win_ref[:, c0:c0 + CONV_CH]).astype(gbu_ref.dtype)
    rest = _dot(hb, win_ref[:, n_gbu:])
    cq = rest[:, :Q_RANK]
    ckv = rest[:, Q_RANK:Q_RANK + KV_RANK]
    kr = rest[:, Q_RANK + KV_RANK:Q_RANK + KV_RANK + LANES]
    kr_sw = rest[:, Q_RANK + KV_RANK + LANES:]
    cos = cos_ref[...]
    sin = sin_ref[...]
    kr_rot = kr * cos + kr_sw * sin
    qq = _dot(_rms(cq, qg_ref[...]).astype(BF16), wq_ref[...])
    kv = _dot(_rms(ckv, kvg_ref[...]).astype(BF16), wkv_ref[...])
    hw = MLA_HEADS * LANES
    for h in range(MLA_HEADS):
        sl = slice(h * LANES, (h + 1) * LANES)
        qh = (qq[:, sl] * cos + qq[:, hw + h * LANES:hw + (h + 1) * LANES] * sin) * MLA_SCALE
        q_ref[:, sl] = qh.astype(q_ref.dtype)
        k_ref[:, sl] = (kv[:, sl] + kr_rot).astype(k_ref.dtype)
    v_ref[...] = kv[:, hw:].astype(v_ref.dtype)


def _even_proj(x2, g, mods, row, n_tiles, w_in, qg, wq, kvg, wkv, cos_t, sin_t, tab_idx):
    t, d = x2.shape
    tm = ROW_TILE
    n_in = w_in.shape[1]
    hw = MLA_HEADS * LANES
    const = lambda i: (0, 0)
    return pl.pallas_call(
        _even_proj_kernel,
        grid=(n_tiles,),
        in_specs=[pl.BlockSpec((tm, d), lambda i: (i, 0)),
                  pl.BlockSpec((1, d), const),
                  _mod_spec(d, 0, row), _mod_spec(d, 1, row),
                  pl.BlockSpec((d, n_in), const),
                  pl.BlockSpec((1, Q_RANK), const),
                  pl.BlockSpec(wq.shape, const),
                  pl.BlockSpec((1, KV_RANK), const),
                  pl.BlockSpec(wkv.shape, const),
                  pl.BlockSpec((tm, LANES), lambda i: (tab_idx(i), 0)),
                  pl.BlockSpec((tm, LANES), lambda i: (tab_idx(i), 0))],
        out_specs=[pl.BlockSpec((tm, 3 * CONV_CH), lambda i: (i, 0)),
                   pl.BlockSpec((tm, hw), lambda i: (i, 0)),
                   pl.BlockSpec((tm, hw), lambda i: (i, 0)),
                   pl.BlockSpec((tm, MLA_HEADS * V_HEAD), lambda i: (i, 0))],
        out_shape=[jax.ShapeDtypeStruct((t, 3 * CONV_CH), BF16),
                   jax.ShapeDtypeStruct((t, hw), BF16),
                   jax.ShapeDtypeStruct((t, hw), BF16),
                   jax.ShapeDtypeStruct((t, MLA_HEADS * V_HEAD), BF16)],
        compiler_params=_cparams("arbitrary"),
        name="even_proj",
    )(x2, g, mods, mods, w_in, qg, wq, kvg, wkv, cos_t, sin_t)


def _shifted(z, prev_row, next_row):
    tm = z.shape[0]
    ridx = lax.broadcasted_iota(jnp.int32, z.shape, 0)
    zm = jnp.where(ridx == 0, prev_row, pltpu.roll(z, 1, 0))
    zp = jnp.where(ridx == tm - 1, next_row, pltpu.roll(z, tm - 1, 0))
    return zm, zp


def _seq_edges(i, n_ctx_tiles, ctx_tps, x_tps):
    j = jnp.where(i < n_ctx_tiles, i % ctx_tps, (i - n_ctx_tiles) % x_tps)
    tps = jnp.where(i < n_ctx_tiles, ctx_tps, x_tps)
    keep_prev = (j != 0).astype(F32)
    keep_next = (j != tps - 1).astype(F32)
    return keep_prev, keep_next


def _even_conv_kernel(cur_ref, prev_ref, next_ref, w_ref, o_ref, *, n_ctx_tiles, ctx_tps, x_tps):
    i = pl.program_id(0)
    keep_prev, keep_next = _seq_edges(i, n_ctx_tiles, ctx_tps, x_tps)
    c = CONV_CH
    cur = cur_ref[...].astype(F32)
    z = cur[:, c:2 * c] * cur[:, 2 * c:]
    pr = prev_ref[...].astype(F32)
    nx = next_ref[...].astype(F32)
    zprev = (pr[HALO_ROWS - 1:, c:2 * c] * pr[HALO_ROWS - 1:, 2 * c:]) * keep_prev
    znext = (nx[0:1, c:2 * c] * nx[0:1, 2 * c:]) * keep_next
    zm, zp = _shifted(z, zprev, znext)
    w = w_ref[...]
    y = zm * w[0:1] + z * w[1:2] + zp * w[2:3]
    o_ref[...] = (cur[:, :c] * y).astype(o_ref.dtype)


HALO_ROWS = 16


def _halo_specs(tm, width, n_rows, colmap):
    nb = n_rows // HALO_ROWS
    r = tm // HALO_ROWS
    prev = pl.BlockSpec((HALO_ROWS, width), lambda i, *a: (jnp.maximum(i * r - 1, 0), colmap(*a)))
    nxt = pl.BlockSpec((HALO_ROWS, width), lambda i, *a: (jnp.minimum((i + 1) * r, nb - 1), colmap(*a)))
    return prev, nxt


def _even_conv(gbu, conv_w, n_ctx_tiles, ctx_tps, x_tps):
    t, n = gbu.shape
    tm = ROW_TILE
    prev, nxt = _halo_specs(tm, n, t, lambda: 0)
    return pl.pallas_call(
        functools.partial(_even_conv_kernel, n_ctx_tiles=n_ctx_tiles, ctx_tps=ctx_tps, x_tps=x_tps),
        grid=(t // tm,),
        in_specs=[pl.BlockSpec((tm, n), lambda i: (i, 0)), prev, nxt,
                  pl.BlockSpec(conv_w.shape, lambda i: (0, 0))],
        out_specs=pl.BlockSpec((tm, CONV_CH), lambda i: (i, 0)),
        out_shape=jax.ShapeDtypeStruct((t, CONV_CH), BF16),
        compiler_params=_cparams("arbitrary"),
        name="even_conv",
    )(gbu, gbu, gbu, conv_w)


def _attn_kernel(q_ref, kc_ref, kx_ref, vc_ref, vx_ref, o_ref, *, n_ctx_q):
    qi = pl.program_id(1)

    def run(use_x):
        for hp in range(MLA_HEADS // 2):
            outs = []
            for h in (2 * hp, 2 * hp + 1):
                ks = slice(h * LANES, (h + 1) * LANES)
                vs = slice(h * V_HEAD, (h + 1) * V_HEAD)
                q = q_ref[:, ks]
                sc = _dot_nt(q, kc_ref[:, ks])
                m = jnp.max(sc, axis=1, keepdims=True)
                if use_x:
                    sx = _dot_nt(q, kx_ref[:, ks])
                    m = jnp.maximum(m, jnp.max(sx, axis=1, keepdims=True))
                pc = jnp.exp(sc - m)
                l = jnp.sum(pc, axis=1, keepdims=True)
                o = _dot(pc.astype(BF16), vc_ref[:, vs])
                if use_x:
                    px = jnp.exp(sx - m)
                    l = l + jnp.sum(px, axis=1, keepdims=True)
                    o = o + _dot(px.astype(BF16), vx_ref[:, vs])
                outs.append(o * (1.0 / l))
            o_ref[:, hp * LANES:(hp + 1) * LANES] = jnp.concatenate(outs, axis=1).astype(o_ref.dtype)

    @pl.when(qi < n_ctx_q)
    def _():
        run(False)

    @pl.when(qi >= n_ctx_q)
    def _():
        run(True)


def _attention(q, k, v, batch, ctx_len, seq_len):
    tq = ROW_TILE
    n_ctx_q = ctx_len // tq
    n_x_q = seq_len // tq
    ctx_blocks = batch * ctx_len // tq
    assert (batch * ctx_len) % seq_len == 0
    x_blk0 = batch * ctx_len // seq_len
    hw = MLA_HEADS * LANES
    vw = MLA_HEADS * V_HEAD

    def qrow(b, qi):
        return jnp.where(qi < n_ctx_q, b * n_ctx_q + qi, ctx_blocks + b * n_x_q + (qi - n_ctx_q))

    return pl.pallas_call(
        functools.partial(_attn_kernel, n_ctx_q=n_ctx_q),
        grid=(batch, n_ctx_q + n_x_q),
        in_specs=[pl.BlockSpec((tq, hw), lambda b, qi: (qrow(b, qi), 0)),
                  pl.BlockSpec((ctx_len, hw), lambda b, qi: (b, 0)),
                  pl.BlockSpec((seq_len, hw), lambda b, qi: (x_blk0 + b, 0)),
                  pl.BlockSpec((ctx_len, vw), lambda b, qi: (b, 0)),
                  pl.BlockSpec((seq_len, vw), lambda b, qi: (x_blk0 + b, 0))],
        out_specs=pl.BlockSpec((tq, vw), lambda b, qi: (qrow(b, qi), 0)),
        out_shape=jax.ShapeDtypeStruct((q.shape[0], vw), BF16),
        compiler_params=_cparams("arbitrary", "arbitrary"),
        name="mla_attention",
    )(q, k, k, v, v)


def _resid_mm_kernel(*refs, n_lhs):
    x_ref, gate_ref = refs[0], refs[1]
    lhs = refs[2:2 + n_lhs]
    ws = refs[2 + n_lhs:2 + 2 * n_lhs]
    o_ref = refs[2 + 2 * n_lhs]
    acc = _dot(lhs[0][...], ws[0][...])
    for a_ref, w_ref in zip(lhs[1:], ws[1:]):
        acc = acc + _dot(a_ref[...], w_ref[...])
    o_ref[...] = x_ref[...] + gate_ref[0] * acc


def _resid_mm(x2, mods, k_gate, row, lhs_list, w_list, tm=512):
    t, d = x2.shape
    n = len(lhs_list)
    in_specs = [pl.BlockSpec((tm, d), lambda i: (i, 0)), _mod_spec(d, k_gate, lambda i: row(i * tm // ROW_TILE))]
    in_specs += [pl.BlockSpec((tm, a.shape[1]), lambda i: (i, 0)) for a in lhs_list]
    in_specs += [pl.BlockSpec(w.shape, lambda i: (0, 0)) for w in w_list]
    return pl.pallas_call(
        functools.partial(_resid_mm_kernel, n_lhs=n),
        grid=(t // tm,),
        in_specs=in_specs,
        out_specs=pl.BlockSpec((tm, d), lambda i: (i, 0)),
        out_shape=jax.ShapeDtypeStruct((t, d), F32),
        compiler_params=_cparams("arbitrary"),
        name="resid_matmul",
    )(x2, mods, *lhs_list, *w_list)


def _ffn_up_kernel(x_ref, g_ref, sh_ref, sc_ref, w_ref, o_ref, *, ffn, chunk):
    hn = _rms(x_ref[...], g_ref[...]) * (1.0 + sc_ref[0]) + sh_ref[0]
    hb = hn.astype(BF16)
    for c0 in range(0, ffn, chunk):
        gate = _dot(hb, w_ref[:, c0:c0 + chunk])
        up = _dot(hb, w_ref[:, ffn + c0:ffn + c0 + chunk])
        o_ref[:, c0:c0 + chunk] = (_silu(gate) * up).astype(o_ref.dtype)


def _ffn_up(x2, g, mods, row, w1):
    t, d = x2.shape
    tm = ROW_TILE
    ffn = w1.shape[1] // 2
    chunk = 256
    assert ffn % chunk == 0
    return pl.pallas_call(
        functools.partial(_ffn_up_kernel, ffn=ffn, chunk=chunk),
        grid=(t // tm,),
        in_specs=[pl.BlockSpec((tm, d), lambda i: (i, 0)),
                  pl.BlockSpec((1, d), lambda i: (0, 0)),
                  _mod_spec(d, 3, row), _mod_spec(d, 4, row),
                  pl.BlockSpec(w1.shape, lambda i: (0, 0))],
        out_specs=pl.BlockSpec((tm, ffn), lambda i: (i, 0)),
        out_shape=jax.ShapeDtypeStruct((t, ffn), BF16),
        compiler_params=_cparams("arbitrary"),
        name="ffn_up",
    )(x2, g, mods, mods, w1)


def _odd_proj_kernel(x_ref, g_ref, sh_ref, sc_ref, w_ref, qkv_ref, gate_ref, abt_ref, *, chunk):
    hn = _rms(x_ref[...], g_ref[...]) * (1.0 + sc_ref[0]) + sh_ref[0]
    hb = hn.astype(BF16)
    n_qkv = qkv_ref.shape[1]
    n_gate = gate_ref.shape[1]
    for c0 in range(0, n_qkv, chunk):
        qkv_ref[:, c0:c0 + chunk] = _dot(hb, w_ref[:, c0:c0 + chunk]).astype(qkv_ref.dtype)
    for c0 in range(0, n_gate, chunk):
        gate_ref[:, c0:c0 + chunk] = _dot(hb, w_ref[:, n_qkv + c0:n_qkv + c0 + chunk]).astype(gate_ref.dtype)
    abt_ref[...] = _dot(hb, w_ref[:, n_qkv + n_gate:])


def _odd_proj(x2, g, mods, row, w_in):
    t, d = x2.shape
    tm = ROW_TILE
    n_qkv = 3 * GDN_QK
    n_gate = GDN_QK
    return pl.pallas_call(
        functools.partial(_odd_proj_kernel, chunk=512),
        grid=(t // tm,),
        in_specs=[pl.BlockSpec((tm, d), lambda i: (i, 0)),
                  pl.BlockSpec((1, d), lambda i: (0, 0)),
                  _mod_spec(d, 0, row), _mod_spec(d, 1, row),
                  pl.BlockSpec(w_in.shape, lambda i: (0, 0))],
        out_specs=[pl.BlockSpec((tm, n_qkv), lambda i: (i, 0)),
                   pl.BlockSpec((tm, n_gate), lambda i: (i, 0)),
                   pl.BlockSpec((tm, LANES), lambda i: (i, 0))],
        out_shape=[jax.ShapeDtypeStruct((t, n_qkv), BF16),
                   jax.ShapeDtypeStruct((t, n_gate), BF16),
                   jax.ShapeDtypeStruct((t, LANES), F32)],
        compiler_params=_cparams("arbitrary"),
        name="odd_proj",
    )(x2, g, mods, mods, w_in)


def _odd_conv_kernel(cur_ref, prev_ref, next_ref, w_ref, o_ref, *, n_ctx_tiles, ctx_tps, x_tps, heads_per_block):
    i = pl.program_id(0)
    j = pl.program_id(1)
    keep_prev, keep_next = _seq_edges(i, n_ctx_tiles, ctx_tps, x_tps)
    z = cur_ref[...].astype(F32)
    zprev = prev_ref[HALO_ROWS - 1:, :].astype(F32) * keep_prev
    znext = next_ref[0:1, :].astype(F32) * keep_next
    zm, zp = _shifted(z, zprev, znext)
    w = w_ref[...]
    y = _silu(zm * w[0:1] + z * w[1:2] + zp * w[2:3])
    blocks_per_part = GDN_HEADS // heads_per_block
    is_qk = j < 2 * blocks_per_part
    is_q = j < blocks_per_part
    for hh in range(heads_per_block):
        yh = y[:, hh * LANES:(hh + 1) * LANES]
        nrm = lax.rsqrt(jnp.sum(yh * yh, axis=-1, keepdims=True) + NORM_EPS)
        nrm = jnp.where(is_q, nrm * (GDN_DK ** -0.5), nrm)
        o_ref[hh] = jnp.where(is_qk, yh * nrm, yh)


def _odd_conv(qkv, conv_w, n_ctx_tiles, ctx_tps, x_tps):
    t, n = qkv.shape
    tm = ROW_TILE
    cb = 512
    hpb = cb // LANES
    prev, nxt = _halo_specs(tm, cb, t, lambda j: j)
    return pl.pallas_call(
        functools.partial(_odd_conv_kernel, n_ctx_tiles=n_ctx_tiles, ctx_tps=ctx_tps, x_tps=x_tps,
                          heads_per_block=hpb),
        grid=(t // tm, n // cb),
        in_specs=[pl.BlockSpec((tm, cb), lambda i, j: (i, j)), prev, nxt,
                  pl.BlockSpec((conv_w.shape[0], cb), lambda i, j: (0, j))],
        out_specs=pl.BlockSpec((hpb, tm, LANES), lambda i, j: (j, i, 0)),
        out_shape=jax.ShapeDtypeStruct((n // LANES, t, LANES), F32),
        compiler_params=_cparams("arbitrary", "arbitrary"),
        name="odd_conv",
    )(qkv, qkv, qkv, conv_w)


def _gdn_kernel(qkv_ref, abt_ref, alog_ref, dtb_ref, o_ref,
                s_ref, gc_s, beta_s, gtot_s, uw_s, qg_s, kd_s, attn_s, vnew_s, gtc_s):
    d = pl.program_id(1)
    step = pl.program_id(2)
    n = GDN_STEP
    c = GDN_CHUNK
    nchunk = n // c
    h_n = GDN_HEADS
    sgn = 1 - 2 * d

    @pl.when(step == 0)
    def _():
        s_ref[...] = jnp.zeros(s_ref.shape, s_ref.dtype)

    ri = lax.broadcasted_iota(jnp.int32, (n, n), 0)
    ci = lax.broadcasted_iota(jnp.int32, (n, n), 1)
    same = (ri // c) == (ci // c)
    rel = (ri - ci) * sgn
    incl = same & (rel >= 0)
    strict = same & (rel > 0)
    eye = (ri == ci).astype(F32)

    hrow = pl.ds(d * h_n, h_n)
    a_all = abt_ref[0, hrow, :]
    bt_all = abt_ref[0, pl.ds(2 * h_n + d * h_n, h_n), :]
    sp_in = a_all + dtb_ref[hrow, :]
    softplus = jnp.maximum(sp_in, 0.0) + jnp.log(1.0 + jnp.exp(-jnp.abs(sp_in)))
    g_all = -jnp.exp(alog_ref[hrow, :]) * softplus
    beta_s[...] = 1.0 / (1.0 + jnp.exp(-bt_all))
    cum_m = jnp.where(same & ((ci - ri) * sgn >= 0), 1.0, 0.0).astype(BF16)
    gc_s[...] = _dot_f32_exact_rhs(g_all, cum_m)
    gtot_s[...] = _dot_f32_exact_rhs(g_all, same.astype(BF16))

    def head(h, carry):
        q = qkv_ref[h]
        k = qkv_ref[h_n + h]
        v = qkv_ref[2 * h_n + h]
        gc_row = gc_s[pl.ds(h, 1), :]
        gc_col = jnp.sum(eye * gc_row, axis=1, keepdims=True)
        beta_col = jnp.sum(eye * beta_s[pl.ds(h, 1), :], axis=1, keepdims=True)
        gtot_col = jnp.sum(eye * gtot_s[pl.ds(h, 1), :], axis=1, keepdims=True)
        gtc_s[...] = jnp.broadcast_to(gtot_col, gtc_s.shape)
        decay = jnp.exp(jnp.where(incl, gc_col - gc_row, -1e30))
        kb = k * beta_col
        k16 = k.astype(BF16)
        a_mat = jnp.where(strict, _dot_nt(kb.astype(BF16), k16) * decay, 0.0)
        attn_s[...] = (_dot_nt(q.astype(BF16), k16) * decay).astype(attn_s.dtype)
        p = -a_mat
        tinv = eye + p
        for _ in range((c - 1).bit_length() - 1):
            p16 = p.astype(BF16)
            p = _dot(p16, p16)
            tinv = tinv + _dot(tinv.astype(BF16), p.astype(BF16))
        egc = jnp.exp(gc_col)
        rhs = jnp.concatenate([v * beta_col, kb * egc], axis=1).astype(BF16)
        uw_s[...] = _dot(tinv.astype(BF16), rhs)
        qg_s[...] = q * egc
        kd_s[...] = k * jnp.exp(gtot_col - gc_col)
        vnew_s[...] = jnp.zeros(vnew_s.shape, vnew_s.dtype)
        s = s_ref[h]
        for cstep in range(nchunk):
            cc = jnp.where(d == 0, cstep, nchunk - 1 - cstep)
            rows = pl.ds(pl.multiple_of(cc * c, c), c)
            wq = jnp.concatenate([uw_s[rows, LANES:], qg_s[rows, :]], axis=0).astype(BF16)
            ws_qs = _dot(wq, s.astype(BF16))
            v_new = uw_s[rows, :LANES] - ws_qs[:c]
            vnew_s[rows, :] = v_new.astype(vnew_s.dtype)
            o = ws_qs[c:] + _dot(attn_s[rows, :], vnew_s[...])
            o_ref[0, h, rows, :] = o
            gend = jnp.exp(gtc_s[pl.ds(pl.multiple_of(cc * c, c), 1), :])
            s = s * gend + _dot_tn(kd_s[rows, :].astype(BF16), v_new.astype(BF16))
        s_ref[h] = s
        return carry

    lax.fori_loop(0, h_n, head, 0)


def _gdn(qkvn, abt_chunks, alog_b, dtb_b, batch, ctx_len, seq_len):
    nh3, t, _ = qkvn.shape
    n = GDN_STEP
    nc = ctx_len // n
    nx = seq_len // n
    ctx_blocks = batch * nc

    def rowblk(b, d, s):
        sc = jnp.where(d == 0, s, nc - 1 - s)
        sx = jnp.where(d == 0, s - nc, nx - 1 - (s - nc))
        return jnp.where(s < nc, b * nc + sc, ctx_blocks + b * nx + sx)

    vm = pltpu.VMEM
    return pl.pallas_call(
        _gdn_kernel,
        grid=(batch, 2, nc + nx),
        in_specs=[pl.BlockSpec((nh3, n, LANES), lambda b, d, s: (0, rowblk(b, d, s), 0)),
                  pl.BlockSpec((1, 4 * GDN_HEADS, n), lambda b, d, s: (rowblk(b, d, s), 0, 0)),
                  pl.BlockSpec(alog_b.shape, lambda b, d, s: (0, 0)),
                  pl.BlockSpec(dtb_b.shape, lambda b, d, s: (0, 0))],
        out_specs=pl.BlockSpec((1, GDN_HEADS, n, LANES), lambda b, d, s: (d, 0, rowblk(b, d, s), 0)),
        out_shape=jax.ShapeDtypeStruct((2, GDN_HEADS, t, LANES), F32),
        scratch_shapes=[vm((GDN_HEADS, GDN_DK, LANES), F32),
                        vm((GDN_HEADS, n), F32),
                        vm((GDN_HEADS, n), F32),
                        vm((GDN_HEADS, n), F32),
                        vm((n, 2 * LANES), F32),
                        vm((n, LANES), F32),
                        vm((n, LANES), F32),
                        vm((n, n), BF16),
                        vm((n, LANES), BF16),
                        vm((n, LANES), F32)],
        compiler_params=_cparams("arbitrary", "arbitrary", "arbitrary"),
        name="gated_delta",
    )(qkvn, abt_chunks, alog_b, dtb_b)


def _odd_out_kernel(x_ref, gate_ref, o_ref_in, z_ref, og_ref, w_ref, out_ref):
    parts = []
    og = og_ref[...]
    for h in range(GDN_HEADS):
        o = o_ref_in[0, h] + o_ref_in[1, h]
        y = _rms(o, og) * _silu(z_ref[:, h * LANES:(h + 1) * LANES].astype(F32))
        parts.append(y.astype(BF16))
    y_all = jnp.concatenate(parts, axis=1)
    out_ref[...] = x_ref[...] + gate_ref[0] * _dot(y_all, w_ref[...])


def _odd_out(x2, mods, row, o_dirs, z, o_norm_g, w_out, row_off, n_rows):
    d = x2.shape[1]
    tm = ROW_TILE
    return pl.pallas_call(
        _odd_out_kernel,
        grid=(n_rows // tm,),
        in_specs=[pl.BlockSpec((tm, d), lambda i: (i + row_off, 0)),
                  _mod_spec(d, 2, row, row_off),
                  pl.BlockSpec((2, GDN_HEADS, tm, LANES), lambda i: (0, 0, i + row_off, 0)),
                  pl.BlockSpec((tm, z.shape[1]), lambda i: (i + row_off, 0)),
                  pl.BlockSpec((1, LANES), lambda i: (0, 0)),
                  pl.BlockSpec(w_out.shape, lambda i: (0, 0))],
        out_specs=pl.BlockSpec((tm, d), lambda i: (i, 0)),
        out_shape=jax.ShapeDtypeStruct((n_rows, d), F32),
        compiler_params=_cparams("arbitrary"),
        name="odd_out",
    )(x2, mods, o_dirs, z, o_norm_g, w_out)


def _router_kernel(x_ref, g_ref, sh_ref, sc_ref, rw_ref, h_ref, info_ref, cnt_ref, carry_s):
    i = pl.program_id(0)

    @pl.when(i == 0)
    def _():
        carry_s[...] = jnp.zeros(carry_s.shape, carry_s.dtype)

    hn = _rms(x_ref[...], g_ref[...]) * (1.0 + sc_ref[0]) + sh_ref[0]
    h_ref[...] = hn
    tm = hn.shape[0]
    lane = lax.broadcasted_iota(jnp.int32, (tm, LANES), 1)
    logits = jnp.where(lane < N_EXPERTS, _dot_f32(hn, rw_ref[...]), -jnp.inf)
    m1 = jnp.max(logits, axis=1, keepdims=True)
    i1 = jnp.min(jnp.where(logits == m1, lane, LANES), axis=1, keepdims=True)
    rest = jnp.where(lane == i1, -jnp.inf, logits)
    m2 = jnp.max(rest, axis=1, keepdims=True)
    i2 = jnp.min(jnp.where(rest == m2, lane, LANES), axis=1, keepdims=True)
    e2 = jnp.exp(m2 - m1)
    w1 = 1.0 / (1.0 + e2)
    w2 = e2 / (1.0 + e2)
    oh1 = (lane == i1).astype(F32)
    oh2 = (lane == i2).astype(F32)
    cnt = oh1 + oh2
    ri = lax.broadcasted_iota(jnp.int32, (tm, tm), 0)
    ci = lax.broadcasted_iota(jnp.int32, (tm, tm), 1)
    before = (ci < ri).astype(BF16)
    pre = _dot(before, cnt.astype(BF16)) + carry_s[0:1, :]
    r1 = jnp.sum(pre * oh1, axis=1, keepdims=True)
    r2 = jnp.sum(pre * oh2, axis=1, keepdims=True)
    info = jnp.where(lane == 0, i1.astype(F32), 0.0)
    info = jnp.where(lane == 1, i2.astype(F32), info)
    info = jnp.where(lane == 2, w1, info)
    info = jnp.where(lane == 3, w2, info)
    info = jnp.where(lane == 4, r1, info)
    info = jnp.where(lane == 5, r2, info)
    info_ref[...] = info
    total = carry_s[0:1, :] + jnp.sum(cnt, axis=0, keepdims=True)
    carry_s[...] = jnp.broadcast_to(total, carry_s.shape)
    cnt_ref[...] = jnp.broadcast_to(total, cnt_ref.shape)


def _router(x2, g, mods, row, router_w_pad):
    t, d = x2.shape
    tm = ROW_TILE
    return pl.pallas_call(
        _router_kernel,
        grid=(t // tm,),
        in_specs=[pl.BlockSpec((tm, d), lambda i: (i, 0)),
                  pl.BlockSpec((1, d), lambda i: (0, 0)),
                  _mod_spec(d, 3, row), _mod_spec(d, 4, row),
                  pl.BlockSpec(router_w_pad.shape, lambda i: (0, 0))],
        out_specs=[pl.BlockSpec((tm, d), lambda i: (i, 0)),
                   pl.BlockSpec((tm, LANES), lambda i: (i, 0)),
                   pl.BlockSpec((8, LANES), lambda i: (0, 0))],
        out_shape=[jax.ShapeDtypeStruct((t, d), F32),
                   jax.ShapeDtypeStruct((t, LANES), F32),
                   jax.ShapeDtypeStruct((8, LANES), F32)],
        scratch_shapes=[pltpu.VMEM((8, LANES), F32)],
        compiler_params=_cparams("arbitrary"),
        name="moe_router",
    )(x2, g, mods, mods, router_w_pad)


def _row_copy(src_hbm, src_row, dst_ref, dst_row, sem):
    return pltpu.make_async_copy(src_hbm.at[pl.ds(src_row, 1)], dst_ref.at[pl.ds(dst_row, 1)], sem)


def _dispatch_kernel(dest_ref, h_hbm, init_hbm, hs_hbm, sem, *, tm):
    del init_hbm
    i = pl.program_id(0)
    base = i * tm

    def issue(r, carry):
        _row_copy(h_hbm, base + r, hs_hbm, dest_ref[0, 0, r], sem).start()
        _row_copy(h_hbm, base + r, hs_hbm, dest_ref[0, 0, tm + r], sem).start()
        return carry

    lax.fori_loop(0, tm, issue, 0)

    def drain(r, carry):
        _row_copy(h_hbm, base, hs_hbm, 0, sem).wait()
        _row_copy(h_hbm, base, hs_hbm, 0, sem).wait()
        return carry

    lax.fori_loop(0, tm, drain, 0)


def _dispatch(h, dest_tiles, n_sorted_rows):
    t, d = h.shape
    tm = ROW_TILE
    init = jnp.zeros((n_sorted_rows, d), h.dtype)
    return pl.pallas_call(
        functools.partial(_dispatch_kernel, tm=tm),
        grid=(t // tm,),
        in_specs=[pl.BlockSpec((1, 1, 2 * tm), lambda i: (i, 0, 0), memory_space=pltpu.SMEM),
                  pl.BlockSpec(memory_space=pl.ANY),
                  pl.BlockSpec(memory_space=pl.ANY)],
        out_specs=pl.BlockSpec(memory_space=pl.ANY),
        out_shape=jax.ShapeDtypeStruct((n_sorted_rows, d), h.dtype),
        scratch_shapes=[pltpu.SemaphoreType.DMA(())],
        input_output_aliases={2: 0},
        compiler_params=_cparams("arbitrary"),
        name="moe_dispatch",
    )(dest_tiles, h, init)


def _expert_kernel(te_ref, nt_ref, hs_ref, wg_ref, wu_ref, w2_ref, ys_ref, acc_s):
    j = pl.program_id(0)
    f = pl.program_id(1)
    nf = pl.num_programs(1)

    @pl.when(j < nt_ref[0])
    def _():
        hb = hs_ref[...].astype(BF16)
        gate = _dot(hb, wg_ref[...])
        up = _dot(hb, wu_ref[...])
        part = _dot((_silu(gate) * up).astype(BF16), w2_ref[...])

        @pl.when(f == 0)
        def _():
            acc_s[...] = part

        @pl.when(f > 0)
        def _():
            acc_s[...] = acc_s[...] + part

        @pl.when(f == nf - 1)
        def _():
            ys_ref[...] = acc_s[...]

    @pl.when((j >= nt_ref[0]) & (f == nf - 1))
    def _():
        ys_ref[...] = jnp.zeros(ys_ref.shape, ys_ref.dtype)


def _experts(hs, tile_expert, n_tiles_used, w1, w2):
    r, d = hs.shape
    tm = EXPERT_TILE
    tf = EXPERT_FCHUNK
    ffn = w2.shape[1]
    nf = ffn // tf
    assert ffn % tf == 0

    def tile(j, nt):
        return jnp.minimum(j, nt[0] - 1)

    def fchunk(j, f, nt):
        return jnp.where(j < nt[0], f, nf - 1)

    grid_spec = pltpu.PrefetchScalarGridSpec(
        num_scalar_prefetch=2,
        grid=(r // tm, nf),
        in_specs=[pl.BlockSpec((tm, d), lambda j, f, te, nt: (tile(j, nt), 0)),
                  pl.BlockSpec((None, d, tf), lambda j, f, te, nt: (te[tile(j, nt)], 0, fchunk(j, f, nt))),
                  pl.BlockSpec((None, d, tf), lambda j, f, te, nt: (te[tile(j, nt)], 0, nf + fchunk(j, f, nt))),
                  pl.BlockSpec((None, tf, d), lambda j, f, te, nt: (te[tile(j, nt)], fchunk(j, f, nt), 0))],
        out_specs=pl.BlockSpec((tm, d), lambda j, f, te, nt: (j, 0)),
        scratch_shapes=[pltpu.VMEM((tm, d), F32)],
    )
    return pl.pallas_call(
        _expert_kernel,
        grid_spec=grid_spec,
        out_shape=jax.ShapeDtypeStruct((r, d), F32),
        compiler_params=_cparams("arbitrary", "arbitrary"),
        name="moe_experts",
    )(tile_expert, n_tiles_used, hs, w1, w1, w2)


def _combine_kernel(dest_ref, x_ref, gate_ref, info_ref, ys_hbm, o_ref, y1_s, y2_s, sem, *, tm):
    def issue(r, carry):
        _row_copy(ys_hbm, dest_ref[0, 0, r], y1_s, r, sem).start()
        _row_copy(ys_hbm, dest_ref[0, 0, tm + r], y2_s, r, sem).start()
        return carry

    lax.fori_loop(0, tm, issue, 0)

    def drain(r, carry):
        _row_copy(ys_hbm, 0, y1_s, 0, sem).wait()
        _row_copy(ys_hbm, 0, y2_s, 0, sem).wait()
        return carry

    lax.fori_loop(0, tm, drain, 0)
    info = info_ref[...]
    y = info[:, 2:3] * y1_s[...] + info[:, 3:4] * y2_s[...]
    o_ref[...] = x_ref[...] + gate_ref[0] * y


def _combine(x2, mods, row, info, dest_tiles, ys):
    t, d = x2.shape
    tm = ROW_TILE
    return pl.pallas_call(
        functools.partial(_combine_kernel, tm=tm),
        grid=(t // tm,),
        in_specs=[pl.BlockSpec((1, 1, 2 * tm), lambda i: (i, 0, 0), memory_space=pltpu.SMEM),
                  pl.BlockSpec((tm, d), lambda i: (i, 0)),
                  _mod_spec(d, 5, row),
                  pl.BlockSpec((tm, LANES), lambda i: (i, 0)),
                  pl.BlockSpec(memory_space=pl.ANY)],
        out_specs=pl.BlockSpec((tm, d), lambda i: (i, 0)),
        out_shape=jax.ShapeDtypeStruct((t, d), F32),
        scratch_shapes=[pltpu.VMEM((tm, d), F32), pltpu.VMEM((tm, d), F32), pltpu.SemaphoreType.DMA(())],
        compiler_params=_cparams("arbitrary"),
        name="moe_combine",
    )(dest_tiles, x2, mods, info, ys)


def _moe(x2, g, mods, row, router_w, w1, w2):
    t, d = x2.shape
    tm = ROW_TILE
    te = EXPERT_TILE
    rw = jnp.pad(router_w, ((0, 0), (0, LANES - N_EXPERTS)))
    h, info, counts = _router(x2, g, mods, row, rw)
    cnt = counts[0, :N_EXPERTS].astype(jnp.int32)
    gsz = ((cnt + te - 1) // te) * te
    ends = jnp.cumsum(gsz)
    offs = ends - gsz
    e1 = info[:, 0].astype(jnp.int32)
    e2 = info[:, 1].astype(jnp.int32)
    dest1 = offs[e1] + info[:, 4].astype(jnp.int32)
    dest2 = offs[e2] + info[:, 5].astype(jnp.int32)
    dest_tiles = jnp.concatenate([dest1.reshape(t // tm, 1, tm), dest2.reshape(t // tm, 1, tm)], axis=2)
    n_sorted = 2 * t + N_EXPERTS * te
    n_tiles_max = n_sorted // te
    starts = jnp.arange(n_tiles_max, dtype=jnp.int32) * te
    tile_expert = jnp.minimum(jnp.sum((starts[:, None] >= ends[None, :]).astype(jnp.int32), axis=1), N_EXPERTS - 1)
    n_used = (ends[-1] // te).astype(jnp.int32).reshape(1)
    hs = _dispatch(h, dest_tiles, n_sorted)
    ys = _experts(hs, tile_expert, n_used, w1, w2)
    return _combine(x2, mods, row, info, dest_tiles, ys)


def _final_norm_kernel(x_ref, g_ref, o_ref):
    o_ref[...] = _rms(x_ref[...], g_ref[...])


def _final_norm(x2, g, row_off, n_rows):
    d = x2.shape[1]
    tm = 512
    off = row_off * ROW_TILE // tm
    return pl.pallas_call(
        _final_norm_kernel,
        grid=(n_rows // tm,),
        in_specs=[pl.BlockSpec((tm, d), lambda i: (i + off, 0)), pl.BlockSpec((1, d), lambda i: (0, 0))],
        out_specs=pl.BlockSpec((tm, d), lambda i: (i, 0)),
        out_shape=jax.ShapeDtypeStruct((n_rows, d), F32),
        compiler_params=_cparams("arbitrary"),
        name="final_norm",
    )(x2, g)


_ROPE_SWAP = tuple(list(range(8, 16)) + list(range(0, 8)) + list(range(24, 32)) + list(range(16, 24)))


def _even_weights(w_in, w_uq, w_ukv):
    d = w_in.shape[0]
    n0 = 3 * CONV_CH + Q_RANK + KV_RANK
    kr = w_in[:, n0:n0 + QK_ROPE]
    kr_sw = kr[:, jnp.array(_ROPE_SWAP)]
    z = lambda n: jnp.zeros((d, n), w_in.dtype)
    w_in_x = jnp.concatenate([w_in[:, :n0], z(QK_NOPE), kr, z(LANES - QK_NOPE - QK_ROPE),
                              z(QK_NOPE), kr_sw, z(LANES - QK_NOPE - QK_ROPE)], axis=1).astype(BF16)
    uq = w_uq.reshape(Q_RANK, MLA_HEADS, QK_NOPE + QK_ROPE)
    pad = LANES - QK_NOPE - QK_ROPE
    uq_plain = jnp.pad(uq, ((0, 0), (0, 0), (0, pad))).reshape(Q_RANK, MLA_HEADS * LANES)
    uq_rope_sw = uq[:, :, QK_NOPE:][:, :, jnp.array(_ROPE_SWAP)]
    uq_sw = jnp.pad(uq_rope_sw, ((0, 0), (0, 0), (QK_NOPE, pad))).reshape(Q_RANK, MLA_HEADS * LANES)
    wq = jnp.concatenate([uq_plain, uq_sw], axis=1).astype(BF16)
    ukv = w_ukv.reshape(KV_RANK, MLA_HEADS, QK_NOPE + V_HEAD)
    k_part = jnp.pad(ukv[:, :, :QK_NOPE], ((0, 0), (0, 0), (0, LANES - QK_NOPE))).reshape(KV_RANK, MLA_HEADS * LANES)
    v_part = ukv[:, :, QK_NOPE:].reshape(KV_RANK, MLA_HEADS * V_HEAD)
    wkv = jnp.concatenate([k_part, v_part], axis=1).astype(BF16)
    return w_in_x, wq, wkv


def _rope_tables(seq_len, ctx_rows):
    n = QK_ROPE // 4
    t = jnp.arange(seq_len)
    inv = ROPE_BASE ** (-jnp.arange(n, dtype=F32) / n)
    ang_r = (t // GRID_W).astype(F32)[:, None] * inv
    ang_c = (t % GRID_W).astype(F32)[:, None] * inv
    cr, sr, cc, sc = jnp.cos(ang_r), jnp.sin(ang_r), jnp.cos(ang_c), jnp.sin(ang_c)
    cos32 = jnp.concatenate([cr, cr, cc, cc], axis=1)
    sin32 = jnp.concatenate([-sr, sr, -sc, sc], axis=1)
    pad = LANES - QK_NOPE - QK_ROPE
    cos_x = jnp.concatenate([jnp.ones((seq_len, QK_NOPE), F32), cos32, jnp.zeros((seq_len, pad), F32)], axis=1)
    sin_x = jnp.concatenate([jnp.zeros((seq_len, QK_NOPE), F32), sin32, jnp.zeros((seq_len, pad), F32)], axis=1)
    cos_c = jnp.concatenate([jnp.ones((ctx_rows, QK_NOPE + QK_ROPE), F32), jnp.zeros((ctx_rows, pad), F32)], axis=1)
    sin_c = jnp.zeros((ctx_rows, LANES), F32)
    return jnp.concatenate([cos_c, cos_x], axis=0), jnp.concatenate([sin_c, sin_x], axis=0)


def _even_layer(xa, mods, dims, tabs, p):
    batch, ctx_len, seq_len = dims
    tm = ROW_TILE
    t = xa.shape[0]
    n_ctx_tiles = batch * ctx_len // tm
    x_tps = seq_len // tm
    row = _mod_row_fn(n_ctx_tiles, x_tps, batch)
    w_in_x, wq, wkv = _even_weights(p["w_in"], p["w_uq"], p["w_ukv"])
    cos_t, sin_t = tabs
    tab_idx = lambda i: jnp.where(i < n_ctx_tiles, 0, 1 + (i - n_ctx_tiles) % x_tps)
    gbu, q, k, v = _even_proj(xa, p["norm1_g"].reshape(1, -1), mods, row, t // tm, w_in_x,
                              p["q_norm_g"].reshape(1, -1), wq, p["kv_norm_g"].reshape(1, -1), wkv,
                              cos_t, sin_t, tab_idx)
    conv_y = _even_conv(gbu, p["conv_w"], n_ctx_tiles, ctx_len // tm, x_tps)
    att = _attention(q, k, v, batch, ctx_len, seq_len)
    w_out = p["w_out"].astype(BF16)
    xa = _resid_mm(xa, mods, 2, row, [conv_y, att], [w_out[:CONV_CH], w_out[CONV_CH:]])
    hmid = _ffn_up(xa, p["norm2_g"].reshape(1, -1), mods, row, p["ffn_w1"].astype(BF16))
    return _resid_mm(xa, mods, 5, row, [hmid], [p["ffn_w2"].astype(BF16)])


def _odd_layer(xa, mods, dims, p, ctx_out):
    batch, ctx_len, seq_len = dims
    tm = ROW_TILE
    t = xa.shape[0]
    n_ctx_tiles = batch * ctx_len // tm
    x_tps = seq_len // tm
    row = _mod_row_fn(n_ctx_tiles, x_tps, batch)
    n_w = p["w_in"].shape[1]
    w_in = jnp.pad(p["w_in"], ((0, 0), (0, 4 * GDN_QK + LANES - n_w))).astype(BF16)
    qkv, z, abt = _odd_proj(xa, p["norm1_g"].reshape(1, -1), mods, row, w_in)
    qkvn = _odd_conv(qkv, p["qkv_conv_w"], n_ctx_tiles, ctx_len // tm, x_tps)
    abt_chunks = abt[:, :4 * GDN_HEADS].reshape(t // GDN_STEP, GDN_STEP, 4 * GDN_HEADS).transpose(0, 2, 1)
    alog_b = jnp.broadcast_to(p["a_log"].reshape(2 * GDN_HEADS, 1), (2 * GDN_HEADS, GDN_STEP))
    dtb_b = jnp.broadcast_to(p["dt_bias"].reshape(2 * GDN_HEADS, 1), (2 * GDN_HEADS, GDN_STEP))
    o_dirs = _gdn(qkvn, abt_chunks, alog_b, dtb_b, batch, ctx_len, seq_len)
    row_off = 0 if ctx_out else n_ctx_tiles
    n_rows = t - row_off * tm
    xo = _odd_out(xa, mods, row, o_dirs, z, p["o_norm_g"].reshape(1, -1), p["w_out"].astype(BF16), row_off, n_rows)
    row_o = row if ctx_out else _mod_row_fn(0, x_tps, batch)
    return _moe(xo, p["norm2_g"].reshape(1, -1), mods, row_o, p["router_w"],
                p["moe_w1"].astype(BF16), p["moe_w2"].astype(BF16))


_EVEN_NAMES = ("mod_w", "mod_b", "norm1_g", "w_in", "conv_w", "q_norm_g", "w_uq", "kv_norm_g", "w_ukv", "w_out",
               "norm2_g", "ffn_w1", "ffn_w2")
_ODD_NAMES = ("mod_w", "mod_b", "norm1_g", "w_in", "qkv_conv_w", "a_log", "dt_bias", "o_norm_g", "w_out",
              "norm2_g", "router_w", "moe_w1", "moe_w2")


def kernel(x, c, ctx, c_ctx, l0_mod_w, l0_mod_b, l0_norm1_g, l0_w_in, l0_conv_w, l0_q_norm_g, l0_w_uq, l0_kv_norm_g, l0_w_ukv, l0_w_out, l0_norm2_g, l0_ffn_w1, l0_ffn_w2, l1_mod_w, l1_mod_b, l1_norm1_g, l1_w_in, l1_qkv_conv_w, l1_a_log, l1_dt_bias, l1_o_norm_g, l1_w_out, l1_norm2_g, l1_router_w, l1_moe_w1, l1_moe_w2, l2_mod_w, l2_mod_b, l2_norm1_g, l2_w_in, l2_conv_w, l2_q_norm_g, l2_w_uq, l2_kv_norm_g, l2_w_ukv, l2_w_out, l2_norm2_g, l2_ffn_w1, l2_ffn_w2, l3_mod_w, l3_mod_b, l3_norm1_g, l3_w_in, l3_qkv_conv_w, l3_a_log, l3_dt_bias, l3_o_norm_g, l3_w_out, l3_norm2_g, l3_router_w, l3_moe_w1, l3_moe_w2, final_norm_g):
    batch, seq_len, d = x.shape
    ctx_len = ctx.shape[1]
    assert seq_len % ROW_TILE == 0 and ctx_len % ROW_TILE == 0 and seq_len % GRID_W == 0
    dims = (batch, ctx_len, seq_len)
    layers = (
        dict(zip(_EVEN_NAMES, (l0_mod_w, l0_mod_b, l0_norm1_g, l0_w_in, l0_conv_w, l0_q_norm_g, l0_w_uq,
                               l0_kv_norm_g, l0_w_ukv, l0_w_out, l0_norm2_g, l0_ffn_w1, l0_ffn_w2))),
        dict(zip(_ODD_NAMES, (l1_mod_w, l1_mod_b, l1_norm1_g, l1_w_in, l1_qkv_conv_w, l1_a_log, l1_dt_bias,
                              l1_o_norm_g, l1_w_out, l1_norm2_g, l1_router_w, l1_moe_w1, l1_moe_w2))),
        dict(zip(_EVEN_NAMES, (l2_mod_w, l2_mod_b, l2_norm1_g, l2_w_in, l2_conv_w, l2_q_norm_g, l2_w_uq,
                               l2_kv_norm_g, l2_w_ukv, l2_w_out, l2_norm2_g, l2_ffn_w1, l2_ffn_w2))),
        dict(zip(_ODD_NAMES, (l3_mod_w, l3_mod_b, l3_norm1_g, l3_w_in, l3_qkv_conv_w, l3_a_log, l3_dt_bias,
                              l3_o_norm_g, l3_w_out, l3_norm2_g, l3_router_w, l3_moe_w1, l3_moe_w2))),
    )
    xa = jnp.concatenate([ctx.reshape(batch * ctx_len, d), x.reshape(batch * seq_len, d)], axis=0)
    mod_rows = 8 * ((batch + 1 + 7) // 8)
    cc = jnp.zeros((mod_rows, d), F32).at[:batch].set(c).at[batch].set(c_ctx)
    tabs = _rope_tables(seq_len, ROW_TILE)
    n_layers = len(layers)
    for li, p in enumerate(layers):
        mods = _mods(cc, p["mod_w"], p["mod_b"])
        if li % 2 == 0:
            xa = _even_layer(xa, mods, dims, tabs, p)
        else:
            xa = _odd_layer(xa, mods, dims, p, ctx_out=li < n_layers - 1)
    out = _final_norm(xa, final_norm_g.reshape(1, -1), 0, batch * seq_len)
    return out.reshape(batch, seq_len, d)
```

```python
import functools
import math

import jax
import jax.numpy as jnp
from jax import lax
from jax.experimental import pallas as pl
from jax.experimental.pallas import tpu as pltpu

F32 = jnp.float32
BF16 = jnp.bfloat16

NORM_EPS = 1e-6
N_MOD = 6
GRID_W = 64
ROPE_BASE = 10000.0

CONV_CH = 512
MLA_HEADS = 8
Q_RANK = 256
KV_RANK = 128
QK_NOPE = 64
QK_ROPE = 32
V_HEAD = 64
MLA_SCALE = (QK_NOPE + QK_ROPE) ** -0.5

GDN_HEADS = 8
GDN_DK = 128
GDN_CHUNK = 64
GDN_QK = GDN_HEADS * GDN_DK

N_EXPERTS = 8

LANES = 128
ROW_TILE = 256
GDN_STEP = 256
GDN_HEAD_GROUP = 8
EXPERT_TILE = 512
EXPERT_FCHUNK = 512
VMEM_LIMIT_BYTES = 56 * 1024 * 1024


def _cparams(*sem):
    return pltpu.CompilerParams(dimension_semantics=sem, vmem_limit_bytes=VMEM_LIMIT_BYTES)


def _silu(x):
    return x * (1.0 / (1.0 + jnp.exp(-x)))


def _rms(x, g):
    return x * lax.rsqrt(jnp.mean(x * x, axis=-1, keepdims=True) + NORM_EPS) * g


def _dot(a, b):
    return jnp.dot(a, b, preferred_element_type=F32)


def _dot_nt(a, b):
    return lax.dot_general(a, b, (((1,), (1,)), ((), ())), preferred_element_type=F32)


def _dot_tn(a, b):
    return lax.dot_general(a, b, (((0,), (0,)), ((), ())), preferred_element_type=F32)


def _split3(x):
    x1 = x.astype(BF16)
    r1 = x - x1.astype(F32)
    x2 = r1.astype(BF16)
    x3 = (r1 - x2.astype(F32)).astype(BF16)
    return x1, x2, x3


def _dot_f32(a, b):
    a1, a2, a3 = _split3(a)
    b1, b2, b3 = _split3(b)
    acc = _dot(a1, b3) + _dot(a2, b2) + _dot(a3, b1)
    acc = acc + _dot(a1, b2) + _dot(a2, b1)
    return acc + _dot(a1, b1)


def _dot_f32_exact_rhs(a, m_bf16):
    a1, a2, a3 = _split3(a)
    return _dot(a3, m_bf16) + _dot(a2, m_bf16) + _dot(a1, m_bf16)


def _mod_row_fn(n_ctx_tiles, tiles_per_seq, batch):
    def row(i):
        return jnp.where(i < n_ctx_tiles, batch, (i - n_ctx_tiles) // tiles_per_seq)
    return row


def _mods_kernel(c_ref, w_ref, b_ref, o_ref):
    o_ref[...] = _dot_f32(_silu(c_ref[...]), w_ref[...]) + b_ref[...]


def _mods(cc, mod_w, mod_b):
    rows, d = cc.shape
    n = mod_w.shape[1]
    out = pl.pallas_call(
        _mods_kernel,
        grid=(n // d,),
        in_specs=[pl.BlockSpec((rows, d), lambda j: (0, 0)),
                  pl.BlockSpec((d, d), lambda j: (0, j)),
                  pl.BlockSpec((1, d), lambda j: (0, j))],
        out_specs=pl.BlockSpec((rows, d), lambda j: (0, j)),
        out_shape=jax.ShapeDtypeStruct((rows, n), F32),
        compiler_params=_cparams("arbitrary"),
        name="adaln_mods",
    )(cc, mod_w, mod_b.reshape(1, n))
    return out.reshape(rows, 1, n)


def _mod_spec(d, k, row, off=0):
    return pl.BlockSpec((1, 1, d), lambda i: (row(i + off), 0, k))


def _even_proj_kernel(x_ref, g_ref, sh_ref, sc_ref, win_ref, qg_ref, wq_ref, kvg_ref, wkv_ref, cos_ref, sin_ref,
                      gbu_ref, q_ref, k_ref, v_ref):
    hn = _rms(x_ref[...], g_ref[...]) * (1.0 + sc_ref[0]) + sh_ref[0]
    hb = hn.astype(BF16)
    n_gbu = 3 * CONV_CH
    for c0 in range(0, n_gbu, CONV_CH):
        gbu_ref[:, c0:c0 + CONV_CH] = _dot(hb, win_ref[:, c0:c0 + CONV_CH]).astype(gbu_ref.dtype)
    rest = _dot(hb, win_ref[:, n_gbu:])
    cq = rest[:, :Q_RANK]
    ckv = rest[:, Q_RANK:Q_RANK + KV_RANK]
    kr = rest[:, Q_RANK + KV_RANK:Q_RANK + KV_RANK + LANES]
    kr_sw = rest[:, Q_RANK + KV_RANK + LANES:]
    cos = cos_ref[...]
    sin = sin_ref[...]
    kr_rot = kr * cos + kr_sw * sin
    qq = _dot(_rms(cq, qg_ref[...]).astype(BF16), wq_ref[...])
    kv = _dot(_rms(ckv, kvg_ref[...]).astype(BF16), wkv_ref[...])
    hw = MLA_HEADS * LANES
    for h in range(MLA_HEADS):
        sl = slice(h * LANES, (h + 1) * LANES)
        qh = (qq[:, sl] * cos + qq[:, hw + h * LANES:hw + (h + 1) * LANES] * sin) * MLA_SCALE
        q_ref[:, sl] = qh.astype(q_ref.dtype)
        k_ref[:, sl] = (kv[:, sl] + kr_rot).astype(k_ref.dtype)
    v_ref[...] = kv[:, hw:].astype(v_ref.dtype)


def _even_proj(x2, g, mods, row, n_tiles, w_in, qg, wq, kvg, wkv, cos_t, sin_t, tab_idx):
    t, d = x2.shape
    tm = ROW_TILE
    n_in = w_in.shape[1]
    hw = MLA_HEADS * LANES
    const = lambda i: (0, 0)
    return pl.pallas_call(
        _even_proj_kernel,
        grid=(n_tiles,),
        in_specs=[pl.BlockSpec((tm, d), lambda i: (i, 0)),
                  pl.BlockSpec((1, d), const),
                  _mod_spec(d, 0, row), _mod_spec(d, 1, row),
                  pl.BlockSpec((d, n_in), const),
                  pl.BlockSpec((1, Q_RANK), const),
                  pl.BlockSpec(wq.shape, const),
                  pl.BlockSpec((1, KV_RANK), const),
                  pl.BlockSpec(wkv.shape, const),
                  pl.BlockSpec((tm, LANES), lambda i: (tab_idx(i), 0)),
                  pl.BlockSpec((tm, LANES), lambda i: (tab_idx(i), 0))],
        out_specs=[pl.BlockSpec((tm, 3 * CONV_CH), lambda i: (i, 0)),
                   pl.BlockSpec((tm, hw), lambda i: (i, 0)),
                   pl.BlockSpec((tm, hw), lambda i: (i, 0)),
                   pl.BlockSpec((tm, MLA_HEADS * V_HEAD), lambda i: (i, 0))],
        out_shape=[jax.ShapeDtypeStruct((t, 3 * CONV_CH), BF16),
                   jax.ShapeDtypeStruct((t, hw), BF16),
                   jax.ShapeDtypeStruct((t, hw), BF16),
                   jax.ShapeDtypeStruct((t, MLA_HEADS * V_HEAD), BF16)],
        compiler_params=_cparams("arbitrary"),
        name="even_proj",
    )(x2, g, mods, mods, w_in, qg, wq, kvg, wkv, cos_t, sin_t)


def _shifted(z, prev_row, next_row):
    tm = z.shape[0]
    ridx = lax.broadcasted_iota(jnp.int32, z.shape, 0)
    zm = jnp.where(ridx == 0, prev_row, pltpu.roll(z, 1, 0))
    zp = jnp.where(ridx == tm - 1, next_row, pltpu.roll(z, tm - 1, 0))
    return zm, zp


def _seq_edges(i, n_ctx_tiles, ctx_tps, x_tps):
    j = jnp.where(i < n_ctx_tiles, i % ctx_tps, (i - n_ctx_tiles) % x_tps)
    tps = jnp.where(i < n_ctx_tiles, ctx_tps, x_tps)
    keep_prev = (j != 0).astype(F32)
    keep_next = (j != tps - 1).astype(F32)
    return keep_prev, keep_next


def _even_conv_kernel(cur_ref, prev_ref, next_ref, w_ref, o_ref, *, n_ctx_tiles, ctx_tps, x_tps):
    i = pl.program_id(0)
    keep_prev, keep_next = _seq_edges(i, n_ctx_tiles, ctx_tps, x_tps)
    c = CONV_CH
    cur = cur_ref[...].astype(F32)
    z = cur[:, c:2 * c] * cur[:, 2 * c:]
    pr = prev_ref[...].astype(F32)
    nx = next_ref[...].astype(F32)
    zprev = (pr[HALO_ROWS - 1:, c:2 * c] * pr[HALO_ROWS - 1:, 2 * c:]) * keep_prev
    znext = (nx[0:1, c:2 * c] * nx[0:1, 2 * c:]) * keep_next
    zm, zp = _shifted(z, zprev, znext)
    w = w_ref[...]
    y = zm * w[0:1] + z * w[1:2] + zp * w[2:3]
    o_ref[...] = (cur[:, :c] * y).astype(o_ref.dtype)


HALO_ROWS = 16


def _halo_specs(tm, width, n_rows, colmap):
    nb = n_rows // HALO_ROWS
    r = tm // HALO_ROWS
    prev = pl.BlockSpec((HALO_ROWS, width), lambda i, *a: (jnp.maximum(i * r - 1, 0), colmap(*a)))
    nxt = pl.BlockSpec((HALO_ROWS, width), lambda i, *a: (jnp.minimum((i + 1) * r, nb - 1), colmap(*a)))
    return prev, nxt


def _even_conv(gbu, conv_w, n_ctx_tiles, ctx_tps, x_tps):
    t, n = gbu.shape
    tm = ROW_TILE
    prev, nxt = _halo_specs(tm, n, t, lambda: 0)
    return pl.pallas_call(
        functools.partial(_even_conv_kernel, n_ctx_tiles=n_ctx_tiles, ctx_tps=ctx_tps, x_tps=x_tps),
        grid=(t // tm,),
        in_specs=[pl.BlockSpec((tm, n), lambda i: (i, 0)), prev, nxt,
                  pl.BlockSpec(conv_w.shape, lambda i: (0, 0))],
        out_specs=pl.BlockSpec((tm, CONV_CH), lambda i: (i, 0)),
        out_shape=jax.ShapeDtypeStruct((t, CONV_CH), BF16),
        compiler_params=_cparams("arbitrary"),
        name="even_conv",
    )(gbu, gbu, gbu, conv_w)


def _attn_kernel(q_ref, kc_ref, kx_ref, vc_ref, vx_ref, o_ref, *, n_ctx_q):
    qi = pl.program_id(1)

    def run(use_x):
        for hp in range(MLA_HEADS // 2):
            outs = []
            for h in (2 * hp, 2 * hp + 1):
                ks = slice(h * LANES, (h + 1) * LANES)
                vs = slice(h * V_HEAD, (h + 1) * V_HEAD)
                q = q_ref[:, ks]
                sc = _dot_nt(q, kc_ref[:, ks])
                m = jnp.max(sc, axis=1, keepdims=True)
                if use_x:
                    sx = _dot_nt(q, kx_ref[:, ks])
                    m = jnp.maximum(m, jnp.max(sx, axis=1, keepdims=True))
                pc = jnp.exp(sc - m)
                l = jnp.sum(pc, axis=1, keepdims=True)
                o = _dot(pc.astype(BF16), vc_ref[:, vs])
                if use_x:
                    px = jnp.exp(sx - m)
                    l = l + jnp.sum(px, axis=1, keepdims=True)
                    o = o + _dot(px.astype(BF16), vx_ref[:, vs])
                outs.append(o * (1.0 / l))
            o_ref[:, hp * LANES:(hp + 1) * LANES] = jnp.concatenate(outs, axis=1).astype(o_ref.dtype)

    @pl.when(qi < n_ctx_q)
    def _():
        run(False)

    @pl.when(qi >= n_ctx_q)
    def _():
        run(True)


def _attention(q, k, v, batch, ctx_len, seq_len):
    tq = ROW_TILE
    n_ctx_q = ctx_len // tq
    n_x_q = seq_len // tq
    ctx_blocks = batch * ctx_len // tq
    assert (batch * ctx_len) % seq_len == 0
    x_blk0 = batch * ctx_len // seq_len
    hw = MLA_HEADS * LANES
    vw = MLA_HEADS * V_HEAD

    def qrow(b, qi):
        return jnp.where(qi < n_ctx_q, b * n_ctx_q + qi, ctx_blocks + b * n_x_q + (qi - n_ctx_q))

    return pl.pallas_call(
        functools.partial(_attn_kernel, n_ctx_q=n_ctx_q),
        grid=(batch, n_ctx_q + n_x_q),
        in_specs=[pl.BlockSpec((tq, hw), lambda b, qi: (qrow(b, qi), 0)),
                  pl.BlockSpec((ctx_len, hw), lambda b, qi: (b, 0)),
                  pl.BlockSpec((seq_len, hw), lambda b, qi: (x_blk0 + b, 0)),
                  pl.BlockSpec((ctx_len, vw), lambda b, qi: (b, 0)),
                  pl.BlockSpec((seq_len, vw), lambda b, qi: (x_blk0 + b, 0))],
        out_specs=pl.BlockSpec((tq, vw), lambda b, qi: (qrow(b, qi), 0)),
        out_shape=jax.ShapeDtypeStruct((q.shape[0], vw), BF16),
        compiler_params=_cparams("arbitrary", "arbitrary"),
        name="mla_attention",
    )(q, k, k, v, v)


def _resid_mm_kernel(*refs, n_lhs):
    x_ref, gate_ref = refs[0], refs[1]
    lhs = refs[2:2 + n_lhs]
    ws = refs[2 + n_lhs:2 + 2 * n_lhs]
    o_ref = refs[2 + 2 * n_lhs]
    acc = _dot(lhs[0][...], ws[0][...])
    for a_ref, w_ref in zip(lhs[1:], ws[1:]):
        acc = acc + _dot(a_ref[...], w_ref[...])
    o_ref[...] = x_ref[...] + gate_ref[0] * acc


def _resid_mm(x2, mods, k_gate, row, lhs_list, w_list, tm=512):
    t, d = x2.shape
    n = len(lhs_list)
    in_specs = [pl.BlockSpec((tm, d), lambda i: (i, 0)), _mod_spec(d, k_gate, lambda i: row(i * tm // ROW_TILE))]
    in_specs += [pl.BlockSpec((tm, a.shape[1]), lambda i: (i, 0)) for a in lhs_list]
    in_specs += [pl.BlockSpec(w.shape, lambda i: (0, 0)) for w in w_list]
    return pl.pallas_call(
        functools.partial(_resid_mm_kernel, n_lhs=n),
        grid=(t // tm,),
        in_specs=in_specs,
        out_specs=pl.BlockSpec((tm, d), lambda i: (i, 0)),
        out_shape=jax.ShapeDtypeStruct((t, d), F32),
        compiler_params=_cparams("arbitrary"),
        name="resid_matmul",
    )(x2, mods, *lhs_list, *w_list)


def _ffn_up_kernel(x_ref, g_ref, sh_ref, sc_ref, w_ref, o_ref, *, ffn, chunk):
    hn = _rms(x_ref[...], g_ref[...]) * (1.0 + sc_ref[0]) + sh_ref[0]
    hb = hn.astype(BF16)
    for c0 in range(0, ffn, chunk):
        gate = _dot(hb, w_ref[:, c0:c0 + chunk])
        up = _dot(hb, w_ref[:, ffn + c0:ffn + c0 + chunk])
        o_ref[:, c0:c0 + chunk] = (_silu(gate) * up).astype(o_ref.dtype)


def _ffn_up(x2, g, mods, row, w1):
    t, d = x2.shape
    tm = ROW_TILE
    ffn = w1.shape[1] // 2
    chunk = 256
    assert ffn % chunk == 0
    return pl.pallas_call(
        functools.partial(_ffn_up_kernel, ffn=ffn, chunk=chunk),
        grid=(t // tm,),
        in_specs=[pl.BlockSpec((tm, d), lambda i: (i, 0)),
                  pl.BlockSpec((1, d), lambda i: (0, 0)),
                  _mod_spec(d, 3, row), _mod_spec(d, 4, row),
                  pl.BlockSpec(w1.shape, lambda i: (0, 0))],
        out_specs=pl.BlockSpec((tm, ffn), lambda i: (i, 0)),
        out_shape=jax.ShapeDtypeStruct((t, ffn), BF16),
        compiler_params=_cparams("arbitrary"),
        name="ffn_up",
    )(x2, g, mods, mods, w1)


def _odd_proj_kernel(x_ref, g_ref, sh_ref, sc_ref, w_ref, qkv_ref, gate_ref, abt_ref, *, chunk):
    hn = _rms(x_ref[...], g_ref[...]) * (1.0 + sc_ref[0]) + sh_ref[0]
    hb = hn.astype(BF16)
    n_qkv = qkv_ref.shape[1]
    n_gate = gate_ref.shape[1]
    for c0 in range(0, n_qkv, chunk):
        qkv_ref[:, c0:c0 + chunk] = _dot(hb, w_ref[:, c0:c0 + chunk]).astype(qkv_ref.dtype)
    for c0 in range(0, n_gate, chunk):
        gate_ref[:, c0:c0 + chunk] = _dot(hb, w_ref[:, n_qkv + c0:n_qkv + c0 + chunk]).astype(gate_ref.dtype)
    abt_ref[...] = _dot(hb, w_ref[:, n_qkv + n_gate:])


def _odd_proj(x2, g, mods, row, w_in):
    t, d = x2.shape
    tm = ROW_TILE
    n_qkv = 3 * GDN_QK
    n_gate = GDN_QK
    return pl.pallas_call(
        functools.partial(_odd_proj_kernel, chunk=512),
        grid=(t // tm,),
        in_specs=[pl.BlockSpec((tm, d), lambda i: (i, 0)),
                  pl.BlockSpec((1, d), lambda i: (0, 0)),
                  _mod_spec(d, 0, row), _mod_spec(d, 1, row),
                  pl.BlockSpec(w_in.shape, lambda i: (0, 0))],
        out_specs=[pl.BlockSpec((tm, n_qkv), lambda i: (i, 0)),
                   pl.BlockSpec((tm, n_gate), lambda i: (i, 0)),
                   pl.BlockSpec((tm, LANES), lambda i: (i, 0))],
        out_shape=[jax.ShapeDtypeStruct((t, n_qkv), BF16),
                   jax.ShapeDtypeStruct((t, n_gate), BF16),
                   jax.ShapeDtypeStruct((t, LANES), F32)],
        compiler_params=_cparams("arbitrary"),
        name="odd_proj",
    )(x2, g, mods, mods, w_in)


def _odd_conv_kernel(cur_ref, prev_ref, next_ref, w_ref, o_ref, *, n_ctx_tiles, ctx_tps, x_tps, heads_per_block):
    i = pl.program_id(0)
    j = pl.program_id(1)
    keep_prev, keep_next = _seq_edges(i, n_ctx_tiles, ctx_tps, x_tps)
    z = cur_ref[...].astype(F32)
    zprev = prev_ref[HALO_ROWS - 1:, :].astype(F32) * keep_prev
    znext = next_ref[0:1, :].astype(F32) * keep_next
    zm, zp = _shifted(z, zprev, znext)
    w = w_ref[...]
    y = _silu(zm * w[0:1] + z * w[1:2] + zp * w[2:3])
    blocks_per_part = GDN_HEADS // heads_per_block
    is_qk = j < 2 * blocks_per_part
    is_q = j < blocks_per_part
    for hh in range(heads_per_block):
        yh = y[:, hh * LANES:(hh + 1) * LANES]
        nrm = lax.rsqrt(jnp.sum(yh * yh, axis=-1, keepdims=True) + NORM_EPS)
        nrm = jnp.where(is_q, nrm * (GDN_DK ** -0.5), nrm)
        o_ref[hh] = jnp.where(is_qk, yh * nrm, yh)


def _odd_conv(qkv, conv_w, n_ctx_tiles, ctx_tps, x_tps):
    t, n = qkv.shape
    tm = ROW_TILE
    cb = 512
    hpb = cb // LANES
    prev, nxt = _halo_specs(tm, cb, t, lambda j: j)
    return pl.pallas_call(
        functools.partial(_odd_conv_kernel, n_ctx_tiles=n_ctx_tiles, ctx_tps=ctx_tps, x_tps=x_tps,
                          heads_per_block=hpb),
        grid=(t // tm, n // cb),
        in_specs=[pl.BlockSpec((tm, cb), lambda i, j: (i, j)), prev, nxt,
                  pl.BlockSpec((conv_w.shape[0], cb), lambda i, j: (0, j))],
        out_specs=pl.BlockSpec((hpb, tm, LANES), lambda i, j: (j, i, 0)),
        out_shape=jax.ShapeDtypeStruct((n // LANES, t, LANES), F32),
        compiler_params=_cparams("arbitrary", "arbitrary"),
        name="odd_conv",
    )(qkv, qkv, qkv, conv_w)


def _gdn_kernel(qkv_ref, abt_ref, alog_ref, dtb_ref, o_ref,
                s_ref, gc_s, beta_s, gtot_s, uw_s, qg_s, kd_s, attn_s, vnew_s, gtc_s):
    d = pl.program_id(1)
    step = pl.program_id(2)
    n = GDN_STEP
    c = GDN_CHUNK
    nchunk = n // c
    h_n = GDN_HEADS
    sgn = 1 - 2 * d

    @pl.when(step == 0)
    def _():
        s_ref[...] = jnp.zeros(s_ref.shape, s_ref.dtype)

    ri = lax.broadcasted_iota(jnp.int32, (n, n), 0)
    ci = lax.broadcasted_iota(jnp.int32, (n, n), 1)
    same = (ri // c) == (ci // c)
    rel = (ri - ci) * sgn
    incl = same & (rel >= 0)
    strict = same & (rel > 0)
    eye = (ri == ci).astype(F32)

    hrow = pl.ds(d * h_n, h_n)
    a_all = abt_ref[0, hrow, :]
    bt_all = abt_ref[0, pl.ds(2 * h_n + d * h_n, h_n), :]
    sp_in = a_all + dtb_ref[hrow, :]
    softplus = jnp.maximum(sp_in, 0.0) + jnp.log(1.0 + jnp.exp(-jnp.abs(sp_in)))
    g_all = -jnp.exp(alog_ref[hrow, :]) * softplus
    beta_s[...] = 1.0 / (1.0 + jnp.exp(-bt_all))
    cum_m = jnp.where(same & ((ci - ri) * sgn >= 0), 1.0, 0.0).astype(BF16)
    gc_s[...] = _dot_f32_exact_rhs(g_all, cum_m)
    gtot_s[...] = _dot_f32_exact_rhs(g_all, same.astype(BF16))

    grp = uw_s.shape[0]
    n_sq = (c - 1).bit_length() - 1

    def head_group(gi, carry):
        heads = [gi * grp + u for u in range(grp)]
        us = range(grp)
        q = [qkv_ref[h] for h in heads]
        k = [qkv_ref[h_n + h] for h in heads]
        v = [qkv_ref[2 * h_n + h] for h in heads]
        gc_row = [gc_s[pl.ds(h, 1), :] for h in heads]
        gc_col = [jnp.sum(eye * gc_row[u], axis=1, keepdims=True) for u in us]
        beta_col = [jnp.sum(eye * beta_s[pl.ds(h, 1), :], axis=1, keepdims=True) for h in heads]
        gtot_col = [jnp.sum(eye * gtot_s[pl.ds(h, 1), :], axis=1, keepdims=True) for h in heads]
        decay = [jnp.exp(jnp.where(incl, gc_col[u] - gc_row[u], -1e30)) for u in us]
        kb = [k[u] * beta_col[u] for u in us]
        k16 = [k[u].astype(BF16) for u in us]
        p = [jnp.where(strict, -(_dot_nt(kb[u].astype(BF16), k16[u]) * decay[u]), 0.0) for u in us]
        for u in us:
            attn_s[u] = (_dot_nt(q[u].astype(BF16), k16[u]) * decay[u]).astype(attn_s.dtype)
            gtc_s[u] = jnp.broadcast_to(gtot_col[u], gtc_s.shape[1:])
        tinv = [eye + p[u] for u in us]
        p16 = [p[u].astype(BF16) for u in us]
        p = [_dot(p16[u], p16[u]) for u in us]
        for _ in range(n_sq - 1):
            p16 = [p[u].astype(BF16) for u in us]
            tp = [_dot(jnp.concatenate([tinv[u].astype(BF16), p16[u]], axis=0), p16[u]) for u in us]
            tinv = [tinv[u] + tp[u][:n] for u in us]
            p = [tp[u][n:] for u in us]
        tinv = [tinv[u] + _dot(tinv[u].astype(BF16), p[u].astype(BF16)) for u in us]
        egc = [jnp.exp(gc_col[u]) for u in us]
        for u in us:
            rhs = jnp.concatenate([v[u] * beta_col[u], kb[u] * egc[u]], axis=1).astype(BF16)
            uw_s[u] = _dot(tinv[u].astype(BF16), rhs)
            qg_s[u] = q[u] * egc[u]
            kd_s[u] = k[u] * jnp.exp(gtot_col[u] - gc_col[u])
            vnew_s[u] = jnp.zeros(vnew_s.shape[1:], vnew_s.dtype)
        s = [s_ref[h] for h in heads]
        for cstep in range(nchunk):
            cc = jnp.where(d == 0, cstep, nchunk - 1 - cstep)
            r0 = pl.multiple_of(cc * c, c)
            rows = pl.ds(r0, c)
            ws_qs = [_dot(jnp.concatenate([uw_s[u, rows, LANES:], qg_s[u, rows, :]], axis=0).astype(BF16),
                          s[u].astype(BF16)) for u in us]
            v_new = [uw_s[u, rows, :LANES] - ws_qs[u][:c] for u in us]
            for u in us:
                vnew_s[u, rows, :] = v_new[u].astype(vnew_s.dtype)
            for u in us:
                o_ref[0, heads[u], rows, :] = ws_qs[u][c:] + _dot(attn_s[u, rows, :], vnew_s[u])
            s = [s[u] * jnp.exp(gtc_s[u, pl.ds(r0, 1), :])
                 + _dot_tn(kd_s[u, rows, :].astype(BF16), v_new[u].astype(BF16)) for u in us]
        for u in us:
            s_ref[heads[u]] = s[u]
        return carry

    lax.fori_loop(0, h_n // grp, head_group, 0)


def _gdn(qkvn, abt_chunks, alog_b, dtb_b, batch, ctx_len, seq_len):
    nh3, t, _ = qkvn.shape
    n = GDN_STEP
    nc = ctx_len // n
    nx = seq_len // n
    ctx_blocks = batch * nc

    def rowblk(b, d, s):
        sc = jnp.where(d == 0, s, nc - 1 - s)
        sx = jnp.where(d == 0, s - nc, nx - 1 - (s - nc))
        return jnp.where(s < nc, b * nc + sc, ctx_blocks + b * nx + sx)

    vm = pltpu.VMEM
    g = GDN_HEAD_GROUP
    return pl.pallas_call(
        _gdn_kernel,
        grid=(batch, 2, nc + nx),
        in_specs=[pl.BlockSpec((nh3, n, LANES), lambda b, d, s: (0, rowblk(b, d, s), 0)),
                  pl.BlockSpec((1, 4 * GDN_HEADS, n), lambda b, d, s: (rowblk(b, d, s), 0, 0)),
                  pl.BlockSpec(alog_b.shape, lambda b, d, s: (0, 0)),
                  pl.BlockSpec(dtb_b.shape, lambda b, d, s: (0, 0))],
        out_specs=pl.BlockSpec((1, GDN_HEADS, n, LANES), lambda b, d, s: (d, 0, rowblk(b, d, s), 0)),
        out_shape=jax.ShapeDtypeStruct((2, GDN_HEADS, t, LANES), F32),
        scratch_shapes=[vm((GDN_HEADS, GDN_DK, LANES), F32),
                        vm((GDN_HEADS, n), F32),
                        vm((GDN_HEADS, n), F32),
                        vm((GDN_HEADS, n), F32),
                        vm((g, n, 2 * LANES), F32),
                        vm((g, n, LANES), F32),
                        vm((g, n, LANES), F32),
                        vm((g, n, n), BF16),
                        vm((g, n, LANES), BF16),
                        vm((g, n, LANES), F32)],
        compiler_params=_cparams("arbitrary", "arbitrary", "arbitrary"),
        name="gated_delta",
    )(qkvn, abt_chunks, alog_b, dtb_b)


def _odd_out_kernel(x_ref, gate_ref, o_ref_in, z_ref, og_ref, w_ref, out_ref):
    parts = []
    og = og_ref[...]
    for h in range(GDN_HEADS):
        o = o_ref_in[0, h] + o_ref_in[1, h]
        y = _rms(o, og) * _silu(z_ref[:, h * LANES:(h + 1) * LANES].astype(F32))
        parts.append(y.astype(BF16))
    y_all = jnp.concatenate(parts, axis=1)
    out_ref[...] = x_ref[...] + gate_ref[0] * _dot(y_all, w_ref[...])


def _odd_out(x2, mods, row, o_dirs, z, o_norm_g, w_out, row_off, n_rows):
    d = x2.shape[1]
    tm = ROW_TILE
    return pl.pallas_call(
        _odd_out_kernel,
        grid=(n_rows // tm,),
        in_specs=[pl.BlockSpec((tm, d), lambda i: (i + row_off, 0)),
                  _mod_spec(d, 2, row, row_off),
                  pl.BlockSpec((2, GDN_HEADS, tm, LANES), lambda i: (0, 0, i + row_off, 0)),
                  pl.BlockSpec((tm, z.shape[1]), lambda i: (i + row_off, 0)),
                  pl.BlockSpec((1, LANES), lambda i: (0, 0)),
                  pl.BlockSpec(w_out.shape, lambda i: (0, 0))],
        out_specs=pl.BlockSpec((tm, d), lambda i: (i, 0)),
        out_shape=jax.ShapeDtypeStruct((n_rows, d), F32),
        compiler_params=_cparams("arbitrary"),
        name="odd_out",
    )(x2, mods, o_dirs, z, o_norm_g, w_out)


def _router_kernel(x_ref, g_ref, sh_ref, sc_ref, rw_ref, h_ref, info_ref, cnt_ref, carry_s):
    i = pl.program_id(0)

    @pl.when(i == 0)
    def _():
        carry_s[...] = jnp.zeros(carry_s.shape, carry_s.dtype)

    hn = _rms(x_ref[...], g_ref[...]) * (1.0 + sc_ref[0]) + sh_ref[0]
    h_ref[...] = hn
    tm = hn.shape[0]
    lane = lax.broadcasted_iota(jnp.int32, (tm, LANES), 1)
    logits = jnp.where(lane < N_EXPERTS, _dot_f32(hn, rw_ref[...]), -jnp.inf)
    m1 = jnp.max(logits, axis=1, keepdims=True)
    i1 = jnp.min(jnp.where(logits == m1, lane, LANES), axis=1, keepdims=True)
    rest = jnp.where(lane == i1, -jnp.inf, logits)
    m2 = jnp.max(rest, axis=1, keepdims=True)
    i2 = jnp.min(jnp.where(rest == m2, lane, LANES), axis=1, keepdims=True)
    e2 = jnp.exp(m2 - m1)
    w1 = 1.0 / (1.0 + e2)
    w2 = e2 / (1.0 + e2)
    oh1 = (lane == i1).astype(F32)
    oh2 = (lane == i2).astype(F32)
    cnt = oh1 + oh2
    ri = lax.broadcasted_iota(jnp.int32, (tm, tm), 0)
    ci = lax.broadcasted_iota(jnp.int32, (tm, tm), 1)
    before = (ci < ri).astype(BF16)
    pre = _dot(before, cnt.astype(BF16)) + carry_s[0:1, :]
    r1 = jnp.sum(pre * oh1, axis=1, keepdims=True)
    r2 = jnp.sum(pre * oh2, axis=1, keepdims=True)
    info = jnp.where(lane == 0, i1.astype(F32), 0.0)
    info = jnp.where(lane == 1, i2.astype(F32), info)
    info = jnp.where(lane == 2, w1, info)
    info = jnp.where(lane == 3, w2, info)
    info = jnp.where(lane == 4, r1, info)
    info = jnp.where(lane == 5, r2, info)
    info_ref[...] = info
    total = carry_s[0:1, :] + jnp.sum(cnt, axis=0, keepdims=True)
    carry_s[...] = jnp.broadcast_to(total, carry_s.shape)
    cnt_ref[...] = jnp.broadcast_to(total, cnt_ref.shape)


def _router(x2, g, mods, row, router_w_pad):
    t, d = x2.shape
    tm = ROW_TILE
    return pl.pallas_call(
        _router_kernel,
        grid=(t // tm,),
        in_specs=[pl.BlockSpec((tm, d), lambda i: (i, 0)),
                  pl.BlockSpec((1, d), lambda i: (0, 0)),
                  _mod_spec(d, 3, row), _mod_spec(d, 4, row),
                  pl.BlockSpec(router_w_pad.shape, lambda i: (0, 0))],
        out_specs=[pl.BlockSpec((tm, d), lambda i: (i, 0)),
                   pl.BlockSpec((tm, LANES), lambda i: (i, 0)),
                   pl.BlockSpec((8, LANES), lambda i: (0, 0))],
        out_shape=[jax.ShapeDtypeStruct((t, d), F32),
                   jax.ShapeDtypeStruct((t, LANES), F32),
                   jax.ShapeDtypeStruct((8, LANES), F32)],
        scratch_shapes=[pltpu.VMEM((8, LANES), F32)],
        compiler_params=_cparams("arbitrary"),
        name="moe_router",
    )(x2, g, mods, mods, router_w_pad)


def _row_copy(src_hbm, src_row, dst_ref, dst_row, sem):
    return pltpu.make_async_copy(src_hbm.at[pl.ds(src_row, 1)], dst_ref.at[pl.ds(dst_row, 1)], sem)


def _dispatch_kernel(dest_ref, h_ref, init_hbm, hs_hbm, sem, *, tm):
    del init_hbm

    def issue(r, carry):
        _row_copy(h_ref, r, hs_hbm, dest_ref[0, 0, r], sem).start()
        _row_copy(h_ref, r, hs_hbm, dest_ref[0, 0, tm + r], sem).start()
        return carry

    lax.fori_loop(0, tm, issue, 0)

    def drain(r, carry):
        _row_copy(h_ref, 0, hs_hbm, 0, sem).wait()
        _row_copy(h_ref, 0, hs_hbm, 0, sem).wait()
        return carry

    lax.fori_loop(0, tm, drain, 0)


def _dispatch(h, dest_tiles, n_sorted_rows):
    t, d = h.shape
    tm = ROW_TILE
    init = jnp.zeros((n_sorted_rows, d), h.dtype)
    return pl.pallas_call(
        functools.partial(_dispatch_kernel, tm=tm),
        grid=(t // tm,),
        in_specs=[pl.BlockSpec((1, 1, 2 * tm), lambda i: (i, 0, 0), memory_space=pltpu.SMEM),
                  pl.BlockSpec((tm, d), lambda i: (i, 0)),
                  pl.BlockSpec(memory_space=pl.ANY)],
        out_specs=pl.BlockSpec(memory_space=pl.ANY),
        out_shape=jax.ShapeDtypeStruct((n_sorted_rows, d), h.dtype),
        scratch_shapes=[pltpu.SemaphoreType.DMA(())],
        input_output_aliases={2: 0},
        compiler_params=_cparams("arbitrary"),
        name="moe_dispatch",
    )(dest_tiles, h, init)


def _expert_kernel(te_ref, nt_ref, hs_ref, wg_ref, wu_ref, w2_ref, ys_ref, acc_s):
    j = pl.program_id(0)
    f = pl.program_id(1)
    nf = pl.num_programs(1)

    @pl.when(j < nt_ref[0])
    def _():
        hb = hs_ref[...].astype(BF16)
        gate = _dot(hb, wg_ref[...])
        up = _dot(hb, wu_ref[...])
        part = _dot((_silu(gate) * up).astype(BF16), w2_ref[...])

        @pl.when(f == 0)
        def _():
            acc_s[...] = part

        @pl.when(f > 0)
        def _():
            acc_s[...] = acc_s[...] + part

        @pl.when(f == nf - 1)
        def _():
            ys_ref[...] = acc_s[...]

    @pl.when((j >= nt_ref[0]) & (f == nf - 1))
    def _():
        ys_ref[...] = jnp.zeros(ys_ref.shape, ys_ref.dtype)


def _experts(hs, tile_expert, n_tiles_used, w1, w2):
    r, d = hs.shape
    tm = EXPERT_TILE
    tf = EXPERT_FCHUNK
    ffn = w2.shape[1]
    nf = ffn // tf
    assert ffn % tf == 0

    def tile(j, nt):
        return jnp.minimum(j, nt[0] - 1)

    def fchunk(j, f, nt):
        return jnp.where(j < nt[0], f, nf - 1)

    grid_spec = pltpu.PrefetchScalarGridSpec(
        num_scalar_prefetch=2,
        grid=(r // tm, nf),
        in_specs=[pl.BlockSpec((tm, d), lambda j, f, te, nt: (tile(j, nt), 0)),
                  pl.BlockSpec((None, d, tf), lambda j, f, te, nt: (te[tile(j, nt)], 0, fchunk(j, f, nt))),
                  pl.BlockSpec((None, d, tf), lambda j, f, te, nt: (te[tile(j, nt)], 0, nf + fchunk(j, f, nt))),
                  pl.BlockSpec((None, tf, d), lambda j, f, te, nt: (te[tile(j, nt)], fchunk(j, f, nt), 0))],
        out_specs=pl.BlockSpec((tm, d), lambda j, f, te, nt: (j, 0)),
        scratch_shapes=[pltpu.VMEM((tm, d), F32)],
    )
    return pl.pallas_call(
        _expert_kernel,
        grid_spec=grid_spec,
        out_shape=jax.ShapeDtypeStruct((r, d), F32),
        compiler_params=_cparams("arbitrary", "arbitrary"),
        name="moe_experts",
    )(tile_expert, n_tiles_used, hs, w1, w1, w2)


def _combine_kernel(dest_ref, x_ref, gate_ref, info_ref, ys_hbm, o_ref, y1_s, y2_s, sem, *, tm):
    def issue(r, carry):
        _row_copy(ys_hbm, dest_ref[0, 0, r], y1_s, r, sem).start()
        _row_copy(ys_hbm, dest_ref[0, 0, tm + r], y2_s, r, sem).start()
        return carry

    lax.fori_loop(0, tm, issue, 0)

    def drain(r, carry):
        _row_copy(ys_hbm, 0, y1_s, 0, sem).wait()
        _row_copy(ys_hbm, 0, y2_s, 0, sem).wait()
        return carry

    lax.fori_loop(0, tm, drain, 0)
    info = info_ref[...]
    y = info[:, 2:3] * y1_s[...] + info[:, 3:4] * y2_s[...]
    o_ref[...] = x_ref[...] + gate_ref[0] * y


def _combine(x2, mods, row, info, dest_tiles, ys):
    t, d = x2.shape
    tm = ROW_TILE
    return pl.pallas_call(
        functools.partial(_combine_kernel, tm=tm),
        grid=(t // tm,),
        in_specs=[pl.BlockSpec((1, 1, 2 * tm), lambda i: (i, 0, 0), memory_space=pltpu.SMEM),
                  pl.BlockSpec((tm, d), lambda i: (i, 0)),
                  _mod_spec(d, 5, row),
                  pl.BlockSpec((tm, LANES), lambda i: (i, 0)),
                  pl.BlockSpec(memory_space=pl.ANY)],
        out_specs=pl.BlockSpec((tm, d), lambda i: (i, 0)),
        out_shape=jax.ShapeDtypeStruct((t, d), F32),
        scratch_shapes=[pltpu.VMEM((tm, d), F32), pltpu.VMEM((tm, d), F32), pltpu.SemaphoreType.DMA(())],
        compiler_params=_cparams("arbitrary"),
        name="moe_combine",
    )(dest_tiles, x2, mods, info, ys)


def _moe(x2, g, mods, row, router_w, w1, w2):
    t, d = x2.shape
    tm = ROW_TILE
    te = EXPERT_TILE
    rw = jnp.pad(router_w, ((0, 0), (0, LANES - N_EXPERTS)))
    h, info, counts = _router(x2, g, mods, row, rw)
    cnt = counts[0, :N_EXPERTS].astype(jnp.int32)
    gsz = ((cnt + te - 1) // te) * te
    ends = jnp.cumsum(gsz)
    offs = ends - gsz
    e1 = info[:, 0].astype(jnp.int32)
    e2 = info[:, 1].astype(jnp.int32)
    dest1 = offs[e1] + info[:, 4].astype(jnp.int32)
    dest2 = offs[e2] + info[:, 5].astype(jnp.int32)
    dest_tiles = jnp.concatenate([dest1.reshape(t // tm, 1, tm), dest2.reshape(t // tm, 1, tm)], axis=2)
    n_sorted = 2 * t + N_EXPERTS * te
    n_tiles_max = n_sorted // te
    starts = jnp.arange(n_tiles_max, dtype=jnp.int32) * te
    tile_expert = jnp.minimum(jnp.sum((starts[:, None] >= ends[None, :]).astype(jnp.int32), axis=1), N_EXPERTS - 1)
    n_used = (ends[-1] // te).astype(jnp.int32).reshape(1)
    hs = _dispatch(h, dest_tiles, n_sorted)
    ys = _experts(hs, tile_expert, n_used, w1, w2)
    return _combine(x2, mods, row, info, dest_tiles, ys)


def _final_norm_kernel(x_ref, g_ref, o_ref):
    o_ref[...] = _rms(x_ref[...], g_ref[...])


def _final_norm(x2, g, row_off, n_rows):
    d = x2.shape[1]
    tm = 512
    off = row_off * ROW_TILE // tm
    return pl.pallas_call(
        _final_norm_kernel,
        grid=(n_rows // tm,),
        in_specs=[pl.BlockSpec((tm, d), lambda i: (i + off, 0)), pl.BlockSpec((1, d), lambda i: (0, 0))],
        out_specs=pl.BlockSpec((tm, d), lambda i: (i, 0)),
        out_shape=jax.ShapeDtypeStruct((n_rows, d), F32),
        compiler_params=_cparams("arbitrary"),
        name="final_norm",
    )(x2, g)


_ROPE_SWAP = tuple(list(range(8, 16)) + list(range(0, 8)) + list(range(24, 32)) + list(range(16, 24)))


def _even_weights(w_in, w_uq, w_ukv):
    d = w_in.shape[0]
    n0 = 3 * CONV_CH + Q_RANK + KV_RANK
    kr = w_in[:, n0:n0 + QK_ROPE]
    kr_sw = kr[:, jnp.array(_ROPE_SWAP)]
    z = lambda n: jnp.zeros((d, n), w_in.dtype)
    w_in_x = jnp.concatenate([w_in[:, :n0], z(QK_NOPE), kr, z(LANES - QK_NOPE - QK_ROPE),
                              z(QK_NOPE), kr_sw, z(LANES - QK_NOPE - QK_ROPE)], axis=1).astype(BF16)
    uq = w_uq.reshape(Q_RANK, MLA_HEADS, QK_NOPE + QK_ROPE)
    pad = LANES - QK_NOPE - QK_ROPE
    uq_plain = jnp.pad(uq, ((0, 0), (0, 0), (0, pad))).reshape(Q_RANK, MLA_HEADS * LANES)
    uq_rope_sw = uq[:, :, QK_NOPE:][:, :, jnp.array(_ROPE_SWAP)]
    uq_sw = jnp.pad(uq_rope_sw, ((0, 0), (0, 0), (QK_NOPE, pad))).reshape(Q_RANK, MLA_HEADS * LANES)
    wq = jnp.concatenate([uq_plain, uq_sw], axis=1).astype(BF16)
    ukv = w_ukv.reshape(KV_RANK, MLA_HEADS, QK_NOPE + V_HEAD)
    k_part = jnp.pad(ukv[:, :, :QK_NOPE], ((0, 0), (0, 0), (0, LANES - QK_NOPE))).reshape(KV_RANK, MLA_HEADS * LANES)
    v_part = ukv[:, :, QK_NOPE:].reshape(KV_RANK, MLA_HEADS * V_HEAD)
    wkv = jnp.concatenate([k_part, v_part], axis=1).astype(BF16)
    return w_in_x, wq, wkv


def _rope_tables(seq_len, ctx_rows):
    n = QK_ROPE // 4
    t = jnp.arange(seq_len)
    inv = ROPE_BASE ** (-jnp.arange(n, dtype=F32) / n)
    ang_r = (t // GRID_W).astype(F32)[:, None] * inv
    ang_c = (t % GRID_W).astype(F32)[:, None] * inv
    cr, sr, cc, sc = jnp.cos(ang_r), jnp.sin(ang_r), jnp.cos(ang_c), jnp.sin(ang_c)
    cos32 = jnp.concatenate([cr, cr, cc, cc], axis=1)
    sin32 = jnp.concatenate([-sr, sr, -sc, sc], axis=1)
    pad = LANES - QK_NOPE - QK_ROPE
    cos_x = jnp.concatenate([jnp.ones((seq_len, QK_NOPE), F32), cos32, jnp.zeros((seq_len, pad), F32)], axis=1)
    sin_x = jnp.concatenate([jnp.zeros((seq_len, QK_NOPE), F32), sin32, jnp.zeros((seq_len, pad), F32)], axis=1)
    cos_c = jnp.concatenate([jnp.ones((ctx_rows, QK_NOPE + QK_ROPE), F32), jnp.zeros((ctx_rows, pad), F32)], axis=1)
    sin_c = jnp.zeros((ctx_rows, LANES), F32)
    return jnp.concatenate([cos_c, cos_x], axis=0), jnp.concatenate([sin_c, sin_x], axis=0)


def _even_layer(xa, mods, dims, tabs, p):
    batch, ctx_len, seq_len = dims
    tm = ROW_TILE
    t = xa.shape[0]
    n_ctx_tiles = batch * ctx_len // tm
    x_tps = seq_len // tm
    row = _mod_row_fn(n_ctx_tiles, x_tps, batch)
    w_in_x, wq, wkv = _even_weights(p["w_in"], p["w_uq"], p["w_ukv"])
    cos_t, sin_t = tabs
    tab_idx = lambda i: jnp.where(i < n_ctx_tiles, 0, 1 + (i - n_ctx_tiles) % x_tps)
    gbu, q, k, v = _even_proj(xa, p["norm1_g"].reshape(1, -1), mods, row, t // tm, w_in_x,
                              p["q_norm_g"].reshape(1, -1), wq, p["kv_norm_g"].reshape(1, -1), wkv,
                              cos_t, sin_t, tab_idx)
    conv_y = _even_conv(gbu, p["conv_w"], n_ctx_tiles, ctx_len // tm, x_tps)
    att = _attention(q, k, v, batch, ctx_len, seq_len)
    w_out = p["w_out"].astype(BF16)
    xa = _resid_mm(xa, mods, 2, row, [conv_y, att], [w_out[:CONV_CH], w_out[CONV_CH:]])
    hmid = _ffn_up(xa, p["norm2_g"].reshape(1, -1), mods, row, p["ffn_w1"].astype(BF16))
    return _resid_mm(xa, mods, 5, row, [hmid], [p["ffn_w2"].astype(BF16)])


def _odd_layer(xa, mods, dims, p, ctx_out):
    batch, ctx_len, seq_len = dims
    tm = ROW_TILE
    t = xa.shape[0]
    n_ctx_tiles = batch * ctx_len // tm
    x_tps = seq_len // tm
    row = _mod_row_fn(n_ctx_tiles, x_tps, batch)
    n_w = p["w_in"].shape[1]
    w_in = jnp.pad(p["w_in"], ((0, 0), (0, 4 * GDN_QK + LANES - n_w))).astype(BF16)
    qkv, z, abt = _odd_proj(xa, p["norm1_g"].reshape(1, -1), mods, row, w_in)
    qkvn = _odd_conv(qkv, p["qkv_conv_w"], n_ctx_tiles, ctx_len // tm, x_tps)
    abt_chunks = abt[:, :4 * GDN_HEADS].reshape(t // GDN_STEP, GDN_STEP, 4 * GDN_HEADS).transpose(0, 2, 1)
    alog_b = jnp.broadcast_to(p["a_log"].reshape(2 * GDN_HEADS, 1), (2 * GDN_HEADS, GDN_STEP))
    dtb_b = jnp.broadcast_to(p["dt_bias"].reshape(2 * GDN_HEADS, 1), (2 * GDN_HEADS, GDN_STEP))
    o_dirs = _gdn(qkvn, abt_chunks, alog_b, dtb_b, batch, ctx_len, seq_len)
    row_off = 0 if ctx_out else n_ctx_tiles
    n_rows = t - row_off * tm
    xo = _odd_out(xa, mods, row, o_dirs, z, p["o_norm_g"].reshape(1, -1), p["w_out"].astype(BF16), row_off, n_rows)
    row_o = row if ctx_out else _mod_row_fn(0, x_tps, batch)
    return _moe(xo, p["norm2_g"].reshape(1, -1), mods, row_o, p["router_w"],
                p["moe_w1"].astype(BF16), p["moe_w2"].astype(BF16))


_EVEN_NAMES = ("mod_w", "mod_b", "norm1_g", "w_in", "conv_w", "q_norm_g", "w_uq", "kv_norm_g", "w_ukv", "w_out",
               "norm2_g", "ffn_w1", "ffn_w2")
_ODD_NAMES = ("mod_w", "mod_b", "norm1_g", "w_in", "qkv_conv_w", "a_log", "dt_bias", "o_norm_g", "w_out",
              "norm2_g", "router_w", "moe_w1", "moe_w2")


def kernel(x, c, ctx, c_ctx, l0_mod_w, l0_mod_b, l0_norm1_g, l0_w_in, l0_conv_w, l0_q_norm_g, l0_w_uq, l0_kv_norm_g, l0_w_ukv, l0_w_out, l0_norm2_g, l0_ffn_w1, l0_ffn_w2, l1_mod_w, l1_mod_b, l1_norm1_g, l1_w_in, l1_qkv_conv_w, l1_a_log, l1_dt_bias, l1_o_norm_g, l1_w_out, l1_norm2_g, l1_router_w, l1_moe_w1, l1_moe_w2, l2_mod_w, l2_mod_b, l2_norm1_g, l2_w_in, l2_conv_w, l2_q_norm_g, l2_w_uq, l2_kv_norm_g, l2_w_ukv, l2_w_out, l2_norm2_g, l2_ffn_w1, l2_ffn_w2, l3_mod_w, l3_mod_b, l3_norm1_g, l3_w_in, l3_qkv_conv_w, l3_a_log, l3_dt_bias, l3_o_norm_g, l3_w_out, l3_norm2_g, l3_router_w, l3_moe_w1, l3_moe_w2, final_norm_g):
    batch, seq_len, d = x.shape
    ctx_len = ctx.shape[1]
    assert seq_len % ROW_TILE == 0 and ctx_len % ROW_TILE == 0 and seq_len % GRID_W == 0
    dims = (batch, ctx_len, seq_len)
    layers = (
        dict(zip(_EVEN_NAMES, (l0_mod_w, l0_mod_b, l0_norm1_g, l0_w_in, l0_conv_w, l0_q_norm_g, l0_w_uq,
                               l0_kv_norm_g, l0_w_ukv, l0_w_out, l0_norm2_g, l0_ffn_w1, l0_ffn_w2))),
        dict(zip(_ODD_NAMES, (l1_mod_w, l1_mod_b, l1_norm1_g, l1_w_in, l1_qkv_conv_w, l1_a_log, l1_dt_bias,
                              l1_o_norm_g, l1_w_out, l1_norm2_g, l1_router_w, l1_moe_w1, l1_moe_w2))),
        dict(zip(_EVEN_NAMES, (l2_mod_w, l2_mod_b, l2_norm1_g, l2_w_in, l2_conv_w, l2_q_norm_g, l2_w_uq,
                               l2_kv_norm_g, l2_w_ukv, l2_w_out, l2_norm2_g, l2_ffn_w1, l2_ffn_w2))),
        dict(zip(_ODD_NAMES, (l3_mod_w, l3_mod_b, l3_norm1_g, l3_w_in, l3_qkv_conv_w, l3_a_log, l3_dt_bias,
                              l3_o_norm_g, l3_w_out, l3_norm2_g, l3_router_w, l3_moe_w1, l3_moe_w2))),
    )
    xa = jnp.concatenate([ctx.reshape(batch * ctx_len, d), x.reshape(batch * seq_len, d)], axis=0)
    mod_rows = 8 * ((batch + 1 + 7) // 8)
    cc = jnp.zeros((mod_rows, d), F32).at[:batch].set(c).at[batch].set(c_ctx)
    tabs = _rope_tables(seq_len, ROW_TILE)
    n_layers = len(layers)
    for li, p in enumerate(layers):
        mods = _mods(cc, p["mod_w"], p["mod_b"])
        if li % 2 == 0:
            xa = _even_layer(xa, mods, dims, tabs, p)
        else:
            xa = _odd_layer(xa, mods, dims, p, ctx_out=li < n_layers - 1)
    out = _final_norm(xa, final_norm_g.reshape(1, -1), 0, batch * seq_len)
    return out.reshape(batch, seq_len, d)
```

```python
import functools
import math

import jax
import jax.numpy as jnp
from jax import lax
from jax.experimental import pallas as pl
from jax.experimental.pallas import tpu as pltpu

F32 = jnp.float32
BF16 = jnp.bfloat16

NORM_EPS = 1e-6
N_MOD = 6
GRID_W = 64
ROPE_BASE = 10000.0

CONV_CH = 512
MLA_HEADS = 8
Q_RANK = 256
KV_RANK = 128
QK_NOPE = 64
QK_ROPE = 32
V_HEAD = 64
MLA_SCALE = (QK_NOPE + QK_ROPE) ** -0.5

GDN_HEADS = 8
GDN_DK = 128
GDN_CHUNK = 64
GDN_QK = GDN_HEADS * GDN_DK

N_EXPERTS = 8

LANES = 128
ROW_TILE = 256
GDN_STEP = 256
GDN_HEAD_GROUP = 8
EXPERT_TILE = 512
EXPERT_FCHUNK = 512
VMEM_LIMIT_BYTES = 56 * 1024 * 1024
EXPERT_VMEM_LIMIT_BYTES = 62 * 1024 * 1024


def _cparams(*sem):
    return pltpu.CompilerParams(dimension_semantics=sem, vmem_limit_bytes=VMEM_LIMIT_BYTES)


def _silu(x):
    return x * (1.0 / (1.0 + jnp.exp(-x)))


def _rms(x, g):
    return x * lax.rsqrt(jnp.mean(x * x, axis=-1, keepdims=True) + NORM_EPS) * g


def _dot(a, b):
    return jnp.dot(a, b, preferred_element_type=F32)


def _dot_nt(a, b):
    return lax.dot_general(a, b, (((1,), (1,)), ((), ())), preferred_element_type=F32)


def _dot_tn(a, b):
    return lax.dot_general(a, b, (((0,), (0,)), ((), ())), preferred_element_type=F32)


def _split3(x):
    x1 = x.astype(BF16)
    r1 = x - x1.astype(F32)
    x2 = r1.astype(BF16)
    x3 = (r1 - x2.astype(F32)).astype(BF16)
    return x1, x2, x3


def _dot_f32(a, b):
    a1, a2, a3 = _split3(a)
    b1, b2, b3 = _split3(b)
    acc = _dot(a1, b3) + _dot(a2, b2) + _dot(a3, b1)
    acc = acc + _dot(a1, b2) + _dot(a2, b1)
    return acc + _dot(a1, b1)


def _dot_f32_exact_rhs(a, m_bf16):
    a1, a2, a3 = _split3(a)
    return _dot(a3, m_bf16) + _dot(a2, m_bf16) + _dot(a1, m_bf16)


def _mod_row_fn(n_ctx_tiles, tiles_per_seq, batch):
    def row(i):
        return jnp.where(i < n_ctx_tiles, batch, (i - n_ctx_tiles) // tiles_per_seq)
    return row


def _mods_kernel(c_ref, w_ref, b_ref, o_ref):
    o_ref[...] = _dot_f32(_silu(c_ref[...]), w_ref[...]) + b_ref[...]


def _mods(cc, mod_w, mod_b):
    rows, d = cc.shape
    n = mod_w.shape[1]
    out = pl.pallas_call(
        _mods_kernel,
        grid=(n // d,),
        in_specs=[pl.BlockSpec((rows, d), lambda j: (0, 0)),
                  pl.BlockSpec((d, d), lambda j: (0, j)),
                  pl.BlockSpec((1, d), lambda j: (0, j))],
        out_specs=pl.BlockSpec((rows, d), lambda j: (0, j)),
        out_shape=jax.ShapeDtypeStruct((rows, n), F32),
        compiler_params=_cparams("arbitrary"),
        name="adaln_mods",
    )(cc, mod_w, mod_b.reshape(1, n))
    return out.reshape(rows, 1, n)


def _mod_spec(d, k, row, off=0):
    return pl.BlockSpec((1, 1, d), lambda i: (row(i + off), 0, k))


def _even_proj_kernel(x_ref, g_ref, sh_ref, sc_ref, win_ref, qg_ref, wq_ref, kvg_ref, wkv_ref, cos_ref, sin_ref,
                      gbu_ref, q_ref, k_ref, v_ref):
    hn = _rms(x_ref[...], g_ref[...]) * (1.0 + sc_ref[0]) + sh_ref[0]
    hb = hn.astype(BF16)
    n_gbu = 3 * CONV_CH
    for c0 in range(0, n_gbu, CONV_CH):
        gbu_ref[:, c0:c0 + CONV_CH] = _dot(hb, win_ref[:, c0:c0 + CONV_CH]).astype(gbu_ref.dtype)
    rest = _dot(hb, win_ref[:, n_gbu:])
    cq = rest[:, :Q_RANK]
    ckv = rest[:, Q_RANK:Q_RANK + KV_RANK]
    kr = rest[:, Q_RANK + KV_RANK:Q_RANK + KV_RANK + LANES]
    kr_sw = rest[:, Q_RANK + KV_RANK + LANES:]
    cos = cos_ref[...]
    sin = sin_ref[...]
    kr_rot = kr * cos + kr_sw * sin
    qq = _dot(_rms(cq, qg_ref[...]).astype(BF16), wq_ref[...])
    kv = _dot(_rms(ckv, kvg_ref[...]).astype(BF16), wkv_ref[...])
    hw = MLA_HEADS * LANES
    for h in range(MLA_HEADS):
        sl = slice(h * LANES, (h + 1) * LANES)
        qh = (qq[:, sl] * cos + qq[:, hw + h * LANES:hw + (h + 1) * LANES] * sin) * MLA_SCALE
        q_ref[:, sl] = qh.astype(q_ref.dtype)
        k_ref[:, sl] = (kv[:, sl] + kr_rot).astype(k_ref.dtype)
    v_ref[...] = kv[:, hw:].astype(v_ref.dtype)


def _even_proj(x2, g, mods, row, n_tiles, w_in, qg, wq, kvg, wkv, cos_t, sin_t, tab_idx):
    t, d = x2.shape
    tm = ROW_TILE
    n_in = w_in.shape[1]
    hw = MLA_HEADS * LANES
    const = lambda i: (0, 0)
    return pl.pallas_call(
        _even_proj_kernel,
        grid=(n_tiles,),
        in_specs=[pl.BlockSpec((tm, d), lambda i: (i, 0)),
                  pl.BlockSpec((1, d), const),
                  _mod_spec(d, 0, row), _mod_spec(d, 1, row),
                  pl.BlockSpec((d, n_in), const),
                  pl.BlockSpec((1, Q_RANK), const),
                  pl.BlockSpec(wq.shape, const),
                  pl.BlockSpec((1, KV_RANK), const),
                  pl.BlockSpec(wkv.shape, const),
                  pl.BlockSpec((tm, LANES), lambda i: (tab_idx(i), 0)),
                  pl.BlockSpec((tm, LANES), lambda i: (tab_idx(i), 0))],
        out_specs=[pl.BlockSpec((tm, 3 * CONV_CH), lambda i: (i, 0)),
                   pl.BlockSpec((tm, hw), lambda i: (i, 0)),
                   pl.BlockSpec((tm, hw), lambda i: (i, 0)),
                   pl.BlockSpec((tm, MLA_HEADS * V_HEAD), lambda i: (i, 0))],
        out_shape=[jax.ShapeDtypeStruct((t, 3 * CONV_CH), BF16),
                   jax.ShapeDtypeStruct((t, hw), BF16),
                   jax.ShapeDtypeStruct((t, hw), BF16),
                   jax.ShapeDtypeStruct((t, MLA_HEADS * V_HEAD), BF16)],
        compiler_params=_cparams("arbitrary"),
        name="even_proj",
    )(x2, g, mods, mods, w_in, qg, wq, kvg, wkv, cos_t, sin_t)


def _shifted(z, prev_row, next_row):
    tm = z.shape[0]
    ridx = lax.broadcasted_iota(jnp.int32, z.shape, 0)
    zm = jnp.where(ridx == 0, prev_row, pltpu.roll(z, 1, 0))
    zp = jnp.where(ridx == tm - 1, next_row, pltpu.roll(z, tm - 1, 0))
    return zm, zp


def _seq_edges(i, n_ctx_tiles, ctx_tps, x_tps):
    j = jnp.where(i < n_ctx_tiles, i % ctx_tps, (i - n_ctx_tiles) % x_tps)
    tps = jnp.where(i < n_ctx_tiles, ctx_tps, x_tps)
    keep_prev = (j != 0).astype(F32)
    keep_next = (j != tps - 1).astype(F32)
    return keep_prev, keep_next


def _even_conv_kernel(cur_ref, prev_ref, next_ref, w_ref, o_ref, *, n_ctx_tiles, ctx_tps, x_tps):
    i = pl.program_id(0)
    keep_prev, keep_next = _seq_edges(i, n_ctx_tiles, ctx_tps, x_tps)
    c = CONV_CH
    cur = cur_ref[...].astype(F32)
    z = cur[:, c:2 * c] * cur[:, 2 * c:]
    pr = prev_ref[...].astype(F32)
    nx = next_ref[...].astype(F32)
    zprev = (pr[HALO_ROWS - 1:, c:2 * c] * pr[HALO_ROWS - 1:, 2 * c:]) * keep_prev
    znext = (nx[0:1, c:2 * c] * nx[0:1, 2 * c:]) * keep_next
    zm, zp = _shifted(z, zprev, znext)
    w = w_ref[...]
    y = zm * w[0:1] + z * w[1:2] + zp * w[2:3]
    o_ref[...] = (cur[:, :c] * y).astype(o_ref.dtype)


HALO_ROWS = 16


def _halo_specs(tm, width, n_rows, colmap):
    nb = n_rows // HALO_ROWS
    r = tm // HALO_ROWS
    prev = pl.BlockSpec((HALO_ROWS, width), lambda i, *a: (jnp.maximum(i * r - 1, 0), colmap(*a)))
    nxt = pl.BlockSpec((HALO_ROWS, width), lambda i, *a: (jnp.minimum((i + 1) * r, nb - 1), colmap(*a)))
    return prev, nxt


def _even_conv(gbu, conv_w, n_ctx_tiles, ctx_tps, x_tps):
    t, n = gbu.shape
    tm = ROW_TILE
    prev, nxt = _halo_specs(tm, n, t, lambda: 0)
    return pl.pallas_call(
        functools.partial(_even_conv_kernel, n_ctx_tiles=n_ctx_tiles, ctx_tps=ctx_tps, x_tps=x_tps),
        grid=(t // tm,),
        in_specs=[pl.BlockSpec((tm, n), lambda i: (i, 0)), prev, nxt,
                  pl.BlockSpec(conv_w.shape, lambda i: (0, 0))],
        out_specs=pl.BlockSpec((tm, CONV_CH), lambda i: (i, 0)),
        out_shape=jax.ShapeDtypeStruct((t, CONV_CH), BF16),
        compiler_params=_cparams("arbitrary"),
        name="even_conv",
    )(gbu, gbu, gbu, conv_w)


def _attn_kernel(q_ref, kc_ref, kx_ref, vc_ref, vx_ref, o_ref, *, n_ctx_q):
    qi = pl.program_id(1)

    def run(use_x):
        for hp in range(MLA_HEADS // 2):
            outs = []
            for h in (2 * hp, 2 * hp + 1):
                ks = slice(h * LANES, (h + 1) * LANES)
                vs = slice(h * V_HEAD, (h + 1) * V_HEAD)
                q = q_ref[:, ks]
                sc = _dot_nt(q, kc_ref[:, ks])
                m = jnp.max(sc, axis=1, keepdims=True)
                if use_x:
                    sx = _dot_nt(q, kx_ref[:, ks])
                    m = jnp.maximum(m, jnp.max(sx, axis=1, keepdims=True))
                pc = jnp.exp(sc - m)
                l = jnp.sum(pc, axis=1, keepdims=True)
                o = _dot(pc.astype(BF16), vc_ref[:, vs])
                if use_x:
                    px = jnp.exp(sx - m)
                    l = l + jnp.sum(px, axis=1, keepdims=True)
                    o = o + _dot(px.astype(BF16), vx_ref[:, vs])
                outs.append(o * (1.0 / l))
            o_ref[:, hp * LANES:(hp + 1) * LANES] = jnp.concatenate(outs, axis=1).astype(o_ref.dtype)

    @pl.when(qi < n_ctx_q)
    def _():
        run(False)

    @pl.when(qi >= n_ctx_q)
    def _():
        run(True)


def _attention(q, k, v, batch, ctx_len, seq_len):
    tq = ROW_TILE
    n_ctx_q = ctx_len // tq
    n_x_q = seq_len // tq
    ctx_blocks = batch * ctx_len // tq
    assert (batch * ctx_len) % seq_len == 0
    x_blk0 = batch * ctx_len // seq_len
    hw = MLA_HEADS * LANES
    vw = MLA_HEADS * V_HEAD

    def qrow(b, qi):
        return jnp.where(qi < n_ctx_q, b * n_ctx_q + qi, ctx_blocks + b * n_x_q + (qi - n_ctx_q))

    return pl.pallas_call(
        functools.partial(_attn_kernel, n_ctx_q=n_ctx_q),
        grid=(batch, n_ctx_q + n_x_q),
        in_specs=[pl.BlockSpec((tq, hw), lambda b, qi: (qrow(b, qi), 0)),
                  pl.BlockSpec((ctx_len, hw), lambda b, qi: (b, 0)),
                  pl.BlockSpec((seq_len, hw), lambda b, qi: (x_blk0 + b, 0)),
                  pl.BlockSpec((ctx_len, vw), lambda b, qi: (b, 0)),
                  pl.BlockSpec((seq_len, vw), lambda b, qi: (x_blk0 + b, 0))],
        out_specs=pl.BlockSpec((tq, vw), lambda b, qi: (qrow(b, qi), 0)),
        out_shape=jax.ShapeDtypeStruct((q.shape[0], vw), BF16),
        compiler_params=_cparams("arbitrary", "arbitrary"),
        name="mla_attention",
    )(q, k, k, v, v)


def _resid_mm_kernel(*refs, n_lhs):
    x_ref, gate_ref = refs[0], refs[1]
    lhs = refs[2:2 + n_lhs]
    ws = refs[2 + n_lhs:2 + 2 * n_lhs]
    o_ref = refs[2 + 2 * n_lhs]
    acc = _dot(lhs[0][...], ws[0][...])
    for a_ref, w_ref in zip(lhs[1:], ws[1:]):
        acc = acc + _dot(a_ref[...], w_ref[...])
    o_ref[...] = x_ref[...] + gate_ref[0] * acc


def _resid_mm(x2, mods, k_gate, row, lhs_list, w_list, tm=512):
    t, d = x2.shape
    n = len(lhs_list)
    in_specs = [pl.BlockSpec((tm, d), lambda i: (i, 0)), _mod_spec(d, k_gate, lambda i: row(i * tm // ROW_TILE))]
    in_specs += [pl.BlockSpec((tm, a.shape[1]), lambda i: (i, 0)) for a in lhs_list]
    in_specs += [pl.BlockSpec(w.shape, lambda i: (0, 0)) for w in w_list]
    return pl.pallas_call(
        functools.partial(_resid_mm_kernel, n_lhs=n),
        grid=(t // tm,),
        in_specs=in_specs,
        out_specs=pl.BlockSpec((tm, d), lambda i: (i, 0)),
        out_shape=jax.ShapeDtypeStruct((t, d), F32),
        compiler_params=_cparams("arbitrary"),
        name="resid_matmul",
    )(x2, mods, *lhs_list, *w_list)


def _ffn_up_kernel(x_ref, g_ref, sh_ref, sc_ref, w_ref, o_ref, *, ffn, chunk):
    hn = _rms(x_ref[...], g_ref[...]) * (1.0 + sc_ref[0]) + sh_ref[0]
    hb = hn.astype(BF16)
    for c0 in range(0, ffn, chunk):
        gate = _dot(hb, w_ref[:, c0:c0 + chunk])
        up = _dot(hb, w_ref[:, ffn + c0:ffn + c0 + chunk])
        o_ref[:, c0:c0 + chunk] = (_silu(gate) * up).astype(o_ref.dtype)


def _ffn_up(x2, g, mods, row, w1):
    t, d = x2.shape
    tm = ROW_TILE
    ffn = w1.shape[1] // 2
    chunk = 256
    assert ffn % chunk == 0
    return pl.pallas_call(
        functools.partial(_ffn_up_kernel, ffn=ffn, chunk=chunk),
        grid=(t // tm,),
        in_specs=[pl.BlockSpec((tm, d), lambda i: (i, 0)),
                  pl.BlockSpec((1, d), lambda i: (0, 0)),
                  _mod_spec(d, 3, row), _mod_spec(d, 4, row),
                  pl.BlockSpec(w1.shape, lambda i: (0, 0))],
        out_specs=pl.BlockSpec((tm, ffn), lambda i: (i, 0)),
        out_shape=jax.ShapeDtypeStruct((t, ffn), BF16),
        compiler_params=_cparams("arbitrary"),
        name="ffn_up",
    )(x2, g, mods, mods, w1)


X_HALO = 8


def _odd_proj_kernel(x_ref, xp_ref, xn_ref, g_ref, sh_ref, sc_ref, w_ref, cw_ref, qkv_ref, gate_ref, abt_ref,
                     *, chunk, n_ctx_tiles, ctx_tps, x_tps):
    i = pl.program_id(0)
    keep_prev, keep_next = _seq_edges(i, n_ctx_tiles, ctx_tps, x_tps)
    tm = x_ref.shape[0]
    xs = jnp.concatenate([x_ref[...], xp_ref[...], xn_ref[...]], axis=0)
    hb = (_rms(xs, g_ref[...]) * (1.0 + sc_ref[0]) + sh_ref[0]).astype(BF16)
    n_qkv = qkv_ref.shape[0] * LANES
    n_gate = gate_ref.shape[1]
    cw = cw_ref[...]
    for c0 in range(0, n_qkv, chunk):
        z_all = _dot(hb, w_ref[:, c0:c0 + chunk])
        z = z_all[:tm]
        zprev = z_all[tm + X_HALO - 1:tm + X_HALO] * keep_prev
        znext = z_all[tm + X_HALO:tm + X_HALO + 1] * keep_next
        zm, zp = _shifted(z, zprev, znext)
        y = _silu(zm * cw[0:1, c0:c0 + chunk] + z * cw[1:2, c0:c0 + chunk] + zp * cw[2:3, c0:c0 + chunk])
        for hh in range(chunk // LANES):
            head = c0 // LANES + hh
            yh = y[:, hh * LANES:(hh + 1) * LANES]
            if head < 2 * GDN_HEADS:
                nrm = lax.rsqrt(jnp.sum(yh * yh, axis=-1, keepdims=True) + NORM_EPS)
                if head < GDN_HEADS:
                    nrm = nrm * (GDN_DK ** -0.5)
                yh = yh * nrm
            qkv_ref[head] = yh
    hb_cur = hb[:tm]
    for c0 in range(0, n_gate, chunk):
        gate_ref[:, c0:c0 + chunk] = _dot(hb_cur, w_ref[:, n_qkv + c0:n_qkv + c0 + chunk]).astype(gate_ref.dtype)
    abt_ref[...] = _dot(hb_cur, w_ref[:, n_qkv + n_gate:])


def _odd_proj(x2, g, mods, row, w_in, conv_w, n_ctx_tiles, ctx_tps, x_tps):
    t, d = x2.shape
    tm = ROW_TILE
    n_qkv = 3 * GDN_QK
    n_gate = GDN_QK
    r = tm // X_HALO
    nb = t // X_HALO
    return pl.pallas_call(
        functools.partial(_odd_proj_kernel, chunk=512, n_ctx_tiles=n_ctx_tiles, ctx_tps=ctx_tps, x_tps=x_tps),
        grid=(t // tm,),
        in_specs=[pl.BlockSpec((tm, d), lambda i: (i, 0)),
                  pl.BlockSpec((X_HALO, d), lambda i: (jnp.maximum(i * r - 1, 0), 0)),
                  pl.BlockSpec((X_HALO, d), lambda i: (jnp.minimum((i + 1) * r, nb - 1), 0)),
                  pl.BlockSpec((1, d), lambda i: (0, 0)),
                  _mod_spec(d, 0, row), _mod_spec(d, 1, row),
                  pl.BlockSpec(w_in.shape, lambda i: (0, 0)),
                  pl.BlockSpec(conv_w.shape, lambda i: (0, 0))],
        out_specs=[pl.BlockSpec((n_qkv // LANES, tm, LANES), lambda i: (0, i, 0)),
                   pl.BlockSpec((tm, n_gate), lambda i: (i, 0)),
                   pl.BlockSpec((tm, LANES), lambda i: (i, 0))],
        out_shape=[jax.ShapeDtypeStruct((n_qkv // LANES, t, LANES), F32),
                   jax.ShapeDtypeStruct((t, n_gate), BF16),
                   jax.ShapeDtypeStruct((t, LANES), F32)],
        compiler_params=_cparams("arbitrary"),
        name="odd_proj",
    )(x2, x2, x2, g, mods, mods, w_in, conv_w)


def _gdn_kernel(qkv_ref, abt_ref, alog_ref, dtb_ref, o_ref,
                s_ref, gc_s, beta_s, gtot_s, uw_s, qg_s, kd_s, attn_s, vnew_s, gtc_s):
    d = pl.program_id(1)
    step = pl.program_id(2)
    n = GDN_STEP
    c = GDN_CHUNK
    nchunk = n // c
    h_n = GDN_HEADS
    sgn = 1 - 2 * d

    @pl.when(step == 0)
    def _():
        s_ref[...] = jnp.zeros(s_ref.shape, s_ref.dtype)

    ri = lax.broadcasted_iota(jnp.int32, (n, n), 0)
    ci = lax.broadcasted_iota(jnp.int32, (n, n), 1)
    same = (ri // c) == (ci // c)
    rel = (ri - ci) * sgn
    incl = same & (rel >= 0)
    strict = same & (rel > 0)
    eye = (ri == ci).astype(F32)

    hrow = pl.ds(d * h_n, h_n)
    a_all = abt_ref[0, hrow, :]
    bt_all = abt_ref[0, pl.ds(2 * h_n + d * h_n, h_n), :]
    sp_in = a_all + dtb_ref[hrow, :]
    softplus = jnp.maximum(sp_in, 0.0) + jnp.log(1.0 + jnp.exp(-jnp.abs(sp_in)))
    g_all = -jnp.exp(alog_ref[hrow, :]) * softplus
    beta_s[...] = 1.0 / (1.0 + jnp.exp(-bt_all))
    cum_m = jnp.where(same & ((ci - ri) * sgn >= 0), 1.0, 0.0).astype(BF16)
    gc_s[...] = _dot_f32_exact_rhs(g_all, cum_m)
    gtot_s[...] = _dot_f32_exact_rhs(g_all, same.astype(BF16))

    grp = uw_s.shape[0]
    n_sq = (c - 1).bit_length() - 1

    def head_group(gi, carry):
        heads = [gi * grp + u for u in range(grp)]
        us = range(grp)
        q = [qkv_ref[h] for h in heads]
        k = [qkv_ref[h_n + h] for h in heads]
        v = [qkv_ref[2 * h_n + h] for h in heads]
        gc_row = [gc_s[pl.ds(h, 1), :] for h in heads]
        gc_col = [jnp.sum(eye * gc_row[u], axis=1, keepdims=True) for u in us]
        beta_col = [jnp.sum(eye * beta_s[pl.ds(h, 1), :], axis=1, keepdims=True) for h in heads]
        gtot_col = [jnp.sum(eye * gtot_s[pl.ds(h, 1), :], axis=1, keepdims=True) for h in heads]
        decay = [jnp.exp(jnp.where(incl, gc_col[u] - gc_row[u], -1e30)) for u in us]
        kb = [k[u] * beta_col[u] for u in us]
        k16 = [k[u].astype(BF16) for u in us]
        p = [jnp.where(strict, -(_dot_nt(kb[u].astype(BF16), k16[u]) * decay[u]), 0.0) for u in us]
        for u in us:
            attn_s[u] = (_dot_nt(q[u].astype(BF16), k16[u]) * decay[u]).astype(attn_s.dtype)
            gtc_s[u] = jnp.broadcast_to(gtot_col[u], gtc_s.shape[1:])
        tinv = [eye + p[u] for u in us]
        p16 = [p[u].astype(BF16) for u in us]
        p = [_dot(p16[u], p16[u]) for u in us]
        for _ in range(n_sq - 1):
            p16 = [p[u].astype(BF16) for u in us]
            tp = [_dot(jnp.concatenate([tinv[u].astype(BF16), p16[u]], axis=0), p16[u]) for u in us]
            tinv = [tinv[u] + tp[u][:n] for u in us]
            p = [tp[u][n:] for u in us]
        tinv = [tinv[u] + _dot(tinv[u].astype(BF16), p[u].astype(BF16)) for u in us]
        egc = [jnp.exp(gc_col[u]) for u in us]
        for u in us:
            rhs = jnp.concatenate([v[u] * beta_col[u], kb[u] * egc[u]], axis=1).astype(BF16)
            uw_s[u] = _dot(tinv[u].astype(BF16), rhs)
            qg_s[u] = q[u] * egc[u]
            kd_s[u] = k[u] * jnp.exp(gtot_col[u] - gc_col[u])
            vnew_s[u] = jnp.zeros(vnew_s.shape[1:], vnew_s.dtype)
        s = [s_ref[h] for h in heads]
        for cstep in range(nchunk):
            cc = jnp.where(d == 0, cstep, nchunk - 1 - cstep)
            r0 = pl.multiple_of(cc * c, c)
            rows = pl.ds(r0, c)
            ws_qs = [_dot(jnp.concatenate([uw_s[u, rows, LANES:], qg_s[u, rows, :]], axis=0).astype(BF16),
                          s[u].astype(BF16)) for u in us]
            v_new = [uw_s[u, rows, :LANES] - ws_qs[u][:c] for u in us]
            for u in us:
                vnew_s[u, rows, :] = v_new[u].astype(vnew_s.dtype)
            for u in us:
                o_ref[0, heads[u], rows, :] = ws_qs[u][c:] + _dot(attn_s[u, rows, :], vnew_s[u])
            s = [s[u] * jnp.exp(gtc_s[u, pl.ds(r0, 1), :])
                 + _dot_tn(kd_s[u, rows, :].astype(BF16), v_new[u].astype(BF16)) for u in us]
        for u in us:
            s_ref[heads[u]] = s[u]
        return carry

    lax.fori_loop(0, h_n // grp, head_group, 0)


def _gdn(qkvn, abt_chunks, alog_b, dtb_b, batch, ctx_len, seq_len):
    nh3, t, _ = qkvn.shape
    n = GDN_STEP
    nc = ctx_len // n
    nx = seq_len // n
    ctx_blocks = batch * nc

    def rowblk(b, d, s):
        sc = jnp.where(d == 0, s, nc - 1 - s)
        sx = jnp.where(d == 0, s - nc, nx - 1 - (s - nc))
        return jnp.where(s < nc, b * nc + sc, ctx_blocks + b * nx + sx)

    vm = pltpu.VMEM
    g = GDN_HEAD_GROUP
    return pl.pallas_call(
        _gdn_kernel,
        grid=(batch, 2, nc + nx),
        in_specs=[pl.BlockSpec((nh3, n, LANES), lambda b, d, s: (0, rowblk(b, d, s), 0)),
                  pl.BlockSpec((1, 4 * GDN_HEADS, n), lambda b, d, s: (rowblk(b, d, s), 0, 0)),
                  pl.BlockSpec(alog_b.shape, lambda b, d, s: (0, 0)),
                  pl.BlockSpec(dtb_b.shape, lambda b, d, s: (0, 0))],
        out_specs=pl.BlockSpec((1, GDN_HEADS, n, LANES), lambda b, d, s: (d, 0, rowblk(b, d, s), 0)),
        out_shape=jax.ShapeDtypeStruct((2, GDN_HEADS, t, LANES), F32),
        scratch_shapes=[vm((GDN_HEADS, GDN_DK, LANES), F32),
                        vm((GDN_HEADS, n), F32),
                        vm((GDN_HEADS, n), F32),
                        vm((GDN_HEADS, n), F32),
                        vm((g, n, 2 * LANES), F32),
                        vm((g, n, LANES), F32),
                        vm((g, n, LANES), F32),
                        vm((g, n, n), BF16),
                        vm((g, n, LANES), BF16),
                        vm((g, n, LANES), F32)],
        compiler_params=_cparams("arbitrary", "arbitrary", "arbitrary"),
        name="gated_delta",
    )(qkvn, abt_chunks, alog_b, dtb_b)


def _odd_out_kernel(x_ref, gate_ref, o_ref_in, z_ref, og_ref, w_ref, out_ref):
    parts = []
    og = og_ref[...]
    for h in range(GDN_HEADS):
        o = o_ref_in[0, h] + o_ref_in[1, h]
        y = _rms(o, og) * _silu(z_ref[:, h * LANES:(h + 1) * LANES].astype(F32))
        parts.append(y.astype(BF16))
    y_all = jnp.concatenate(parts, axis=1)
    out_ref[...] = x_ref[...] + gate_ref[0] * _dot(y_all, w_ref[...])


def _odd_out(x2, mods, row, o_dirs, z, o_norm_g, w_out, row_off, n_rows):
    d = x2.shape[1]
    tm = ROW_TILE
    return pl.pallas_call(
        _odd_out_kernel,
        grid=(n_rows // tm,),
        in_specs=[pl.BlockSpec((tm, d), lambda i: (i + row_off, 0)),
                  _mod_spec(d, 2, row, row_off),
                  pl.BlockSpec((2, GDN_HEADS, tm, LANES), lambda i: (0, 0, i + row_off, 0)),
                  pl.BlockSpec((tm, z.shape[1]), lambda i: (i + row_off, 0)),
                  pl.BlockSpec((1, LANES), lambda i: (0, 0)),
                  pl.BlockSpec(w_out.shape, lambda i: (0, 0))],
        out_specs=pl.BlockSpec((tm, d), lambda i: (i, 0)),
        out_shape=jax.ShapeDtypeStruct((n_rows, d), F32),
        compiler_params=_cparams("arbitrary"),
        name="odd_out",
    )(x2, mods, o_dirs, z, o_norm_g, w_out)


def _router_kernel(x_ref, g_ref, sh_ref, sc_ref, rw_ref, h_ref, info_ref, cnt_ref, carry_s):
    i = pl.program_id(0)

    @pl.when(i == 0)
    def _():
        carry_s[...] = jnp.zeros(carry_s.shape, carry_s.dtype)

    hn = _rms(x_ref[...], g_ref[...]) * (1.0 + sc_ref[0]) + sh_ref[0]
    h_ref[...] = hn
    tm = hn.shape[0]
    lane = lax.broadcasted_iota(jnp.int32, (tm, LANES), 1)
    logits = jnp.where(lane < N_EXPERTS, _dot_f32(hn, rw_ref[...]), -jnp.inf)
    m1 = jnp.max(logits, axis=1, keepdims=True)
    i1 = jnp.min(jnp.where(logits == m1, lane, LANES), axis=1, keepdims=True)
    rest = jnp.where(lane == i1, -jnp.inf, logits)
    m2 = jnp.max(rest, axis=1, keepdims=True)
    i2 = jnp.min(jnp.where(rest == m2, lane, LANES), axis=1, keepdims=True)
    e2 = jnp.exp(m2 - m1)
    w1 = 1.0 / (1.0 + e2)
    w2 = e2 / (1.0 + e2)
    oh1 = (lane == i1).astype(F32)
    oh2 = (lane == i2).astype(F32)
    cnt = oh1 + oh2
    ri = lax.broadcasted_iota(jnp.int32, (tm, tm), 0)
    ci = lax.broadcasted_iota(jnp.int32, (tm, tm), 1)
    before = (ci < ri).astype(BF16)
    pre = _dot(before, cnt.astype(BF16)) + carry_s[0:1, :]
    r1 = jnp.sum(pre * oh1, axis=1, keepdims=True)
    r2 = jnp.sum(pre * oh2, axis=1, keepdims=True)
    info = jnp.where(lane == 0, i1.astype(F32), 0.0)
    info = jnp.where(lane == 1, i2.astype(F32), info)
    info = jnp.where(lane == 2, w1, info)
    info = jnp.where(lane == 3, w2, info)
    info = jnp.where(lane == 4, r1, info)
    info = jnp.where(lane == 5, r2, info)
    info_ref[...] = info
    total = carry_s[0:1, :] + jnp.sum(cnt, axis=0, keepdims=True)
    carry_s[...] = jnp.broadcast_to(total, carry_s.shape)
    cnt_ref[...] = jnp.broadcast_to(total, cnt_ref.shape)


def _router(x2, g, mods, row, router_w_pad):
    t, d = x2.shape
    tm = ROW_TILE
    return pl.pallas_call(
        _router_kernel,
        grid=(t // tm,),
        in_specs=[pl.BlockSpec((tm, d), lambda i: (i, 0)),
                  pl.BlockSpec((1, d), lambda i: (0, 0)),
                  _mod_spec(d, 3, row), _mod_spec(d, 4, row),
                  pl.BlockSpec(router_w_pad.shape, lambda i: (0, 0))],
        out_specs=[pl.BlockSpec((tm, d), lambda i: (i, 0)),
                   pl.BlockSpec((tm, LANES), lambda i: (i, 0)),
                   pl.BlockSpec((8, LANES), lambda i: (0, 0))],
        out_shape=[jax.ShapeDtypeStruct((t, d), F32),
                   jax.ShapeDtypeStruct((t, LANES), F32),
                   jax.ShapeDtypeStruct((8, LANES), F32)],
        scratch_shapes=[pltpu.VMEM((8, LANES), F32)],
        compiler_params=_cparams("arbitrary"),
        name="moe_router",
    )(x2, g, mods, mods, router_w_pad)


def _row_copy(src_hbm, src_row, dst_ref, dst_row, sem):
    return pltpu.make_async_copy(src_hbm.at[pl.ds(src_row, 1)], dst_ref.at[pl.ds(dst_row, 1)], sem)


def _dispatch_kernel(dest_ref, h_ref, init_hbm, hs_hbm, sem, *, tm):
    del init_hbm

    def issue(r, carry):
        _row_copy(h_ref, r, hs_hbm, dest_ref[0, 0, r], sem).start(priority=0)
        _row_copy(h_ref, r, hs_hbm, dest_ref[0, 0, tm + r], sem).start(priority=1)
        return carry

    lax.fori_loop(0, tm, issue, 0)

    def drain(r, carry):
        _row_copy(h_ref, 0, hs_hbm, 0, sem).wait()
        _row_copy(h_ref, 0, hs_hbm, 0, sem).wait()
        return carry

    lax.fori_loop(0, tm, drain, 0)


def _dispatch(h, dest_tiles, n_sorted_rows):
    t, d = h.shape
    tm = ROW_TILE
    init = jnp.zeros((n_sorted_rows, d), h.dtype)
    return pl.pallas_call(
        functools.partial(_dispatch_kernel, tm=tm),
        grid=(t // tm,),
        in_specs=[pl.BlockSpec((1, 1, 2 * tm), lambda i: (i, 0, 0), memory_space=pltpu.SMEM),
                  pl.BlockSpec((tm, d), lambda i: (i, 0)),
                  pl.BlockSpec(memory_space=pl.ANY)],
        out_specs=pl.BlockSpec(memory_space=pl.ANY),
        out_shape=jax.ShapeDtypeStruct((n_sorted_rows, d), h.dtype),
        scratch_shapes=[pltpu.SemaphoreType.DMA(())],
        input_output_aliases={2: 0},
        compiler_params=_cparams("arbitrary"),
        name="moe_dispatch",
    )(dest_tiles, h, init)


def _expert_kernel(te_ref, nt_ref, hs_ref, w1_ref, w2_ref, ys_ref, *, ffn, chunk):
    del te_ref
    j = pl.program_id(0)

    @pl.when(j < nt_ref[0])
    def _():
        hb = hs_ref[...].astype(BF16)
        acc = None
        for c0 in range(0, ffn, chunk):
            gate = _dot(hb, w1_ref[:, c0:c0 + chunk])
            up = _dot(hb, w1_ref[:, ffn + c0:ffn + c0 + chunk])
            part = _dot((_silu(gate) * up).astype(BF16), w2_ref[c0:c0 + chunk, :])
            acc = part if acc is None else acc + part
        ys_ref[...] = acc

    @pl.when(j >= nt_ref[0])
    def _():
        ys_ref[...] = jnp.zeros(ys_ref.shape, ys_ref.dtype)


def _experts(hs, tile_expert, n_tiles_used, w1, w2):
    r, d = hs.shape
    tm = EXPERT_TILE
    ffn = w2.shape[1]
    assert ffn % EXPERT_FCHUNK == 0

    def tile(j, nt):
        return jnp.minimum(j, nt[0] - 1)

    grid_spec = pltpu.PrefetchScalarGridSpec(
        num_scalar_prefetch=2,
        grid=(r // tm,),
        in_specs=[pl.BlockSpec((tm, d), lambda j, te, nt: (tile(j, nt), 0)),
                  pl.BlockSpec((None, d, 2 * ffn), lambda j, te, nt: (te[tile(j, nt)], 0, 0)),
                  pl.BlockSpec((None, ffn, d), lambda j, te, nt: (te[tile(j, nt)], 0, 0))],
        out_specs=pl.BlockSpec((tm, d), lambda j, te, nt: (j, 0)),
    )
    return pl.pallas_call(
        functools.partial(_expert_kernel, ffn=ffn, chunk=EXPERT_FCHUNK),
        grid_spec=grid_spec,
        out_shape=jax.ShapeDtypeStruct((r, d), F32),
        compiler_params=pltpu.CompilerParams(dimension_semantics=("arbitrary",),
                                             vmem_limit_bytes=EXPERT_VMEM_LIMIT_BYTES),
        name="moe_experts",
    )(tile_expert, n_tiles_used, hs, w1, w2)


def _combine_kernel(dest_ref, x_ref, gate_ref, info_ref, ys_hbm, o_ref, y1_s, y2_s, sem, *, tm):
    def issue(r, carry):
        _row_copy(ys_hbm, dest_ref[0, 0, r], y1_s, r, sem).start(priority=0)
        _row_copy(ys_hbm, dest_ref[0, 0, tm + r], y2_s, r, sem).start(priority=1)
        return carry

    lax.fori_loop(0, tm, issue, 0)

    def drain(r, carry):
        _row_copy(ys_hbm, 0, y1_s, 0, sem).wait()
        _row_copy(ys_hbm, 0, y2_s, 0, sem).wait()
        return carry

    lax.fori_loop(0, tm, drain, 0)
    info = info_ref[...]
    y = info[:, 2:3] * y1_s[...] + info[:, 3:4] * y2_s[...]
    o_ref[...] = x_ref[...] + gate_ref[0] * y


def _combine(x2, mods, row, info, dest_tiles, ys):
    t, d = x2.shape
    tm = ROW_TILE
    return pl.pallas_call(
        functools.partial(_combine_kernel, tm=tm),
        grid=(t // tm,),
        in_specs=[pl.BlockSpec((1, 1, 2 * tm), lambda i: (i, 0, 0), memory_space=pltpu.SMEM),
                  pl.BlockSpec((tm, d), lambda i: (i, 0)),
                  _mod_spec(d, 5, row),
                  pl.BlockSpec((tm, LANES), lambda i: (i, 0)),
                  pl.BlockSpec(memory_space=pl.ANY)],
        out_specs=pl.BlockSpec((tm, d), lambda i: (i, 0)),
        out_shape=jax.ShapeDtypeStruct((t, d), F32),
        scratch_shapes=[pltpu.VMEM((tm, d), F32), pltpu.VMEM((tm, d), F32), pltpu.SemaphoreType.DMA(())],
        compiler_params=_cparams("arbitrary"),
        name="moe_combine",
    )(dest_tiles, x2, mods, info, ys)


def _moe(x2, g, mods, row, router_w, w1, w2):
    t, d = x2.shape
    tm = ROW_TILE
    te = EXPERT_TILE
    rw = jnp.pad(router_w, ((0, 0), (0, LANES - N_EXPERTS)))
    h, info, counts = _router(x2, g, mods, row, rw)
    cnt = counts[0, :N_EXPERTS].astype(jnp.int32)
    gsz = ((cnt + te - 1) // te) * te
    ends = jnp.cumsum(gsz)
    offs = ends - gsz
    e1 = info[:, 0].astype(jnp.int32)
    e2 = info[:, 1].astype(jnp.int32)
    dest1 = offs[e1] + info[:, 4].astype(jnp.int32)
    dest2 = offs[e2] + info[:, 5].astype(jnp.int32)
    dest_tiles = jnp.concatenate([dest1.reshape(t // tm, 1, tm), dest2.reshape(t // tm, 1, tm)], axis=2)
    n_sorted = 2 * t + N_EXPERTS * te
    n_tiles_max = n_sorted // te
    starts = jnp.arange(n_tiles_max, dtype=jnp.int32) * te
    tile_expert = jnp.minimum(jnp.sum((starts[:, None] >= ends[None, :]).astype(jnp.int32), axis=1), N_EXPERTS - 1)
    n_used = (ends[-1] // te).astype(jnp.int32).reshape(1)
    hs = _dispatch(h, dest_tiles, n_sorted)
    ys = _experts(hs, tile_expert, n_used, w1, w2)
    return _combine(x2, mods, row, info, dest_tiles, ys)


def _final_norm_kernel(x_ref, g_ref, o_ref):
    o_ref[...] = _rms(x_ref[...], g_ref[...])


def _final_norm(x2, g, row_off, n_rows):
    d = x2.shape[1]
    tm = 512
    off = row_off * ROW_TILE // tm
    return pl.pallas_call(
        _final_norm_kernel,
        grid=(n_rows // tm,),
        in_specs=[pl.BlockSpec((tm, d), lambda i: (i + off, 0)), pl.BlockSpec((1, d), lambda i: (0, 0))],
        out_specs=pl.BlockSpec((tm, d), lambda i: (i, 0)),
        out_shape=jax.ShapeDtypeStruct((n_rows, d), F32),
        compiler_params=_cparams("arbitrary"),
        name="final_norm",
    )(x2, g)


_ROPE_SWAP = tuple(list(range(8, 16)) + list(range(0, 8)) + list(range(24, 32)) + list(range(16, 24)))


def _even_weights(w_in, w_uq, w_ukv):
    d = w_in.shape[0]
    n0 = 3 * CONV_CH + Q_RANK + KV_RANK
    kr = w_in[:, n0:n0 + QK_ROPE]
    kr_sw = kr[:, jnp.array(_ROPE_SWAP)]
    z = lambda n: jnp.zeros((d, n), w_in.dtype)
    w_in_x = jnp.concatenate([w_in[:, :n0], z(QK_NOPE), kr, z(LANES - QK_NOPE - QK_ROPE),
                              z(QK_NOPE), kr_sw, z(LANES - QK_NOPE - QK_ROPE)], axis=1).astype(BF16)
    uq = w_uq.reshape(Q_RANK, MLA_HEADS, QK_NOPE + QK_ROPE)
    pad = LANES - QK_NOPE - QK_ROPE
    uq_plain = jnp.pad(uq, ((0, 0), (0, 0), (0, pad))).reshape(Q_RANK, MLA_HEADS * LANES)
    uq_rope_sw = uq[:, :, QK_NOPE:][:, :, jnp.array(_ROPE_SWAP)]
    uq_sw = jnp.pad(uq_rope_sw, ((0, 0), (0, 0), (QK_NOPE, pad))).reshape(Q_RANK, MLA_HEADS * LANES)
    wq = jnp.concatenate([uq_plain, uq_sw], axis=1).astype(BF16)
    ukv = w_ukv.reshape(KV_RANK, MLA_HEADS, QK_NOPE + V_HEAD)
    k_part = jnp.pad(ukv[:, :, :QK_NOPE], ((0, 0), (0, 0), (0, LANES - QK_NOPE))).reshape(KV_RANK, MLA_HEADS * LANES)
    v_part = ukv[:, :, QK_NOPE:].reshape(KV_RANK, MLA_HEADS * V_HEAD)
    wkv = jnp.concatenate([k_part, v_part], axis=1).astype(BF16)
    return w_in_x, wq, wkv


def _rope_tables(seq_len, ctx_rows):
    n = QK_ROPE // 4
    t = jnp.arange(seq_len)
    inv = ROPE_BASE ** (-jnp.arange(n, dtype=F32) / n)
    ang_r = (t // GRID_W).astype(F32)[:, None] * inv
    ang_c = (t % GRID_W).astype(F32)[:, None] * inv
    cr, sr, cc, sc = jnp.cos(ang_r), jnp.sin(ang_r), jnp.cos(ang_c), jnp.sin(ang_c)
    cos32 = jnp.concatenate([cr, cr, cc, cc], axis=1)
    sin32 = jnp.concatenate([-sr, sr, -sc, sc], axis=1)
    pad = LANES - QK_NOPE - QK_ROPE
    cos_x = jnp.concatenate([jnp.ones((seq_len, QK_NOPE), F32), cos32, jnp.zeros((seq_len, pad), F32)], axis=1)
    sin_x = jnp.concatenate([jnp.zeros((seq_len, QK_NOPE), F32), sin32, jnp.zeros((seq_len, pad), F32)], axis=1)
    cos_c = jnp.concatenate([jnp.ones((ctx_rows, QK_NOPE + QK_ROPE), F32), jnp.zeros((ctx_rows, pad), F32)], axis=1)
    sin_c = jnp.zeros((ctx_rows, LANES), F32)
    return jnp.concatenate([cos_c, cos_x], axis=0), jnp.concatenate([sin_c, sin_x], axis=0)


def _even_layer(xa, mods, dims, tabs, p):
    batch, ctx_len, seq_len = dims
    tm = ROW_TILE
    t = xa.shape[0]
    n_ctx_tiles = batch * ctx_len // tm
    x_tps = seq_len // tm
    row = _mod_row_fn(n_ctx_tiles, x_tps, batch)
    w_in_x, wq, wkv = _even_weights(p["w_in"], p["w_uq"], p["w_ukv"])
    cos_t, sin_t = tabs
    tab_idx = lambda i: jnp.where(i < n_ctx_tiles, 0, 1 + (i - n_ctx_tiles) % x_tps)
    gbu, q, k, v = _even_proj(xa, p["norm1_g"].reshape(1, -1), mods, row, t // tm, w_in_x,
                              p["q_norm_g"].reshape(1, -1), wq, p["kv_norm_g"].reshape(1, -1), wkv,
                              cos_t, sin_t, tab_idx)
    conv_y = _even_conv(gbu, p["conv_w"], n_ctx_tiles, ctx_len // tm, x_tps)
    att = _attention(q, k, v, batch, ctx_len, seq_len)
    w_out = p["w_out"].astype(BF16)
    xa = _resid_mm(xa, mods, 2, row, [conv_y, att], [w_out[:CONV_CH], w_out[CONV_CH:]])
    hmid = _ffn_up(xa, p["norm2_g"].reshape(1, -1), mods, row, p["ffn_w1"].astype(BF16))
    return _resid_mm(xa, mods, 5, row, [hmid], [p["ffn_w2"].astype(BF16)])


def _odd_layer(xa, mods, dims, p, ctx_out):
    batch, ctx_len, seq_len = dims
    tm = ROW_TILE
    t = xa.shape[0]
    n_ctx_tiles = batch * ctx_len // tm
    x_tps = seq_len // tm
    row = _mod_row_fn(n_ctx_tiles, x_tps, batch)
    n_w = p["w_in"].shape[1]
    w_in = jnp.pad(p["w_in"], ((0, 0), (0, 4 * GDN_QK + LANES - n_w))).astype(BF16)
    qkvn, z, abt = _odd_proj(xa, p["norm1_g"].reshape(1, -1), mods, row, w_in, p["qkv_conv_w"],
                             n_ctx_tiles, ctx_len // tm, x_tps)
    abt_chunks = abt[:, :4 * GDN_HEADS].reshape(t // GDN_STEP, GDN_STEP, 4 * GDN_HEADS).transpose(0, 2, 1)
    alog_b = jnp.broadcast_to(p["a_log"].reshape(2 * GDN_HEADS, 1), (2 * GDN_HEADS, GDN_STEP))
    dtb_b = jnp.broadcast_to(p["dt_bias"].reshape(2 * GDN_HEADS, 1), (2 * GDN_HEADS, GDN_STEP))
    o_dirs = _gdn(qkvn, abt_chunks, alog_b, dtb_b, batch, ctx_len, seq_len)
    row_off = 0 if ctx_out else n_ctx_tiles
    n_rows = t - row_off * tm
    xo = _odd_out(xa, mods, row, o_dirs, z, p["o_norm_g"].reshape(1, -1), p["w_out"].astype(BF16), row_off, n_rows)
    row_o = row if ctx_out else _mod_row_fn(0, x_tps, batch)
    return _moe(xo, p["norm2_g"].reshape(1, -1), mods, row_o, p["router_w"],
                p["moe_w1"].astype(BF16), p["moe_w2"].astype(BF16))


_EVEN_NAMES = ("mod_w", "mod_b", "norm1_g", "w_in", "conv_w", "q_norm_g", "w_uq", "kv_norm_g", "w_ukv", "w_out",
               "norm2_g", "ffn_w1", "ffn_w2")
_ODD_NAMES = ("mod_w", "mod_b", "norm1_g", "w_in", "qkv_conv_w", "a_log", "dt_bias", "o_norm_g", "w_out",
              "norm2_g", "router_w", "moe_w1", "moe_w2")


def kernel(x, c, ctx, c_ctx, l0_mod_w, l0_mod_b, l0_norm1_g, l0_w_in, l0_conv_w, l0_q_norm_g, l0_w_uq, l0_kv_norm_g, l0_w_ukv, l0_w_out, l0_norm2_g, l0_ffn_w1, l0_ffn_w2, l1_mod_w, l1_mod_b, l1_norm1_g, l1_w_in, l1_qkv_conv_w, l1_a_log, l1_dt_bias, l1_o_norm_g, l1_w_out, l1_norm2_g, l1_router_w, l1_moe_w1, l1_moe_w2, l2_mod_w, l2_mod_b, l2_norm1_g, l2_w_in, l2_conv_w, l2_q_norm_g, l2_w_uq, l2_kv_norm_g, l2_w_ukv, l2_w_out, l2_norm2_g, l2_ffn_w1, l2_ffn_w2, l3_mod_w, l3_mod_b, l3_norm1_g, l3_w_in, l3_qkv_conv_w, l3_a_log, l3_dt_bias, l3_o_norm_g, l3_w_out, l3_norm2_g, l3_router_w, l3_moe_w1, l3_moe_w2, final_norm_g):
    batch, seq_len, d = x.shape
    ctx_len = ctx.shape[1]
    assert seq_len % ROW_TILE == 0 and ctx_len % ROW_TILE == 0 and seq_len % GRID_W == 0
    dims = (batch, ctx_len, seq_len)
    layers = (
        dict(zip(_EVEN_NAMES, (l0_mod_w, l0_mod_b, l0_norm1_g, l0_w_in, l0_conv_w, l0_q_norm_g, l0_w_uq,
                               l0_kv_norm_g, l0_w_ukv, l0_w_out, l0_norm2_g, l0_ffn_w1, l0_ffn_w2))),
        dict(zip(_ODD_NAMES, (l1_mod_w, l1_mod_b, l1_norm1_g, l1_w_in, l1_qkv_conv_w, l1_a_log, l1_dt_bias,
                              l1_o_norm_g, l1_w_out, l1_norm2_g, l1_router_w, l1_moe_w1, l1_moe_w2))),
        dict(zip(_EVEN_NAMES, (l2_mod_w, l2_mod_b, l2_norm1_g, l2_w_in, l2_conv_w, l2_q_norm_g, l2_w_uq,
                               l2_kv_norm_g, l2_w_ukv, l2_w_out, l2_norm2_g, l2_ffn_w1, l2_ffn_w2))),
        dict(zip(_ODD_NAMES, (l3_mod_w, l3_mod_b, l3_norm1_g, l3_w_in, l3_qkv_conv_w, l3_a_log, l3_dt_bias,
                              l3_o_norm_g, l3_w_out, l3_norm2_g, l3_router_w, l3_moe_w1, l3_moe_w2))),
    )
    xa = jnp.concatenate([ctx.reshape(batch * ctx_len, d), x.reshape(batch * seq_len, d)], axis=0)
    mod_rows = 8 * ((batch + 1 + 7) // 8)
    cc = jnp.zeros((mod_rows, d), F32).at[:batch].set(c).at[batch].set(c_ctx)
    tabs = _rope_tables(seq_len, ROW_TILE)
    n_layers = len(layers)
    for li, p in enumerate(layers):
        mods = _mods(cc, p["mod_w"], p["mod_b"])
        if li % 2 == 0:
            xa = _even_layer(xa, mods, dims, tabs, p)
        else:
            xa = _odd_layer(xa, mods, dims, p, ctx_out=li < n_layers - 1)
    out = _final_norm(xa, final_norm_g.reshape(1, -1), 0, batch * seq_len)
    return out.reshape(batch, seq_len, d)
```

```python
import functools
import math

import jax
import jax.numpy as jnp
from jax import lax
from jax.experimental import pallas as pl
from jax.experimental.pallas import tpu as pltpu

F32 = jnp.float32
BF16 = jnp.bfloat16

NORM_EPS = 1e-6
N_MOD = 6
GRID_W = 64
ROPE_BASE = 10000.0

CONV_CH = 512
MLA_HEADS = 8
Q_RANK = 256
KV_RANK = 128
QK_NOPE = 64
QK_ROPE = 32
V_HEAD = 64
MLA_SCALE = (QK_NOPE + QK_ROPE) ** -0.5

GDN_HEADS = 8
GDN_DK = 128
GDN_CHUNK = 64
GDN_QK = GDN_HEADS * GDN_DK

N_EXPERTS = 8

LANES = 128
ROW_TILE = 256
GDN_STEP = 256
GDN_HEAD_GROUP = 8
ROW_DMA_UNROLL = 8
EXPERT_TILE = 512
EXPERT_FCHUNK = 512
VMEM_LIMIT_BYTES = 56 * 1024 * 1024
EXPERT_VMEM_LIMIT_BYTES = 62 * 1024 * 1024


def _cparams(*sem):
    return pltpu.CompilerParams(dimension_semantics=sem, vmem_limit_bytes=VMEM_LIMIT_BYTES)


def _silu(x):
    return x * (1.0 / (1.0 + jnp.exp(-x)))


def _rms(x, g):
    return x * lax.rsqrt(jnp.mean(x * x, axis=-1, keepdims=True) + NORM_EPS) * g


def _dot(a, b):
    return jnp.dot(a, b, preferred_element_type=F32)


def _dot_nt(a, b):
    return lax.dot_general(a, b, (((1,), (1,)), ((), ())), preferred_element_type=F32)


def _dot_tn(a, b):
    return lax.dot_general(a, b, (((0,), (0,)), ((), ())), preferred_element_type=F32)


def _split3(x):
    x1 = x.astype(BF16)
    r1 = x - x1.astype(F32)
    x2 = r1.astype(BF16)
    x3 = (r1 - x2.astype(F32)).astype(BF16)
    return x1, x2, x3


def _dot_f32(a, b):
    a1, a2, a3 = _split3(a)
    b1, b2, b3 = _split3(b)
    acc = _dot(a1, b3) + _dot(a2, b2) + _dot(a3, b1)
    acc = acc + _dot(a1, b2) + _dot(a2, b1)
    return acc + _dot(a1, b1)


def _dot_f32_exact_rhs(a, m_bf16):
    a1, a2, a3 = _split3(a)
    return _dot(a3, m_bf16) + _dot(a2, m_bf16) + _dot(a1, m_bf16)


def _mod_row_fn(n_ctx_tiles, tiles_per_seq, batch):
    def row(i):
        return jnp.where(i < n_ctx_tiles, batch, (i - n_ctx_tiles) // tiles_per_seq)
    return row


def _mods_kernel(c_ref, w_ref, b_ref, o_ref):
    o_ref[...] = _dot_f32(_silu(c_ref[...]), w_ref[...]) + b_ref[...]


def _mods(cc, mod_w, mod_b):
    rows, d = cc.shape
    n = mod_w.shape[1]
    out = pl.pallas_call(
        _mods_kernel,
        grid=(n // d,),
        in_specs=[pl.BlockSpec((rows, d), lambda j: (0, 0)),
                  pl.BlockSpec((d, d), lambda j: (0, j)),
                  pl.BlockSpec((1, d), lambda j: (0, j))],
        out_specs=pl.BlockSpec((rows, d), lambda j: (0, j)),
        out_shape=jax.ShapeDtypeStruct((rows, n), F32),
        compiler_params=_cparams("arbitrary"),
        name="adaln_mods",
    )(cc, mod_w, mod_b.reshape(1, n))
    return out.reshape(rows, 1, n)


def _mod_spec(d, k, row, off=0):
    return pl.BlockSpec((1, 1, d), lambda i: (row(i + off), 0, k))


def _even_proj_kernel(x_ref, g_ref, sh_ref, sc_ref, win_ref, qg_ref, wq_ref, kvg_ref, wkv_ref, cos_ref, sin_ref,
                      gbu_ref, q_ref, k_ref, v_ref):
    hn = _rms(x_ref[...], g_ref[...]) * (1.0 + sc_ref[0]) + sh_ref[0]
    hb = hn.astype(BF16)
    n_gbu = 3 * CONV_CH
    for c0 in range(0, n_gbu, CONV_CH):
        gbu_ref[:, c0:c0 + CONV_CH] = _dot(hb, win_ref[:, c0:c0 + CONV_CH]).astype(gbu_ref.dtype)
    rest = _dot(hb, win_ref[:, n_gbu:])
    cq = rest[:, :Q_RANK]
    ckv = rest[:, Q_RANK:Q_RANK + KV_RANK]
    kr = rest[:, Q_RANK + KV_RANK:Q_RANK + KV_RANK + LANES]
    kr_sw = rest[:, Q_RANK + KV_RANK + LANES:]
    cos = cos_ref[...]
    sin = sin_ref[...]
    kr_rot = kr * cos + kr_sw * sin
    qq = _dot(_rms(cq, qg_ref[...]).astype(BF16), wq_ref[...])
    kv = _dot(_rms(ckv, kvg_ref[...]).astype(BF16), wkv_ref[...])
    hw = MLA_HEADS * LANES
    for h in range(MLA_HEADS):
        sl = slice(h * LANES, (h + 1) * LANES)
        qh = (qq[:, sl] * cos + qq[:, hw + h * LANES:hw + (h + 1) * LANES] * sin) * MLA_SCALE
        q_ref[:, sl] = qh.astype(q_ref.dtype)
        k_ref[:, sl] = (kv[:, sl] + kr_rot).astype(k_ref.dtype)
    v_ref[...] = kv[:, hw:].astype(v_ref.dtype)


def _even_proj(x2, g, mods, row, n_tiles, w_in, qg, wq, kvg, wkv, cos_t, sin_t, tab_idx):
    t, d = x2.shape
    tm = ROW_TILE
    n_in = w_in.shape[1]
    hw = MLA_HEADS * LANES
    const = lambda i: (0, 0)
    return pl.pallas_call(
        _even_proj_kernel,
        grid=(n_tiles,),
        in_specs=[pl.BlockSpec((tm, d), lambda i: (i, 0)),
                  pl.BlockSpec((1, d), const),
                  _mod_spec(d, 0, row), _mod_spec(d, 1, row),
                  pl.BlockSpec((d, n_in), const),
                  pl.BlockSpec((1, Q_RANK), const),
                  pl.BlockSpec(wq.shape, const),
                  pl.BlockSpec((1, KV_RANK), const),
                  pl.BlockSpec(wkv.shape, const),
                  pl.BlockSpec((tm, LANES), lambda i: (tab_idx(i), 0)),
                  pl.BlockSpec((tm, LANES), lambda i: (tab_idx(i), 0))],
        out_specs=[pl.BlockSpec((tm, 3 * CONV_CH), lambda i: (i, 0)),
                   pl.BlockSpec((tm, hw), lambda i: (i, 0)),
                   pl.BlockSpec((tm, hw), lambda i: (i, 0)),
                   pl.BlockSpec((tm, MLA_HEADS * V_HEAD), lambda i: (i, 0))],
        out_shape=[jax.ShapeDtypeStruct((t, 3 * CONV_CH), BF16),
                   jax.ShapeDtypeStruct((t, hw), BF16),
                   jax.ShapeDtypeStruct((t, hw), BF16),
                   jax.ShapeDtypeStruct((t, MLA_HEADS * V_HEAD), BF16)],
        compiler_params=_cparams("arbitrary"),
        name="even_proj",
    )(x2, g, mods, mods, w_in, qg, wq, kvg, wkv, cos_t, sin_t)


def _shifted(z, prev_row, next_row):
    tm = z.shape[0]
    ridx = lax.broadcasted_iota(jnp.int32, z.shape, 0)
    zm = jnp.where(ridx == 0, prev_row, pltpu.roll(z, 1, 0))
    zp = jnp.where(ridx == tm - 1, next_row, pltpu.roll(z, tm - 1, 0))
    return zm, zp


def _seq_edges(i, n_ctx_tiles, ctx_tps, x_tps):
    j = jnp.where(i < n_ctx_tiles, i % ctx_tps, (i - n_ctx_tiles) % x_tps)
    tps = jnp.where(i < n_ctx_tiles, ctx_tps, x_tps)
    keep_prev = (j != 0).astype(F32)
    keep_next = (j != tps - 1).astype(F32)
    return keep_prev, keep_next


def _even_conv_kernel(cur_ref, prev_ref, next_ref, w_ref, o_ref, *, n_ctx_tiles, ctx_tps, x_tps):
    i = pl.program_id(0)
    keep_prev, keep_next = _seq_edges(i, n_ctx_tiles, ctx_tps, x_tps)
    c = CONV_CH
    cur = cur_ref[...].astype(F32)
    z = cur[:, c:2 * c] * cur[:, 2 * c:]
    pr = prev_ref[...].astype(F32)
    nx = next_ref[...].astype(F32)
    zprev = (pr[HALO_ROWS - 1:, c:2 * c] * pr[HALO_ROWS - 1:, 2 * c:]) * keep_prev
    znext = (nx[0:1, c:2 * c] * nx[0:1, 2 * c:]) * keep_next
    zm, zp = _shifted(z, zprev, znext)
    w = w_ref[...]
    y = zm * w[0:1] + z * w[1:2] + zp * w[2:3]
    o_ref[...] = (cur[:, :c] * y).astype(o_ref.dtype)


HALO_ROWS = 16


def _halo_specs(tm, width, n_rows, colmap):
    nb = n_rows // HALO_ROWS
    r = tm // HALO_ROWS
    prev = pl.BlockSpec((HALO_ROWS, width), lambda i, *a: (jnp.maximum(i * r - 1, 0), colmap(*a)))
    nxt = pl.BlockSpec((HALO_ROWS, width), lambda i, *a: (jnp.minimum((i + 1) * r, nb - 1), colmap(*a)))
    return prev, nxt


def _even_conv(gbu, conv_w, n_ctx_tiles, ctx_tps, x_tps):
    t, n = gbu.shape
    tm = ROW_TILE
    prev, nxt = _halo_specs(tm, n, t, lambda: 0)
    return pl.pallas_call(
        functools.partial(_even_conv_kernel, n_ctx_tiles=n_ctx_tiles, ctx_tps=ctx_tps, x_tps=x_tps),
        grid=(t // tm,),
        in_specs=[pl.BlockSpec((tm, n), lambda i: (i, 0)), prev, nxt,
                  pl.BlockSpec(conv_w.shape, lambda i: (0, 0))],
        out_specs=pl.BlockSpec((tm, CONV_CH), lambda i: (i, 0)),
        out_shape=jax.ShapeDtypeStruct((t, CONV_CH), BF16),
        compiler_params=_cparams("arbitrary"),
        name="even_conv",
    )(gbu, gbu, gbu, conv_w)


def _attn_kernel(q_ref, kc_ref, kx_ref, vc_ref, vx_ref, o_ref, *, n_ctx_q):
    qi = pl.program_id(1)

    def run(use_x):
        for hp in range(MLA_HEADS // 2):
            outs = []
            for h in (2 * hp, 2 * hp + 1):
                ks = slice(h * LANES, (h + 1) * LANES)
                vs = slice(h * V_HEAD, (h + 1) * V_HEAD)
                q = q_ref[:, ks]
                sc = _dot_nt(q, kc_ref[:, ks])
                m = jnp.max(sc, axis=1, keepdims=True)
                if use_x:
                    sx = _dot_nt(q, kx_ref[:, ks])
                    m = jnp.maximum(m, jnp.max(sx, axis=1, keepdims=True))
                pc = jnp.exp(sc - m)
                l = jnp.sum(pc, axis=1, keepdims=True)
                o = _dot(pc.astype(BF16), vc_ref[:, vs])
                if use_x:
                    px = jnp.exp(sx - m)
                    l = l + jnp.sum(px, axis=1, keepdims=True)
                    o = o + _dot(px.astype(BF16), vx_ref[:, vs])
                outs.append(o * (1.0 / l))
            o_ref[:, hp * LANES:(hp + 1) * LANES] = jnp.concatenate(outs, axis=1).astype(o_ref.dtype)

    @pl.when(qi < n_ctx_q)
    def _():
        run(False)

    @pl.when(qi >= n_ctx_q)
    def _():
        run(True)


def _attention(q, k, v, batch, ctx_len, seq_len):
    tq = ROW_TILE
    n_ctx_q = ctx_len // tq
    n_x_q = seq_len // tq
    ctx_blocks = batch * ctx_len // tq
    assert (batch * ctx_len) % seq_len == 0
    x_blk0 = batch * ctx_len // seq_len
    hw = MLA_HEADS * LANES
    vw = MLA_HEADS * V_HEAD

    def qrow(b, qi):
        return jnp.where(qi < n_ctx_q, b * n_ctx_q + qi, ctx_blocks + b * n_x_q + (qi - n_ctx_q))

    return pl.pallas_call(
        functools.partial(_attn_kernel, n_ctx_q=n_ctx_q),
        grid=(batch, n_ctx_q + n_x_q),
        in_specs=[pl.BlockSpec((tq, hw), lambda b, qi: (qrow(b, qi), 0)),
                  pl.BlockSpec((ctx_len, hw), lambda b, qi: (b, 0)),
                  pl.BlockSpec((seq_len, hw), lambda b, qi: (x_blk0 + b, 0)),
                  pl.BlockSpec((ctx_len, vw), lambda b, qi: (b, 0)),
                  pl.BlockSpec((seq_len, vw), lambda b, qi: (x_blk0 + b, 0))],
        out_specs=pl.BlockSpec((tq, vw), lambda b, qi: (qrow(b, qi), 0)),
        out_shape=jax.ShapeDtypeStruct((q.shape[0], vw), BF16),
        compiler_params=_cparams("arbitrary", "arbitrary"),
        name="mla_attention",
    )(q, k, k, v, v)


def _resid_mm_kernel(*refs, n_lhs):
    x_ref, gate_ref = refs[0], refs[1]
    lhs = refs[2:2 + n_lhs]
    ws = refs[2 + n_lhs:2 + 2 * n_lhs]
    o_ref = refs[2 + 2 * n_lhs]
    acc = _dot(lhs[0][...], ws[0][...])
    for a_ref, w_ref in zip(lhs[1:], ws[1:]):
        acc = acc + _dot(a_ref[...], w_ref[...])
    o_ref[...] = x_ref[...] + gate_ref[0] * acc


def _resid_mm(x2, mods, k_gate, row, lhs_list, w_list, tm=512):
    t, d = x2.shape
    n = len(lhs_list)
    in_specs = [pl.BlockSpec((tm, d), lambda i: (i, 0)), _mod_spec(d, k_gate, lambda i: row(i * tm // ROW_TILE))]
    in_specs += [pl.BlockSpec((tm, a.shape[1]), lambda i: (i, 0)) for a in lhs_list]
    in_specs += [pl.BlockSpec(w.shape, lambda i: (0, 0)) for w in w_list]
    return pl.pallas_call(
        functools.partial(_resid_mm_kernel, n_lhs=n),
        grid=(t // tm,),
        in_specs=in_specs,
        out_specs=pl.BlockSpec((tm, d), lambda i: (i, 0)),
        out_shape=jax.ShapeDtypeStruct((t, d), F32),
        compiler_params=_cparams("arbitrary"),
        name="resid_matmul",
    )(x2, mods, *lhs_list, *w_list)


def _ffn_up_kernel(x_ref, g_ref, sh_ref, sc_ref, w_ref, o_ref, *, ffn, chunk):
    hn = _rms(x_ref[...], g_ref[...]) * (1.0 + sc_ref[0]) + sh_ref[0]
    hb = hn.astype(BF16)
    for c0 in range(0, ffn, chunk):
        gate = _dot(hb, w_ref[:, c0:c0 + chunk])
        up = _dot(hb, w_ref[:, ffn + c0:ffn + c0 + chunk])
        o_ref[:, c0:c0 + chunk] = (_silu(gate) * up).astype(o_ref.dtype)


def _ffn_up(x2, g, mods, row, w1):
    t, d = x2.shape
    tm = ROW_TILE
    ffn = w1.shape[1] // 2
    chunk = 256
    assert ffn % chunk == 0
    return pl.pallas_call(
        functools.partial(_ffn_up_kernel, ffn=ffn, chunk=chunk),
        grid=(t // tm,),
        in_specs=[pl.BlockSpec((tm, d), lambda i: (i, 0)),
                  pl.BlockSpec((1, d), lambda i: (0, 0)),
                  _mod_spec(d, 3, row), _mod_spec(d, 4, row),
                  pl.BlockSpec(w1.shape, lambda i: (0, 0))],
        out_specs=pl.BlockSpec((tm, ffn), lambda i: (i, 0)),
        out_shape=jax.ShapeDtypeStruct((t, ffn), BF16),
        compiler_params=_cparams("arbitrary"),
        name="ffn_up",
    )(x2, g, mods, mods, w1)


X_HALO = 8


def _odd_proj_kernel(x_ref, xp_ref, xn_ref, g_ref, sh_ref, sc_ref, w_ref, cw_ref, qkv_ref, gate_ref, abt_ref,
                     *, chunk, n_ctx_tiles, ctx_tps, x_tps):
    i = pl.program_id(0)
    keep_prev, keep_next = _seq_edges(i, n_ctx_tiles, ctx_tps, x_tps)
    tm = x_ref.shape[0]
    xs = jnp.concatenate([x_ref[...], xp_ref[...], xn_ref[...]], axis=0)
    hb = (_rms(xs, g_ref[...]) * (1.0 + sc_ref[0]) + sh_ref[0]).astype(BF16)
    n_qkv = qkv_ref.shape[0] * LANES
    n_gate = gate_ref.shape[1]
    cw = cw_ref[...]
    for c0 in range(0, n_qkv, chunk):
        z_all = _dot(hb, w_ref[:, c0:c0 + chunk])
        z = z_all[:tm]
        zprev = z_all[tm + X_HALO - 1:tm + X_HALO] * keep_prev
        znext = z_all[tm + X_HALO:tm + X_HALO + 1] * keep_next
        zm, zp = _shifted(z, zprev, znext)
        y = _silu(zm * cw[0:1, c0:c0 + chunk] + z * cw[1:2, c0:c0 + chunk] + zp * cw[2:3, c0:c0 + chunk])
        for hh in range(chunk // LANES):
            head = c0 // LANES + hh
            yh = y[:, hh * LANES:(hh + 1) * LANES]
            if head < 2 * GDN_HEADS:
                nrm = lax.rsqrt(jnp.sum(yh * yh, axis=-1, keepdims=True) + NORM_EPS)
                if head < GDN_HEADS:
                    nrm = nrm * (GDN_DK ** -0.5)
                yh = yh * nrm
            qkv_ref[head] = yh
    hb_cur = hb[:tm]
    for c0 in range(0, n_gate, chunk):
        gate_ref[:, c0:c0 + chunk] = _dot(hb_cur, w_ref[:, n_qkv + c0:n_qkv + c0 + chunk]).astype(gate_ref.dtype)
    abt_ref[...] = _dot(hb_cur, w_ref[:, n_qkv + n_gate:])


def _odd_proj(x2, g, mods, row, w_in, conv_w, n_ctx_tiles, ctx_tps, x_tps):
    t, d = x2.shape
    tm = ROW_TILE
    n_qkv = 3 * GDN_QK
    n_gate = GDN_QK
    r = tm // X_HALO
    nb = t // X_HALO
    return pl.pallas_call(
        functools.partial(_odd_proj_kernel, chunk=512, n_ctx_tiles=n_ctx_tiles, ctx_tps=ctx_tps, x_tps=x_tps),
        grid=(t // tm,),
        in_specs=[pl.BlockSpec((tm, d), lambda i: (i, 0)),
                  pl.BlockSpec((X_HALO, d), lambda i: (jnp.maximum(i * r - 1, 0), 0)),
                  pl.BlockSpec((X_HALO, d), lambda i: (jnp.minimum((i + 1) * r, nb - 1), 0)),
                  pl.BlockSpec((1, d), lambda i: (0, 0)),
                  _mod_spec(d, 0, row), _mod_spec(d, 1, row),
                  pl.BlockSpec(w_in.shape, lambda i: (0, 0)),
                  pl.BlockSpec(conv_w.shape, lambda i: (0, 0))],
        out_specs=[pl.BlockSpec((n_qkv // LANES, tm, LANES), lambda i: (0, i, 0)),
                   pl.BlockSpec((tm, n_gate), lambda i: (i, 0)),
                   pl.BlockSpec((tm, LANES), lambda i: (i, 0))],
        out_shape=[jax.ShapeDtypeStruct((n_qkv // LANES, t, LANES), F32),
                   jax.ShapeDtypeStruct((t, n_gate), BF16),
                   jax.ShapeDtypeStruct((t, LANES), F32)],
        compiler_params=_cparams("arbitrary"),
        name="odd_proj",
    )(x2, x2, x2, g, mods, mods, w_in, conv_w)


def _gdn_kernel(qkv_ref, abt_ref, alog_ref, dtb_ref, o_ref,
                s_ref, gc_s, beta_s, gtot_s, uw_s, qg_s, kd_s, attn_s, vnew_s, gtc_s):
    d = pl.program_id(1)
    step = pl.program_id(2)
    n = GDN_STEP
    c = GDN_CHUNK
    nchunk = n // c
    h_n = GDN_HEADS
    sgn = 1 - 2 * d

    @pl.when(step == 0)
    def _():
        s_ref[...] = jnp.zeros(s_ref.shape, s_ref.dtype)

    ri = lax.broadcasted_iota(jnp.int32, (n, n), 0)
    ci = lax.broadcasted_iota(jnp.int32, (n, n), 1)
    same = (ri // c) == (ci // c)
    rel = (ri - ci) * sgn
    incl = same & (rel >= 0)
    strict = same & (rel > 0)
    eye = (ri == ci).astype(F32)

    hrow = pl.ds(d * h_n, h_n)
    a_all = abt_ref[0, hrow, :]
    bt_all = abt_ref[0, pl.ds(2 * h_n + d * h_n, h_n), :]
    sp_in = a_all + dtb_ref[hrow, :]
    softplus = jnp.maximum(sp_in, 0.0) + jnp.log(1.0 + jnp.exp(-jnp.abs(sp_in)))
    g_all = -jnp.exp(alog_ref[hrow, :]) * softplus
    beta_s[...] = 1.0 / (1.0 + jnp.exp(-bt_all))
    cum_m = jnp.where(same & ((ci - ri) * sgn >= 0), 1.0, 0.0).astype(BF16)
    gc_s[...] = _dot_f32_exact_rhs(g_all, cum_m)
    gtot_s[...] = _dot_f32_exact_rhs(g_all, same.astype(BF16))

    grp = uw_s.shape[0]
    n_sq = (c - 1).bit_length() - 1

    def head_group(gi, carry):
        heads = [gi * grp + u for u in range(grp)]
        us = range(grp)
        q = [qkv_ref[h] for h in heads]
        k = [qkv_ref[h_n + h] for h in heads]
        v = [qkv_ref[2 * h_n + h] for h in heads]
        gc_row = [gc_s[pl.ds(h, 1), :] for h in heads]
        gc_col = [jnp.sum(eye * gc_row[u], axis=1, keepdims=True) for u in us]
        beta_col = [jnp.sum(eye * beta_s[pl.ds(h, 1), :], axis=1, keepdims=True) for h in heads]
        gtot_col = [jnp.sum(eye * gtot_s[pl.ds(h, 1), :], axis=1, keepdims=True) for h in heads]
        decay = [jnp.exp(jnp.where(incl, gc_col[u] - gc_row[u], -1e30)) for u in us]
        kb = [k[u] * beta_col[u] for u in us]
        k16 = [k[u].astype(BF16) for u in us]
        p = [jnp.where(strict, -(_dot_nt(kb[u].astype(BF16), k16[u]) * decay[u]), 0.0) for u in us]
        for u in us:
            attn_s[u] = (_dot_nt(q[u].astype(BF16), k16[u]) * decay[u]).astype(attn_s.dtype)
            gtc_s[u] = jnp.broadcast_to(gtot_col[u], gtc_s.shape[1:])
        tinv = [eye + p[u] for u in us]
        p16 = [p[u].astype(BF16) for u in us]
        p = [_dot(p16[u], p16[u]) for u in us]
        for _ in range(n_sq - 1):
            p16 = [p[u].astype(BF16) for u in us]
            tp = [_dot(jnp.concatenate([tinv[u].astype(BF16), p16[u]], axis=0), p16[u]) for u in us]
            tinv = [tinv[u] + tp[u][:n] for u in us]
            p = [tp[u][n:] for u in us]
        tinv = [tinv[u] + _dot(tinv[u].astype(BF16), p[u].astype(BF16)) for u in us]
        egc = [jnp.exp(gc_col[u]) for u in us]
        for u in us:
            rhs = jnp.concatenate([v[u] * beta_col[u], kb[u] * egc[u]], axis=1).astype(BF16)
            uw_s[u] = _dot(tinv[u].astype(BF16), rhs)
            qg_s[u] = q[u] * egc[u]
            kd_s[u] = k[u] * jnp.exp(gtot_col[u] - gc_col[u])
            vnew_s[u] = jnp.zeros(vnew_s.shape[1:], vnew_s.dtype)
        s = [s_ref[h] for h in heads]
        for cstep in range(nchunk):
            cc = jnp.where(d == 0, cstep, nchunk - 1 - cstep)
            r0 = pl.multiple_of(cc * c, c)
            rows = pl.ds(r0, c)
            ws_qs = [_dot(jnp.concatenate([uw_s[u, rows, LANES:], qg_s[u, rows, :]], axis=0).astype(BF16),
                          s[u].astype(BF16)) for u in us]
            v_new = [uw_s[u, rows, :LANES] - ws_qs[u][:c] for u in us]
            for u in us:
                vnew_s[u, rows, :] = v_new[u].astype(vnew_s.dtype)
            for u in us:
                o_ref[0, heads[u], rows, :] = ws_qs[u][c:] + _dot(attn_s[u, rows, :], vnew_s[u])
            s = [s[u] * jnp.exp(gtc_s[u, pl.ds(r0, 1), :])
                 + _dot_tn(kd_s[u, rows, :].astype(BF16), v_new[u].astype(BF16)) for u in us]
        for u in us:
            s_ref[heads[u]] = s[u]
        return carry

    lax.fori_loop(0, h_n // grp, head_group, 0)


def _gdn(qkvn, abt_chunks, alog_b, dtb_b, batch, ctx_len, seq_len):
    nh3, t, _ = qkvn.shape
    n = GDN_STEP
    nc = ctx_len // n
    nx = seq_len // n
    ctx_blocks = batch * nc

    def rowblk(b, d, s):
        sc = jnp.where(d == 0, s, nc - 1 - s)
        sx = jnp.where(d == 0, s - nc, nx - 1 - (s - nc))
        return jnp.where(s < nc, b * nc + sc, ctx_blocks + b * nx + sx)

    vm = pltpu.VMEM
    g = GDN_HEAD_GROUP
    return pl.pallas_call(
        _gdn_kernel,
        grid=(batch, 2, nc + nx),
        in_specs=[pl.BlockSpec((nh3, n, LANES), lambda b, d, s: (0, rowblk(b, d, s), 0)),
                  pl.BlockSpec((1, 4 * GDN_HEADS, n), lambda b, d, s: (rowblk(b, d, s), 0, 0)),
                  pl.BlockSpec(alog_b.shape, lambda b, d, s: (0, 0)),
                  pl.BlockSpec(dtb_b.shape, lambda b, d, s: (0, 0))],
        out_specs=pl.BlockSpec((1, GDN_HEADS, n, LANES), lambda b, d, s: (d, 0, rowblk(b, d, s), 0)),
        out_shape=jax.ShapeDtypeStruct((2, GDN_HEADS, t, LANES), F32),
        scratch_shapes=[vm((GDN_HEADS, GDN_DK, LANES), F32),
                        vm((GDN_HEADS, n), F32),
                        vm((GDN_HEADS, n), F32),
                        vm((GDN_HEADS, n), F32),
                        vm((g, n, 2 * LANES), F32),
                        vm((g, n, LANES), F32),
                        vm((g, n, LANES), F32),
                        vm((g, n, n), BF16),
                        vm((g, n, LANES), BF16),
                        vm((g, n, LANES), F32)],
        compiler_params=_cparams("arbitrary", "arbitrary", "arbitrary"),
        name="gated_delta",
    )(qkvn, abt_chunks, alog_b, dtb_b)


def _odd_out_kernel(x_ref, gate_ref, o_ref_in, z_ref, og_ref, w_ref, out_ref):
    parts = []
    og = og_ref[...]
    for h in range(GDN_HEADS):
        o = o_ref_in[0, h] + o_ref_in[1, h]
        y = _rms(o, og) * _silu(z_ref[:, h * LANES:(h + 1) * LANES].astype(F32))
        parts.append(y.astype(BF16))
    y_all = jnp.concatenate(parts, axis=1)
    out_ref[...] = x_ref[...] + gate_ref[0] * _dot(y_all, w_ref[...])


def _odd_out(x2, mods, row, o_dirs, z, o_norm_g, w_out, row_off, n_rows):
    d = x2.shape[1]
    tm = ROW_TILE
    return pl.pallas_call(
        _odd_out_kernel,
        grid=(n_rows // tm,),
        in_specs=[pl.BlockSpec((tm, d), lambda i: (i + row_off, 0)),
                  _mod_spec(d, 2, row, row_off),
                  pl.BlockSpec((2, GDN_HEADS, tm, LANES), lambda i: (0, 0, i + row_off, 0)),
                  pl.BlockSpec((tm, z.shape[1]), lambda i: (i + row_off, 0)),
                  pl.BlockSpec((1, LANES), lambda i: (0, 0)),
                  pl.BlockSpec(w_out.shape, lambda i: (0, 0))],
        out_specs=pl.BlockSpec((tm, d), lambda i: (i, 0)),
        out_shape=jax.ShapeDtypeStruct((n_rows, d), F32),
        compiler_params=_cparams("arbitrary"),
        name="odd_out",
    )(x2, mods, o_dirs, z, o_norm_g, w_out)


def _router_kernel(x_ref, g_ref, sh_ref, sc_ref, rw_ref, h_ref, info_ref, cnt_ref, carry_s):
    i = pl.program_id(0)

    @pl.when(i == 0)
    def _():
        carry_s[...] = jnp.zeros(carry_s.shape, carry_s.dtype)

    hn = _rms(x_ref[...], g_ref[...]) * (1.0 + sc_ref[0]) + sh_ref[0]
    h_ref[...] = hn
    tm = hn.shape[0]
    lane = lax.broadcasted_iota(jnp.int32, (tm, LANES), 1)
    logits = jnp.where(lane < N_EXPERTS, _dot_f32(hn, rw_ref[...]), -jnp.inf)
    m1 = jnp.max(logits, axis=1, keepdims=True)
    i1 = jnp.min(jnp.where(logits == m1, lane, LANES), axis=1, keepdims=True)
    rest = jnp.where(lane == i1, -jnp.inf, logits)
    m2 = jnp.max(rest, axis=1, keepdims=True)
    i2 = jnp.min(jnp.where(rest == m2, lane, LANES), axis=1, keepdims=True)
    e2 = jnp.exp(m2 - m1)
    w1 = 1.0 / (1.0 + e2)
    w2 = e2 / (1.0 + e2)
    oh1 = (lane == i1).astype(F32)
    oh2 = (lane == i2).astype(F32)
    cnt = oh1 + oh2
    ri = lax.broadcasted_iota(jnp.int32, (tm, tm), 0)
    ci = lax.broadcasted_iota(jnp.int32, (tm, tm), 1)
    before = (ci < ri).astype(BF16)
    pre = _dot(before, cnt.astype(BF16)) + carry_s[0:1, :]
    r1 = jnp.sum(pre * oh1, axis=1, keepdims=True)
    r2 = jnp.sum(pre * oh2, axis=1, keepdims=True)
    info = jnp.where(lane == 0, i1.astype(F32), 0.0)
    info = jnp.where(lane == 1, i2.astype(F32), info)
    info = jnp.where(lane == 2, w1, info)
    info = jnp.where(lane == 3, w2, info)
    info = jnp.where(lane == 4, r1, info)
    info = jnp.where(lane == 5, r2, info)
    info_ref[...] = info
    total = carry_s[0:1, :] + jnp.sum(cnt, axis=0, keepdims=True)
    carry_s[...] = jnp.broadcast_to(total, carry_s.shape)
    cnt_ref[...] = jnp.broadcast_to(total, cnt_ref.shape)


def _router(x2, g, mods, row, router_w_pad):
    t, d = x2.shape
    tm = ROW_TILE
    return pl.pallas_call(
        _router_kernel,
        grid=(t // tm,),
        in_specs=[pl.BlockSpec((tm, d), lambda i: (i, 0)),
                  pl.BlockSpec((1, d), lambda i: (0, 0)),
                  _mod_spec(d, 3, row), _mod_spec(d, 4, row),
                  pl.BlockSpec(router_w_pad.shape, lambda i: (0, 0))],
        out_specs=[pl.BlockSpec((tm, d), lambda i: (i, 0)),
                   pl.BlockSpec((tm, LANES), lambda i: (i, 0)),
                   pl.BlockSpec((8, LANES), lambda i: (0, 0))],
        out_shape=[jax.ShapeDtypeStruct((t, d), F32),
                   jax.ShapeDtypeStruct((t, LANES), F32),
                   jax.ShapeDtypeStruct((8, LANES), F32)],
        scratch_shapes=[pltpu.VMEM((8, LANES), F32)],
        compiler_params=_cparams("arbitrary"),
        name="moe_router",
    )(x2, g, mods, mods, router_w_pad)


def _row_copy(src_hbm, src_row, dst_ref, dst_row, sem):
    return pltpu.make_async_copy(src_hbm.at[pl.ds(src_row, 1)], dst_ref.at[pl.ds(dst_row, 1)], sem)


def _dispatch_kernel(dest_ref, h_ref, init_hbm, hs_hbm, sem, *, tm):
    del init_hbm

    def issue(r, carry):
        _row_copy(h_ref, r, hs_hbm, dest_ref[0, 0, r], sem.at[0]).start()
        _row_copy(h_ref, r, hs_hbm, dest_ref[0, 0, tm + r], sem.at[1]).start()
        return carry

    lax.fori_loop(0, tm, issue, 0, unroll=ROW_DMA_UNROLL)
    for k in range(2):
        pltpu.make_async_copy(h_ref, hs_hbm.at[pl.ds(0, tm)], sem.at[k]).wait()


def _dispatch(h, dest_tiles, n_sorted_rows):
    t, d = h.shape
    tm = ROW_TILE
    init = jnp.zeros((n_sorted_rows, d), h.dtype)
    return pl.pallas_call(
        functools.partial(_dispatch_kernel, tm=tm),
        grid=(t // tm,),
        in_specs=[pl.BlockSpec((1, 1, 2 * tm), lambda i: (i, 0, 0), memory_space=pltpu.SMEM),
                  pl.BlockSpec((tm, d), lambda i: (i, 0)),
                  pl.BlockSpec(memory_space=pl.ANY)],
        out_specs=pl.BlockSpec(memory_space=pl.ANY),
        out_shape=jax.ShapeDtypeStruct((n_sorted_rows, d), h.dtype),
        scratch_shapes=[pltpu.SemaphoreType.DMA((2,))],
        input_output_aliases={2: 0},
        compiler_params=_cparams("arbitrary"),
        name="moe_dispatch",
    )(dest_tiles, h, init)


def _expert_kernel(te_ref, nt_ref, hs_ref, w1_ref, w2_ref, ys_ref, *, ffn, chunk):
    del te_ref
    j = pl.program_id(0)

    @pl.when(j < nt_ref[0])
    def _():
        hb = hs_ref[...].astype(BF16)
        acc = None
        for c0 in range(0, ffn, chunk):
            gate = _dot(hb, w1_ref[:, c0:c0 + chunk])
            up = _dot(hb, w1_ref[:, ffn + c0:ffn + c0 + chunk])
            part = _dot((_silu(gate) * up).astype(BF16), w2_ref[c0:c0 + chunk, :])
            acc = part if acc is None else acc + part
        ys_ref[...] = acc

    @pl.when(j >= nt_ref[0])
    def _():
        ys_ref[...] = jnp.zeros(ys_ref.shape, ys_ref.dtype)


def _experts(hs, tile_expert, n_tiles_used, w1, w2):
    r, d = hs.shape
    tm = EXPERT_TILE
    ffn = w2.shape[1]
    assert ffn % EXPERT_FCHUNK == 0

    def tile(j, nt):
        return jnp.minimum(j, nt[0] - 1)

    grid_spec = pltpu.PrefetchScalarGridSpec(
        num_scalar_prefetch=2,
        grid=(r // tm,),
        in_specs=[pl.BlockSpec((tm, d), lambda j, te, nt: (tile(j, nt), 0)),
                  pl.BlockSpec((None, d, 2 * ffn), lambda j, te, nt: (te[tile(j, nt)], 0, 0)),
                  pl.BlockSpec((None, ffn, d), lambda j, te, nt: (te[tile(j, nt)], 0, 0))],
        out_specs=pl.BlockSpec((tm, d), lambda j, te, nt: (j, 0)),
    )
    return pl.pallas_call(
        functools.partial(_expert_kernel, ffn=ffn, chunk=EXPERT_FCHUNK),
        grid_spec=grid_spec,
        out_shape=jax.ShapeDtypeStruct((r, d), F32),
        compiler_params=pltpu.CompilerParams(dimension_semantics=("arbitrary",),
                                             vmem_limit_bytes=EXPERT_VMEM_LIMIT_BYTES),
        name="moe_experts",
    )(tile_expert, n_tiles_used, hs, w1, w2)


def _combine_kernel(dest_ref, x_ref, gate_ref, info_ref, ys_hbm, o_ref, y1_s, y2_s, sem, *, tm):
    def issue(r, carry):
        _row_copy(ys_hbm, dest_ref[0, 0, r], y1_s, r, sem.at[0]).start()
        _row_copy(ys_hbm, dest_ref[0, 0, tm + r], y2_s, r, sem.at[1]).start()
        return carry

    lax.fori_loop(0, tm, issue, 0, unroll=ROW_DMA_UNROLL)
    pltpu.make_async_copy(ys_hbm.at[pl.ds(0, tm)], y1_s, sem.at[0]).wait()
    pltpu.make_async_copy(ys_hbm.at[pl.ds(0, tm)], y2_s, sem.at[1]).wait()
    info = info_ref[...]
    y = info[:, 2:3] * y1_s[...] + info[:, 3:4] * y2_s[...]
    o_ref[...] = x_ref[...] + gate_ref[0] * y


def _combine(x2, mods, row, info, dest_tiles, ys):
    t, d = x2.shape
    tm = ROW_TILE
    return pl.pallas_call(
        functools.partial(_combine_kernel, tm=tm),
        grid=(t // tm,),
        in_specs=[pl.BlockSpec((1, 1, 2 * tm), lambda i: (i, 0, 0), memory_space=pltpu.SMEM),
                  pl.BlockSpec((tm, d), lambda i: (i, 0)),
                  _mod_spec(d, 5, row),
                  pl.BlockSpec((tm, LANES), lambda i: (i, 0)),
                  pl.BlockSpec(memory_space=pl.ANY)],
        out_specs=pl.BlockSpec((tm, d), lambda i: (i, 0)),
        out_shape=jax.ShapeDtypeStruct((t, d), F32),
        scratch_shapes=[pltpu.VMEM((tm, d), F32), pltpu.VMEM((tm, d), F32), pltpu.SemaphoreType.DMA((2,))],
        compiler_params=_cparams("arbitrary"),
        name="moe_combine",
    )(dest_tiles, x2, mods, info, ys)


def _moe(x2, g, mods, row, router_w, w1, w2):
    t, d = x2.shape
    tm = ROW_TILE
    te = EXPERT_TILE
    rw = jnp.pad(router_w, ((0, 0), (0, LANES - N_EXPERTS)))
    h, info, counts = _router(x2, g, mods, row, rw)
    cnt = counts[0, :N_EXPERTS].astype(jnp.int32)
    gsz = ((cnt + te - 1) // te) * te
    ends = jnp.cumsum(gsz)
    offs = ends - gsz
    e1 = info[:, 0].astype(jnp.int32)
    e2 = info[:, 1].astype(jnp.int32)
    dest1 = offs[e1] + info[:, 4].astype(jnp.int32)
    dest2 = offs[e2] + info[:, 5].astype(jnp.int32)
    dest_tiles = jnp.concatenate([dest1.reshape(t // tm, 1, tm), dest2.reshape(t // tm, 1, tm)], axis=2)
    n_sorted = 2 * t + N_EXPERTS * te
    n_tiles_max = n_sorted // te
    starts = jnp.arange(n_tiles_max, dtype=jnp.int32) * te
    tile_expert = jnp.minimum(jnp.sum((starts[:, None] >= ends[None, :]).astype(jnp.int32), axis=1), N_EXPERTS - 1)
    n_used = (ends[-1] // te).astype(jnp.int32).reshape(1)
    hs = _dispatch(h, dest_tiles, n_sorted)
    ys = _experts(hs, tile_expert, n_used, w1, w2)
    return _combine(x2, mods, row, info, dest_tiles, ys)


def _final_norm_kernel(x_ref, g_ref, o_ref):
    o_ref[...] = _rms(x_ref[...], g_ref[...])


def _final_norm(x2, g, row_off, n_rows):
    d = x2.shape[1]
    tm = 512
    off = row_off * ROW_TILE // tm
    return pl.pallas_call(
        _final_norm_kernel,
        grid=(n_rows // tm,),
        in_specs=[pl.BlockSpec((tm, d), lambda i: (i + off, 0)), pl.BlockSpec((1, d), lambda i: (0, 0))],
        out_specs=pl.BlockSpec((tm, d), lambda i: (i, 0)),
        out_shape=jax.ShapeDtypeStruct((n_rows, d), F32),
        compiler_params=_cparams("arbitrary"),
        name="final_norm",
    )(x2, g)


_ROPE_SWAP = tuple(list(range(8, 16)) + list(range(0, 8)) + list(range(24, 32)) + list(range(16, 24)))


def _even_weights(w_in, w_uq, w_ukv):
    d = w_in.shape[0]
    n0 = 3 * CONV_CH + Q_RANK + KV_RANK
    kr = w_in[:, n0:n0 + QK_ROPE]
    kr_sw = kr[:, jnp.array(_ROPE_SWAP)]
    z = lambda n: jnp.zeros((d, n), w_in.dtype)
    w_in_x = jnp.concatenate([w_in[:, :n0], z(QK_NOPE), kr, z(LANES - QK_NOPE - QK_ROPE),
                              z(QK_NOPE), kr_sw, z(LANES - QK_NOPE - QK_ROPE)], axis=1).astype(BF16)
    uq = w_uq.reshape(Q_RANK, MLA_HEADS, QK_NOPE + QK_ROPE)
    pad = LANES - QK_NOPE - QK_ROPE
    uq_plain = jnp.pad(uq, ((0, 0), (0, 0), (0, pad))).reshape(Q_RANK, MLA_HEADS * LANES)
    uq_rope_sw = uq[:, :, QK_NOPE:][:, :, jnp.array(_ROPE_SWAP)]
    uq_sw = jnp.pad(uq_rope_sw, ((0, 0), (0, 0), (QK_NOPE, pad))).reshape(Q_RANK, MLA_HEADS * LANES)
    wq = jnp.concatenate([uq_plain, uq_sw], axis=1).astype(BF16)
    ukv = w_ukv.reshape(KV_RANK, MLA_HEADS, QK_NOPE + V_HEAD)
    k_part = jnp.pad(ukv[:, :, :QK_NOPE], ((0, 0), (0, 0), (0, LANES - QK_NOPE))).reshape(KV_RANK, MLA_HEADS * LANES)
    v_part = ukv[:, :, QK_NOPE:].reshape(KV_RANK, MLA_HEADS * V_HEAD)
    wkv = jnp.concatenate([k_part, v_part], axis=1).astype(BF16)
    return w_in_x, wq, wkv


def _rope_tables(seq_len, ctx_rows):
    n = QK_ROPE // 4
    t = jnp.arange(seq_len)
    inv = ROPE_BASE ** (-jnp.arange(n, dtype=F32) / n)
    ang_r = (t // GRID_W).astype(F32)[:, None] * inv
    ang_c = (t % GRID_W).astype(F32)[:, None] * inv
    cr, sr, cc, sc = jnp.cos(ang_r), jnp.sin(ang_r), jnp.cos(ang_c), jnp.sin(ang_c)
    cos32 = jnp.concatenate([cr, cr, cc, cc], axis=1)
    sin32 = jnp.concatenate([-sr, sr, -sc, sc], axis=1)
    pad = LANES - QK_NOPE - QK_ROPE
    cos_x = jnp.concatenate([jnp.ones((seq_len, QK_NOPE), F32), cos32, jnp.zeros((seq_len, pad), F32)], axis=1)
    sin_x = jnp.concatenate([jnp.zeros((seq_len, QK_NOPE), F32), sin32, jnp.zeros((seq_len, pad), F32)], axis=1)
    cos_c = jnp.concatenate([jnp.ones((ctx_rows, QK_NOPE + QK_ROPE), F32), jnp.zeros((ctx_rows, pad), F32)], axis=1)
    sin_c = jnp.zeros((ctx_rows, LANES), F32)
    return jnp.concatenate([cos_c, cos_x], axis=0), jnp.concatenate([sin_c, sin_x], axis=0)


def _even_layer(xa, mods, dims, tabs, p):
    batch, ctx_len, seq_len = dims
    tm = ROW_TILE
    t = xa.shape[0]
    n_ctx_tiles = batch * ctx_len // tm
    x_tps = seq_len // tm
    row = _mod_row_fn(n_ctx_tiles, x_tps, batch)
    w_in_x, wq, wkv = _even_weights(p["w_in"], p["w_uq"], p["w_ukv"])
    cos_t, sin_t = tabs
    tab_idx = lambda i: jnp.where(i < n_ctx_tiles, 0, 1 + (i - n_ctx_tiles) % x_tps)
    gbu, q, k, v = _even_proj(xa, p["norm1_g"].reshape(1, -1), mods, row, t // tm, w_in_x,
                              p["q_norm_g"].reshape(1, -1), wq, p["kv_norm_g"].reshape(1, -1), wkv,
                              cos_t, sin_t, tab_idx)
    conv_y = _even_conv(gbu, p["conv_w"], n_ctx_tiles, ctx_len // tm, x_tps)
    att = _attention(q, k, v, batch, ctx_len, seq_len)
    w_out = p["w_out"].astype(BF16)
    xa = _resid_mm(xa, mods, 2, row, [conv_y, att], [w_out[:CONV_CH], w_out[CONV_CH:]])
    hmid = _ffn_up(xa, p["norm2_g"].reshape(1, -1), mods, row, p["ffn_w1"].astype(BF16))
    return _resid_mm(xa, mods, 5, row, [hmid], [p["ffn_w2"].astype(BF16)])


def _odd_layer(xa, mods, dims, p, ctx_out):
    batch, ctx_len, seq_len = dims
    tm = ROW_TILE
    t = xa.shape[0]
    n_ctx_tiles = batch * ctx_len // tm
    x_tps = seq_len // tm
    row = _mod_row_fn(n_ctx_tiles, x_tps, batch)
    n_w = p["w_in"].shape[1]
    w_in = jnp.pad(p["w_in"], ((0, 0), (0, 4 * GDN_QK + LANES - n_w))).astype(BF16)
    qkvn, z, abt = _odd_proj(xa, p["norm1_g"].reshape(1, -1), mods, row, w_in, p["qkv_conv_w"],
                             n_ctx_tiles, ctx_len // tm, x_tps)
    abt_chunks = abt[:, :4 * GDN_HEADS].reshape(t // GDN_STEP, GDN_STEP, 4 * GDN_HEADS).transpose(0, 2, 1)
    alog_b = jnp.broadcast_to(p["a_log"].reshape(2 * GDN_HEADS, 1), (2 * GDN_HEADS, GDN_STEP))
    dtb_b = jnp.broadcast_to(p["dt_bias"].reshape(2 * GDN_HEADS, 1), (2 * GDN_HEADS, GDN_STEP))
    o_dirs = _gdn(qkvn, abt_chunks, alog_b, dtb_b, batch, ctx_len, seq_len)
    row_off = 0 if ctx_out else n_ctx_tiles
    n_rows = t - row_off * tm
    xo = _odd_out(xa, mods, row, o_dirs, z, p["o_norm_g"].reshape(1, -1), p["w_out"].astype(BF16), row_off, n_rows)
    row_o = row if ctx_out else _mod_row_fn(0, x_tps, batch)
    return _moe(xo, p["norm2_g"].reshape(1, -1), mods, row_o, p["router_w"],
                p["moe_w1"].astype(BF16), p["moe_w2"].astype(BF16))


_EVEN_NAMES = ("mod_w", "mod_b", "norm1_g", "w_in", "conv_w", "q_norm_g", "w_uq", "kv_norm_g", "w_ukv", "w_out",
               "norm2_g", "ffn_w1", "ffn_w2")
_ODD_NAMES = ("mod_w", "mod_b", "norm1_g", "w_in", "qkv_conv_w", "a_log", "dt_bias", "o_norm_g", "w_out",
              "norm2_g", "router_w", "moe_w1", "moe_w2")


def kernel(x, c, ctx, c_ctx, l0_mod_w, l0_mod_b, l0_norm1_g, l0_w_in, l0_conv_w, l0_q_norm_g, l0_w_uq, l0_kv_norm_g, l0_w_ukv, l0_w_out, l0_norm2_g, l0_ffn_w1, l0_ffn_w2, l1_mod_w, l1_mod_b, l1_norm1_g, l1_w_in, l1_qkv_conv_w, l1_a_log, l1_dt_bias, l1_o_norm_g, l1_w_out, l1_norm2_g, l1_router_w, l1_moe_w1, l1_moe_w2, l2_mod_w, l2_mod_b, l2_norm1_g, l2_w_in, l2_conv_w, l2_q_norm_g, l2_w_uq, l2_kv_norm_g, l2_w_ukv, l2_w_out, l2_norm2_g, l2_ffn_w1, l2_ffn_w2, l3_mod_w, l3_mod_b, l3_norm1_g, l3_w_in, l3_qkv_conv_w, l3_a_log, l3_dt_bias, l3_o_norm_g, l3_w_out, l3_norm2_g, l3_router_w, l3_moe_w1, l3_moe_w2, final_norm_g):
    batch, seq_len, d = x.shape
    ctx_len = ctx.shape[1]
    assert seq_len % ROW_TILE == 0 and ctx_len % ROW_TILE == 0 and seq_len % GRID_W == 0
    dims = (batch, ctx_len, seq_len)
    layers = (
        dict(zip(_EVEN_NAMES, (l0_mod_w, l0_mod_b, l0_norm1_g, l0_w_in, l0_conv_w, l0_q_norm_g, l0_w_uq,
                               l0_kv_norm_g, l0_w_ukv, l0_w_out, l0_norm2_g, l0_ffn_w1, l0_ffn_w2))),
        dict(zip(_ODD_NAMES, (l1_mod_w, l1_mod_b, l1_norm1_g, l1_w_in, l1_qkv_conv_w, l1_a_log, l1_dt_bias,
                              l1_o_norm_g, l1_w_out, l1_norm2_g, l1_router_w, l1_moe_w1, l1_moe_w2))),
        dict(zip(_EVEN_NAMES, (l2_mod_w, l2_mod_b, l2_norm1_g, l2_w_in, l2_conv_w, l2_q_norm_g, l2_w_uq,
                               l2_kv_norm_g, l2_w_ukv, l2_w_out, l2_norm2_g, l2_ffn_w1, l2_ffn_w2))),
        dict(zip(_ODD_NAMES, (l3_mod_w, l3_mod_b, l3_norm1_g, l3_w_in, l3_qkv_conv_w, l3_a_log, l3_dt_bias,
                              l3_o_norm_g, l3_w_out, l3_norm2_g, l3_router_w, l3_moe_w1, l3_moe_w2))),
    )
    xa = jnp.concatenate([ctx.reshape(batch * ctx_len, d), x.reshape(batch * seq_len, d)], axis=0)
    mod_rows = 8 * ((batch + 1 + 7) // 8)
    cc = jnp.zeros((mod_rows, d), F32).at[:batch].set(c).at[batch].set(c_ctx)
    tabs = _rope_tables(seq_len, ROW_TILE)
    n_layers = len(layers)
    for li, p in enumerate(layers):
        mods = _mods(cc, p["mod_w"], p["mod_b"])
        if li % 2 == 0:
            xa = _even_layer(xa, mods, dims, tabs, p)
        else:
            xa = _odd_layer(xa, mods, dims, p, ctx_out=li < n_layers - 1)
    out = _final_norm(xa, final_norm_g.reshape(1, -1), 0, batch * seq_len)
    return out.reshape(batch, seq_len, d)
```

```python
import functools
import math

import jax
import jax.numpy as jnp
from jax import lax
from jax.experimental import pallas as pl
from jax.experimental.pallas import tpu as pltpu

F32 = jnp.float32
BF16 = jnp.bfloat16

NORM_EPS = 1e-6
N_MOD = 6
GRID_W = 64
ROPE_BASE = 10000.0

CONV_CH = 512
MLA_HEADS = 8
Q_RANK = 256
KV_RANK = 128
QK_NOPE = 64
QK_ROPE = 32
V_HEAD = 64
MLA_SCALE = (QK_NOPE + QK_ROPE) ** -0.5

GDN_HEADS = 8
GDN_DK = 128
GDN_CHUNK = 64
GDN_QK = GDN_HEADS * GDN_DK

N_EXPERTS = 8

LANES = 128
ROW_TILE = 256
GDN_STEP = 256
GDN_HEAD_GROUP = 8
ROW_DMA_UNROLL = 8
EXPERT_TILE = 512
EXPERT_FCHUNK = 512
VMEM_LIMIT_BYTES = 56 * 1024 * 1024
EXPERT_VMEM_LIMIT_BYTES = 62 * 1024 * 1024


def _cparams(*sem):
    return pltpu.CompilerParams(dimension_semantics=sem, vmem_limit_bytes=VMEM_LIMIT_BYTES)


def _silu(x):
    return x * (1.0 / (1.0 + jnp.exp(-x)))


def _rms(x, g):
    return x * lax.rsqrt(jnp.mean(x * x, axis=-1, keepdims=True) + NORM_EPS) * g


def _dot(a, b):
    return jnp.dot(a, b, preferred_element_type=F32)


def _dot_nt(a, b):
    return lax.dot_general(a, b, (((1,), (1,)), ((), ())), preferred_element_type=F32)


def _dot_tn(a, b):
    return lax.dot_general(a, b, (((0,), (0,)), ((), ())), preferred_element_type=F32)


def _split3(x):
    x1 = x.astype(BF16)
    r1 = x - x1.astype(F32)
    x2 = r1.astype(BF16)
    x3 = (r1 - x2.astype(F32)).astype(BF16)
    return x1, x2, x3


def _dot_f32(a, b):
    a1, a2, a3 = _split3(a)
    b1, b2, b3 = _split3(b)
    acc = _dot(a1, b3) + _dot(a2, b2) + _dot(a3, b1)
    acc = acc + _dot(a1, b2) + _dot(a2, b1)
    return acc + _dot(a1, b1)


def _dot_f32_exact_rhs(a, m_bf16):
    a1, a2, a3 = _split3(a)
    return _dot(a3, m_bf16) + _dot(a2, m_bf16) + _dot(a1, m_bf16)


def _mod_row_fn(n_ctx_tiles, tiles_per_seq, batch):
    def row(i):
        return jnp.where(i < n_ctx_tiles, batch, (i - n_ctx_tiles) // tiles_per_seq)
    return row


def _mods_kernel(c_ref, w_ref, b_ref, o_ref):
    o_ref[...] = _dot_f32(_silu(c_ref[...]), w_ref[...]) + b_ref[...]


def _mods(cc, mod_w, mod_b):
    rows, d = cc.shape
    n = mod_w.shape[1]
    out = pl.pallas_call(
        _mods_kernel,
        grid=(n // d,),
        in_specs=[pl.BlockSpec((rows, d), lambda j: (0, 0)),
                  pl.BlockSpec((d, d), lambda j: (0, j)),
                  pl.BlockSpec((1, d), lambda j: (0, j))],
        out_specs=pl.BlockSpec((rows, d), lambda j: (0, j)),
        out_shape=jax.ShapeDtypeStruct((rows, n), F32),
        compiler_params=_cparams("arbitrary"),
        name="adaln_mods",
    )(cc, mod_w, mod_b.reshape(1, n))
    return out.reshape(rows, 1, n)


def _mod_spec(d, k, row, off=0):
    return pl.BlockSpec((1, 1, d), lambda i: (row(i + off), 0, k))


def _even_proj_kernel(x_ref, xp_ref, xn_ref, g_ref, sh_ref, sc_ref, win_ref, cw_ref, qg_ref, wq_ref, kvg_ref, wkv_ref,
                      cos_ref, sin_ref, conv_ref, q_ref, k_ref, v_ref, *, n_ctx_tiles, ctx_tps, x_tps):
    i = pl.program_id(0)
    keep_prev, keep_next = _seq_edges(i, n_ctx_tiles, ctx_tps, x_tps)
    tm = x_ref.shape[0]
    xs = jnp.concatenate([x_ref[...], xp_ref[...], xn_ref[...]], axis=0)
    hb_all = (_rms(xs, g_ref[...]) * (1.0 + sc_ref[0]) + sh_ref[0]).astype(BF16)
    hb = hb_all[:tm]
    c = CONV_CH
    n_gbu = 3 * c
    z_all = _dot(hb_all, win_ref[:, c:2 * c]) * _dot(hb_all, win_ref[:, 2 * c:n_gbu])
    z = z_all[:tm]
    zm, zp = _shifted(z, z_all[tm + X_HALO - 1:tm + X_HALO] * keep_prev, z_all[tm + X_HALO:tm + X_HALO + 1] * keep_next)
    cw = cw_ref[...]
    conv = zm * cw[0:1] + z * cw[1:2] + zp * cw[2:3]
    conv_ref[...] = (_dot(hb, win_ref[:, :c]) * conv).astype(conv_ref.dtype)
    rest = _dot(hb, win_ref[:, n_gbu:])
    cq = rest[:, :Q_RANK]
    ckv = rest[:, Q_RANK:Q_RANK + KV_RANK]
    kr = rest[:, Q_RANK + KV_RANK:Q_RANK + KV_RANK + LANES]
    kr_sw = rest[:, Q_RANK + KV_RANK + LANES:]
    cos = cos_ref[...]
    sin = sin_ref[...]
    kr_rot = kr * cos + kr_sw * sin
    qq = _dot(_rms(cq, qg_ref[...]).astype(BF16), wq_ref[...])
    kv = _dot(_rms(ckv, kvg_ref[...]).astype(BF16), wkv_ref[...])
    hw = MLA_HEADS * LANES
    for h in range(MLA_HEADS):
        sl = slice(h * LANES, (h + 1) * LANES)
        qh = (qq[:, sl] * cos + qq[:, hw + h * LANES:hw + (h + 1) * LANES] * sin) * MLA_SCALE
        q_ref[:, sl] = qh.astype(q_ref.dtype)
        k_ref[:, sl] = (kv[:, sl] + kr_rot).astype(k_ref.dtype)
    v_ref[...] = kv[:, hw:].astype(v_ref.dtype)


def _even_proj(x2, g, mods, row, n_tiles, w_in, conv_w, qg, wq, kvg, wkv, cos_t, sin_t, tab_idx,
               n_ctx_tiles, ctx_tps, x_tps):
    t, d = x2.shape
    tm = ROW_TILE
    n_in = w_in.shape[1]
    hw = MLA_HEADS * LANES
    const = lambda i: (0, 0)
    r = tm // X_HALO
    nb = t // X_HALO
    return pl.pallas_call(
        functools.partial(_even_proj_kernel, n_ctx_tiles=n_ctx_tiles, ctx_tps=ctx_tps, x_tps=x_tps),
        grid=(n_tiles,),
        in_specs=[pl.BlockSpec((tm, d), lambda i: (i, 0)),
                  pl.BlockSpec((X_HALO, d), lambda i: (jnp.maximum(i * r - 1, 0), 0)),
                  pl.BlockSpec((X_HALO, d), lambda i: (jnp.minimum((i + 1) * r, nb - 1), 0)),
                  pl.BlockSpec((1, d), const),
                  _mod_spec(d, 0, row), _mod_spec(d, 1, row),
                  pl.BlockSpec((d, n_in), const),
                  pl.BlockSpec(conv_w.shape, const),
                  pl.BlockSpec((1, Q_RANK), const),
                  pl.BlockSpec(wq.shape, const),
                  pl.BlockSpec((1, KV_RANK), const),
                  pl.BlockSpec(wkv.shape, const),
                  pl.BlockSpec((tm, LANES), lambda i: (tab_idx(i), 0)),
                  pl.BlockSpec((tm, LANES), lambda i: (tab_idx(i), 0))],
        out_specs=[pl.BlockSpec((tm, CONV_CH), lambda i: (i, 0)),
                   pl.BlockSpec((tm, hw), lambda i: (i, 0)),
                   pl.BlockSpec((tm, hw), lambda i: (i, 0)),
                   pl.BlockSpec((tm, MLA_HEADS * V_HEAD), lambda i: (i, 0))],
        out_shape=[jax.ShapeDtypeStruct((t, CONV_CH), BF16),
                   jax.ShapeDtypeStruct((t, hw), BF16),
                   jax.ShapeDtypeStruct((t, hw), BF16),
                   jax.ShapeDtypeStruct((t, MLA_HEADS * V_HEAD), BF16)],
        compiler_params=_cparams("arbitrary"),
        name="even_proj",
    )(x2, x2, x2, g, mods, mods, w_in, conv_w, qg, wq, kvg, wkv, cos_t, sin_t)


def _shifted(z, prev_row, next_row):
    tm = z.shape[0]
    ridx = lax.broadcasted_iota(jnp.int32, z.shape, 0)
    zm = jnp.where(ridx == 0, prev_row, pltpu.roll(z, 1, 0))
    zp = jnp.where(ridx == tm - 1, next_row, pltpu.roll(z, tm - 1, 0))
    return zm, zp


def _seq_edges(i, n_ctx_tiles, ctx_tps, x_tps):
    j = jnp.where(i < n_ctx_tiles, i % ctx_tps, (i - n_ctx_tiles) % x_tps)
    tps = jnp.where(i < n_ctx_tiles, ctx_tps, x_tps)
    keep_prev = (j != 0).astype(F32)
    keep_next = (j != tps - 1).astype(F32)
    return keep_prev, keep_next


def _attn_kernel(q_ref, kc_ref, kx_ref, vc_ref, vx_ref, o_ref, *, n_ctx_q):
    qi = pl.program_id(1)

    def run(use_x):
        for hp in range(MLA_HEADS // 2):
            outs = []
            for h in (2 * hp, 2 * hp + 1):
                ks = slice(h * LANES, (h + 1) * LANES)
                vs = slice(h * V_HEAD, (h + 1) * V_HEAD)
                q = q_ref[:, ks]
                sc = _dot_nt(q, kc_ref[:, ks])
                m = jnp.max(sc, axis=1, keepdims=True)
                if use_x:
                    sx = _dot_nt(q, kx_ref[:, ks])
                    m = jnp.maximum(m, jnp.max(sx, axis=1, keepdims=True))
                pc = jnp.exp(sc - m)
                l = jnp.sum(pc, axis=1, keepdims=True)
                o = _dot(pc.astype(BF16), vc_ref[:, vs])
                if use_x:
                    px = jnp.exp(sx - m)
                    l = l + jnp.sum(px, axis=1, keepdims=True)
                    o = o + _dot(px.astype(BF16), vx_ref[:, vs])
                outs.append(o * (1.0 / l))
            o_ref[:, hp * LANES:(hp + 1) * LANES] = jnp.concatenate(outs, axis=1).astype(o_ref.dtype)

    @pl.when(qi < n_ctx_q)
    def _():
        run(False)

    @pl.when(qi >= n_ctx_q)
    def _():
        run(True)


def _attention(q, k, v, batch, ctx_len, seq_len):
    tq = ROW_TILE
    n_ctx_q = ctx_len // tq
    n_x_q = seq_len // tq
    ctx_blocks = batch * ctx_len // tq
    assert (batch * ctx_len) % seq_len == 0
    x_blk0 = batch * ctx_len // seq_len
    hw = MLA_HEADS * LANES
    vw = MLA_HEADS * V_HEAD

    def qrow(b, qi):
        return jnp.where(qi < n_ctx_q, b * n_ctx_q + qi, ctx_blocks + b * n_x_q + (qi - n_ctx_q))

    return pl.pallas_call(
        functools.partial(_attn_kernel, n_ctx_q=n_ctx_q),
        grid=(batch, n_ctx_q + n_x_q),
        in_specs=[pl.BlockSpec((tq, hw), lambda b, qi: (qrow(b, qi), 0)),
                  pl.BlockSpec((ctx_len, hw), lambda b, qi: (b, 0)),
                  pl.BlockSpec((seq_len, hw), lambda b, qi: (x_blk0 + b, 0)),
                  pl.BlockSpec((ctx_len, vw), lambda b, qi: (b, 0)),
                  pl.BlockSpec((seq_len, vw), lambda b, qi: (x_blk0 + b, 0))],
        out_specs=pl.BlockSpec((tq, vw), lambda b, qi: (qrow(b, qi), 0)),
        out_shape=jax.ShapeDtypeStruct((q.shape[0], vw), BF16),
        compiler_params=_cparams("arbitrary", "arbitrary"),
        name="mla_attention",
    )(q, k, k, v, v)


def _resid_mm_kernel(*refs, n_lhs):
    x_ref, gate_ref = refs[0], refs[1]
    lhs = refs[2:2 + n_lhs]
    ws = refs[2 + n_lhs:2 + 2 * n_lhs]
    o_ref = refs[2 + 2 * n_lhs]
    acc = _dot(lhs[0][...], ws[0][...])
    for a_ref, w_ref in zip(lhs[1:], ws[1:]):
        acc = acc + _dot(a_ref[...], w_ref[...])
    o_ref[...] = x_ref[...] + gate_ref[0] * acc


def _resid_mm(x2, mods, k_gate, row, lhs_list, w_list, tm=512):
    t, d = x2.shape
    n = len(lhs_list)
    in_specs = [pl.BlockSpec((tm, d), lambda i: (i, 0)), _mod_spec(d, k_gate, lambda i: row(i * tm // ROW_TILE))]
    in_specs += [pl.BlockSpec((tm, a.shape[1]), lambda i: (i, 0)) for a in lhs_list]
    in_specs += [pl.BlockSpec(w.shape, lambda i: (0, 0)) for w in w_list]
    return pl.pallas_call(
        functools.partial(_resid_mm_kernel, n_lhs=n),
        grid=(t // tm,),
        in_specs=in_specs,
        out_specs=pl.BlockSpec((tm, d), lambda i: (i, 0)),
        out_shape=jax.ShapeDtypeStruct((t, d), F32),
        compiler_params=_cparams("arbitrary"),
        name="resid_matmul",
    )(x2, mods, *lhs_list, *w_list)


def _ffn_kernel(x_ref, g_ref, sh_ref, sc_ref, gate_ref, w1_ref, w2_ref, o_ref, *, ffn, chunk):
    x = x_ref[...]
    hb = (_rms(x, g_ref[...]) * (1.0 + sc_ref[0]) + sh_ref[0]).astype(BF16)
    acc = None
    for c0 in range(0, ffn, chunk):
        gate = _dot(hb, w1_ref[:, c0:c0 + chunk])
        up = _dot(hb, w1_ref[:, ffn + c0:ffn + c0 + chunk])
        part = _dot((_silu(gate) * up).astype(BF16), w2_ref[c0:c0 + chunk, :])
        acc = part if acc is None else acc + part
    o_ref[...] = x + gate_ref[0] * acc


def _ffn(x2, g, mods, row, w1, w2):
    t, d = x2.shape
    tm = ROW_TILE
    ffn = w2.shape[0]
    chunk = ffn // 2
    assert ffn % chunk == 0 and chunk % LANES == 0
    return pl.pallas_call(
        functools.partial(_ffn_kernel, ffn=ffn, chunk=chunk),
        grid=(t // tm,),
        in_specs=[pl.BlockSpec((tm, d), lambda i: (i, 0)),
                  pl.BlockSpec((1, d), lambda i: (0, 0)),
                  _mod_spec(d, 3, row), _mod_spec(d, 4, row), _mod_spec(d, 5, row),
                  pl.BlockSpec(w1.shape, lambda i: (0, 0)),
                  pl.BlockSpec(w2.shape, lambda i: (0, 0))],
        out_specs=pl.BlockSpec((tm, d), lambda i: (i, 0)),
        out_shape=jax.ShapeDtypeStruct((t, d), F32),
        compiler_params=_cparams("arbitrary"),
        name="dense_ffn",
    )(x2, g, mods, mods, mods, w1, w2)


X_HALO = 8


def _odd_proj_kernel(x_ref, xp_ref, xn_ref, g_ref, sh_ref, sc_ref, w_ref, cw_ref, qkv_ref, gate_ref, abt_ref,
                     *, chunk, n_ctx_tiles, ctx_tps, x_tps):
    i = pl.program_id(0)
    keep_prev, keep_next = _seq_edges(i, n_ctx_tiles, ctx_tps, x_tps)
    tm = x_ref.shape[0]
    xs = jnp.concatenate([x_ref[...], xp_ref[...], xn_ref[...]], axis=0)
    hb = (_rms(xs, g_ref[...]) * (1.0 + sc_ref[0]) + sh_ref[0]).astype(BF16)
    n_qkv = qkv_ref.shape[0] * LANES
    n_gate = gate_ref.shape[1]
    cw = cw_ref[...]
    for c0 in range(0, n_qkv, chunk):
        z_all = _dot(hb, w_ref[:, c0:c0 + chunk])
        z = z_all[:tm]
        zprev = z_all[tm + X_HALO - 1:tm + X_HALO] * keep_prev
        znext = z_all[tm + X_HALO:tm + X_HALO + 1] * keep_next
        zm, zp = _shifted(z, zprev, znext)
        y = _silu(zm * cw[0:1, c0:c0 + chunk] + z * cw[1:2, c0:c0 + chunk] + zp * cw[2:3, c0:c0 + chunk])
        for hh in range(chunk // LANES):
            head = c0 // LANES + hh
            yh = y[:, hh * LANES:(hh + 1) * LANES]
            if head < 2 * GDN_HEADS:
                nrm = lax.rsqrt(jnp.sum(yh * yh, axis=-1, keepdims=True) + NORM_EPS)
                if head < GDN_HEADS:
                    nrm = nrm * (GDN_DK ** -0.5)
                yh = yh * nrm
            qkv_ref[head] = yh
    hb_cur = hb[:tm]
    for c0 in range(0, n_gate, chunk):
        gate_ref[:, c0:c0 + chunk] = _dot(hb_cur, w_ref[:, n_qkv + c0:n_qkv + c0 + chunk]).astype(gate_ref.dtype)
    abt_ref[...] = _dot(hb_cur, w_ref[:, n_qkv + n_gate:])


def _odd_proj(x2, g, mods, row, w_in, conv_w, n_ctx_tiles, ctx_tps, x_tps):
    t, d = x2.shape
    tm = ROW_TILE
    n_qkv = 3 * GDN_QK
    n_gate = GDN_QK
    r = tm // X_HALO
    nb = t // X_HALO
    return pl.pallas_call(
        functools.partial(_odd_proj_kernel, chunk=512, n_ctx_tiles=n_ctx_tiles, ctx_tps=ctx_tps, x_tps=x_tps),
        grid=(t // tm,),
        in_specs=[pl.BlockSpec((tm, d), lambda i: (i, 0)),
                  pl.BlockSpec((X_HALO, d), lambda i: (jnp.maximum(i * r - 1, 0), 0)),
                  pl.BlockSpec((X_HALO, d), lambda i: (jnp.minimum((i + 1) * r, nb - 1), 0)),
                  pl.BlockSpec((1, d), lambda i: (0, 0)),
                  _mod_spec(d, 0, row), _mod_spec(d, 1, row),
                  pl.BlockSpec(w_in.shape, lambda i: (0, 0)),
                  pl.BlockSpec(conv_w.shape, lambda i: (0, 0))],
        out_specs=[pl.BlockSpec((n_qkv // LANES, tm, LANES), lambda i: (0, i, 0)),
                   pl.BlockSpec((tm, n_gate), lambda i: (i, 0)),
                   pl.BlockSpec((tm, LANES), lambda i: (i, 0))],
        out_shape=[jax.ShapeDtypeStruct((n_qkv // LANES, t, LANES), F32),
                   jax.ShapeDtypeStruct((t, n_gate), BF16),
                   jax.ShapeDtypeStruct((t, LANES), F32)],
        compiler_params=_cparams("arbitrary"),
        name="odd_proj",
    )(x2, x2, x2, g, mods, mods, w_in, conv_w)


def _gdn_kernel(qkv_ref, abt_ref, alog_ref, dtb_ref, o_ref,
                s_ref, gc_s, beta_s, gtot_s, uw_s, qg_s, kd_s, attn_s, vnew_s, gtc_s):
    d = pl.program_id(1)
    step = pl.program_id(2)
    n = GDN_STEP
    c = GDN_CHUNK
    nchunk = n // c
    h_n = GDN_HEADS
    sgn = 1 - 2 * d

    @pl.when(step == 0)
    def _():
        s_ref[...] = jnp.zeros(s_ref.shape, s_ref.dtype)

    ri = lax.broadcasted_iota(jnp.int32, (n, n), 0)
    ci = lax.broadcasted_iota(jnp.int32, (n, n), 1)
    same = (ri // c) == (ci // c)
    rel = (ri - ci) * sgn
    incl = same & (rel >= 0)
    strict = same & (rel > 0)
    eye = (ri == ci).astype(F32)

    hrow = pl.ds(d * h_n, h_n)
    a_all = abt_ref[0, hrow, :]
    bt_all = abt_ref[0, pl.ds(2 * h_n + d * h_n, h_n), :]
    sp_in = a_all + dtb_ref[hrow, :]
    softplus = jnp.maximum(sp_in, 0.0) + jnp.log(1.0 + jnp.exp(-jnp.abs(sp_in)))
    g_all = -jnp.exp(alog_ref[hrow, :]) * softplus
    beta_s[...] = 1.0 / (1.0 + jnp.exp(-bt_all))
    cum_m = jnp.where(same & ((ci - ri) * sgn >= 0), 1.0, 0.0).astype(BF16)
    gc_s[...] = _dot_f32_exact_rhs(g_all, cum_m)
    gtot_s[...] = _dot_f32_exact_rhs(g_all, same.astype(BF16))

    grp = uw_s.shape[0]
    n_sq = (c - 1).bit_length() - 1

    def head_group(gi, carry):
        heads = [gi * grp + u for u in range(grp)]
        us = range(grp)
        q = [qkv_ref[h] for h in heads]
        k = [qkv_ref[h_n + h] for h in heads]
        v = [qkv_ref[2 * h_n + h] for h in heads]
        gc_row = [gc_s[pl.ds(h, 1), :] for h in heads]
        gc_col = [jnp.sum(eye * gc_row[u], axis=1, keepdims=True) for u in us]
        beta_col = [jnp.sum(eye * beta_s[pl.ds(h, 1), :], axis=1, keepdims=True) for h in heads]
        gtot_col = [jnp.sum(eye * gtot_s[pl.ds(h, 1), :], axis=1, keepdims=True) for h in heads]
        decay = [jnp.exp(jnp.where(incl, gc_col[u] - gc_row[u], -1e30)) for u in us]
        kb = [k[u] * beta_col[u] for u in us]
        k16 = [k[u].astype(BF16) for u in us]
        p = [jnp.where(strict, -(_dot_nt(kb[u].astype(BF16), k16[u]) * decay[u]), 0.0) for u in us]
        for u in us:
            attn_s[u] = (_dot_nt(q[u].astype(BF16), k16[u]) * decay[u]).astype(attn_s.dtype)
            gtc_s[u] = jnp.broadcast_to(gtot_col[u], gtc_s.shape[1:])
        tinv = [eye + p[u] for u in us]
        p16 = [p[u].astype(BF16) for u in us]
        p = [_dot(p16[u], p16[u]) for u in us]
        for _ in range(n_sq - 1):
            p16 = [p[u].astype(BF16) for u in us]
            tp = [_dot(jnp.concatenate([tinv[u].astype(BF16), p16[u]], axis=0), p16[u]) for u in us]
            tinv = [tinv[u] + tp[u][:n] for u in us]
            p = [tp[u][n:] for u in us]
        tinv = [tinv[u] + _dot(tinv[u].astype(BF16), p[u].astype(BF16)) for u in us]
        egc = [jnp.exp(gc_col[u]) for u in us]
        for u in us:
            rhs = jnp.concatenate([v[u] * beta_col[u], kb[u] * egc[u]], axis=1).astype(BF16)
            uw_s[u] = _dot(tinv[u].astype(BF16), rhs)
            qg_s[u] = q[u] * egc[u]
            kd_s[u] = k[u] * jnp.exp(gtot_col[u] - gc_col[u])
            vnew_s[u] = jnp.zeros(vnew_s.shape[1:], vnew_s.dtype)
        s = [s_ref[h] for h in heads]
        for cstep in range(nchunk):
            cc = jnp.where(d == 0, cstep, nchunk - 1 - cstep)
            r0 = pl.multiple_of(cc * c, c)
            rows = pl.ds(r0, c)
            ws_qs = [_dot(jnp.concatenate([uw_s[u, rows, LANES:], qg_s[u, rows, :]], axis=0).astype(BF16),
                          s[u].astype(BF16)) for u in us]
            v_new = [uw_s[u, rows, :LANES] - ws_qs[u][:c] for u in us]
            for u in us:
                vnew_s[u, rows, :] = v_new[u].astype(vnew_s.dtype)
            for u in us:
                o_ref[0, heads[u], rows, :] = ws_qs[u][c:] + _dot(attn_s[u, rows, :], vnew_s[u])
            s = [s[u] * jnp.exp(gtc_s[u, pl.ds(r0, 1), :])
                 + _dot_tn(kd_s[u, rows, :].astype(BF16), v_new[u].astype(BF16)) for u in us]
        for u in us:
            s_ref[heads[u]] = s[u]
        return carry

    lax.fori_loop(0, h_n // grp, head_group, 0)


def _gdn(qkvn, abt_chunks, alog_b, dtb_b, batch, ctx_len, seq_len):
    nh3, t, _ = qkvn.shape
    n = GDN_STEP
    nc = ctx_len // n
    nx = seq_len // n
    ctx_blocks = batch * nc

    def rowblk(b, d, s):
        sc = jnp.where(d == 0, s, nc - 1 - s)
        sx = jnp.where(d == 0, s - nc, nx - 1 - (s - nc))
        return jnp.where(s < nc, b * nc + sc, ctx_blocks + b * nx + sx)

    vm = pltpu.VMEM
    g = GDN_HEAD_GROUP
    return pl.pallas_call(
        _gdn_kernel,
        grid=(batch, 2, nc + nx),
        in_specs=[pl.BlockSpec((nh3, n, LANES), lambda b, d, s: (0, rowblk(b, d, s), 0)),
                  pl.BlockSpec((1, 4 * GDN_HEADS, n), lambda b, d, s: (rowblk(b, d, s), 0, 0)),
                  pl.BlockSpec(alog_b.shape, lambda b, d, s: (0, 0)),
                  pl.BlockSpec(dtb_b.shape, lambda b, d, s: (0, 0))],
        out_specs=pl.BlockSpec((1, GDN_HEADS, n, LANES), lambda b, d, s: (d, 0, rowblk(b, d, s), 0)),
        out_shape=jax.ShapeDtypeStruct((2, GDN_HEADS, t, LANES), F32),
        scratch_shapes=[vm((GDN_HEADS, GDN_DK, LANES), F32),
                        vm((GDN_HEADS, n), F32),
                        vm((GDN_HEADS, n), F32),
                        vm((GDN_HEADS, n), F32),
                        vm((g, n, 2 * LANES), F32),
                        vm((g, n, LANES), F32),
                        vm((g, n, LANES), F32),
                        vm((g, n, n), BF16),
                        vm((g, n, LANES), BF16),
                        vm((g, n, LANES), F32)],
        compiler_params=_cparams("arbitrary", "arbitrary", "arbitrary"),
        name="gated_delta",
    )(qkvn, abt_chunks, alog_b, dtb_b)


def _odd_out_kernel(x_ref, gate_ref, o_ref_in, z_ref, og_ref, w_ref, out_ref):
    parts = []
    og = og_ref[...]
    for h in range(GDN_HEADS):
        o = o_ref_in[0, h] + o_ref_in[1, h]
        y = _rms(o, og) * _silu(z_ref[:, h * LANES:(h + 1) * LANES].astype(F32))
        parts.append(y.astype(BF16))
    y_all = jnp.concatenate(parts, axis=1)
    out_ref[...] = x_ref[...] + gate_ref[0] * _dot(y_all, w_ref[...])


def _odd_out(x2, mods, row, o_dirs, z, o_norm_g, w_out, row_off, n_rows):
    d = x2.shape[1]
    tm = ROW_TILE
    return pl.pallas_call(
        _odd_out_kernel,
        grid=(n_rows // tm,),
        in_specs=[pl.BlockSpec((tm, d), lambda i: (i + row_off, 0)),
                  _mod_spec(d, 2, row, row_off),
                  pl.BlockSpec((2, GDN_HEADS, tm, LANES), lambda i: (0, 0, i + row_off, 0)),
                  pl.BlockSpec((tm, z.shape[1]), lambda i: (i + row_off, 0)),
                  pl.BlockSpec((1, LANES), lambda i: (0, 0)),
                  pl.BlockSpec(w_out.shape, lambda i: (0, 0))],
        out_specs=pl.BlockSpec((tm, d), lambda i: (i, 0)),
        out_shape=jax.ShapeDtypeStruct((n_rows, d), F32),
        compiler_params=_cparams("arbitrary"),
        name="odd_out",
    )(x2, mods, o_dirs, z, o_norm_g, w_out)


def _router_kernel(x_ref, g_ref, sh_ref, sc_ref, rw_ref, h_ref, info_ref, cnt_ref, carry_s):
    i = pl.program_id(0)

    @pl.when(i == 0)
    def _():
        carry_s[...] = jnp.zeros(carry_s.shape, carry_s.dtype)

    hn = _rms(x_ref[...], g_ref[...]) * (1.0 + sc_ref[0]) + sh_ref[0]
    h_ref[...] = hn
    tm = hn.shape[0]
    lane = lax.broadcasted_iota(jnp.int32, (tm, LANES), 1)
    logits = jnp.where(lane < N_EXPERTS, _dot_f32(hn, rw_ref[...]), -jnp.inf)
    m1 = jnp.max(logits, axis=1, keepdims=True)
    i1 = jnp.min(jnp.where(logits == m1, lane, LANES), axis=1, keepdims=True)
    rest = jnp.where(lane == i1, -jnp.inf, logits)
    m2 = jnp.max(rest, axis=1, keepdims=True)
    i2 = jnp.min(jnp.where(rest == m2, lane, LANES), axis=1, keepdims=True)
    e2 = jnp.exp(m2 - m1)
    w1 = 1.0 / (1.0 + e2)
    w2 = e2 / (1.0 + e2)
    oh1 = (lane == i1).astype(F32)
    oh2 = (lane == i2).astype(F32)
    cnt = oh1 + oh2
    ri = lax.broadcasted_iota(jnp.int32, (tm, tm), 0)
    ci = lax.broadcasted_iota(jnp.int32, (tm, tm), 1)
    before = (ci < ri).astype(BF16)
    pre = _dot(before, cnt.astype(BF16)) + carry_s[0:1, :]
    r1 = jnp.sum(pre * oh1, axis=1, keepdims=True)
    r2 = jnp.sum(pre * oh2, axis=1, keepdims=True)
    info = jnp.where(lane == 0, i1.astype(F32), 0.0)
    info = jnp.where(lane == 1, i2.astype(F32), info)
    info = jnp.where(lane == 2, w1, info)
    info = jnp.where(lane == 3, w2, info)
    info = jnp.where(lane == 4, r1, info)
    info = jnp.where(lane == 5, r2, info)
    info_ref[...] = info
    total = carry_s[0:1, :] + jnp.sum(cnt, axis=0, keepdims=True)
    carry_s[...] = jnp.broadcast_to(total, carry_s.shape)
    cnt_ref[...] = jnp.broadcast_to(total, cnt_ref.shape)


def _router(x2, g, mods, row, router_w_pad):
    t, d = x2.shape
    tm = ROW_TILE
    return pl.pallas_call(
        _router_kernel,
        grid=(t // tm,),
        in_specs=[pl.BlockSpec((tm, d), lambda i: (i, 0)),
                  pl.BlockSpec((1, d), lambda i: (0, 0)),
                  _mod_spec(d, 3, row), _mod_spec(d, 4, row),
                  pl.BlockSpec(router_w_pad.shape, lambda i: (0, 0))],
        out_specs=[pl.BlockSpec((tm, d), lambda i: (i, 0)),
                   pl.BlockSpec((tm, LANES), lambda i: (i, 0)),
                   pl.BlockSpec((8, LANES), lambda i: (0, 0))],
        out_shape=[jax.ShapeDtypeStruct((t, d), F32),
                   jax.ShapeDtypeStruct((t, LANES), F32),
                   jax.ShapeDtypeStruct((8, LANES), F32)],
        scratch_shapes=[pltpu.VMEM((8, LANES), F32)],
        compiler_params=_cparams("arbitrary"),
        name="moe_router",
    )(x2, g, mods, mods, router_w_pad)


def _row_copy(src_hbm, src_row, dst_ref, dst_row, sem):
    return pltpu.make_async_copy(src_hbm.at[pl.ds(src_row, 1)], dst_ref.at[pl.ds(dst_row, 1)], sem)


def _dispatch_kernel(dest_ref, h_ref, init_hbm, hs_hbm, sem, *, tm):
    del init_hbm

    def issue(r, carry):
        _row_copy(h_ref, r, hs_hbm, dest_ref[0, 0, r], sem.at[0]).start()
        _row_copy(h_ref, r, hs_hbm, dest_ref[0, 0, tm + r], sem.at[1]).start()
        return carry

    lax.fori_loop(0, tm, issue, 0, unroll=ROW_DMA_UNROLL)
    for k in range(2):
        pltpu.make_async_copy(h_ref, hs_hbm.at[pl.ds(0, tm)], sem.at[k]).wait()


def _dispatch(h, dest_tiles, n_sorted_rows):
    t, d = h.shape
    tm = ROW_TILE
    init = jnp.zeros((n_sorted_rows, d), h.dtype)
    return pl.pallas_call(
        functools.partial(_dispatch_kernel, tm=tm),
        grid=(t // tm,),
        in_specs=[pl.BlockSpec((1, 1, 2 * tm), lambda i: (i, 0, 0), memory_space=pltpu.SMEM),
                  pl.BlockSpec((tm, d), lambda i: (i, 0)),
                  pl.BlockSpec(memory_space=pl.ANY)],
        out_specs=pl.BlockSpec(memory_space=pl.ANY),
        out_shape=jax.ShapeDtypeStruct((n_sorted_rows, d), h.dtype),
        scratch_shapes=[pltpu.SemaphoreType.DMA((2,))],
        input_output_aliases={2: 0},
        compiler_params=_cparams("arbitrary"),
        name="moe_dispatch",
    )(dest_tiles, h, init)


def _expert_kernel(te_ref, nt_ref, hs_ref, w1_ref, w2_ref, ys_ref, *, ffn, chunk):
    del te_ref
    j = pl.program_id(0)

    @pl.when(j < nt_ref[0])
    def _():
        hb = hs_ref[...].astype(BF16)
        acc = None
        for c0 in range(0, ffn, chunk):
            gate = _dot(hb, w1_ref[:, c0:c0 + chunk])
            up = _dot(hb, w1_ref[:, ffn + c0:ffn + c0 + chunk])
            part = _dot((_silu(gate) * up).astype(BF16), w2_ref[c0:c0 + chunk, :])
            acc = part if acc is None else acc + part
        ys_ref[...] = acc

    @pl.when(j >= nt_ref[0])
    def _():
        ys_ref[...] = jnp.zeros(ys_ref.shape, ys_ref.dtype)


def _experts(hs, tile_expert, n_tiles_used, w1, w2):
    r, d = hs.shape
    tm = EXPERT_TILE
    ffn = w2.shape[1]
    assert ffn % EXPERT_FCHUNK == 0

    def tile(j, nt):
        return jnp.minimum(j, nt[0] - 1)

    grid_spec = pltpu.PrefetchScalarGridSpec(
        num_scalar_prefetch=2,
        grid=(r // tm,),
        in_specs=[pl.BlockSpec((tm, d), lambda j, te, nt: (tile(j, nt), 0)),
                  pl.BlockSpec((None, d, 2 * ffn), lambda j, te, nt: (te[tile(j, nt)], 0, 0)),
                  pl.BlockSpec((None, ffn, d), lambda j, te, nt: (te[tile(j, nt)], 0, 0))],
        out_specs=pl.BlockSpec((tm, d), lambda j, te, nt: (j, 0)),
    )
    return pl.pallas_call(
        functools.partial(_expert_kernel, ffn=ffn, chunk=EXPERT_FCHUNK),
        grid_spec=grid_spec,
        out_shape=jax.ShapeDtypeStruct((r, d), F32),
        compiler_params=pltpu.CompilerParams(dimension_semantics=("arbitrary",),
                                             vmem_limit_bytes=EXPERT_VMEM_LIMIT_BYTES),
        name="moe_experts",
    )(tile_expert, n_tiles_used, hs, w1, w2)


def _combine_kernel(dest_ref, x_ref, gate_ref, info_ref, fg_ref, ys_hbm, o_ref, y1_s, y2_s, sem, *, tm, final_norm):
    def issue(r, carry):
        _row_copy(ys_hbm, dest_ref[0, 0, r], y1_s, r, sem.at[0]).start()
        _row_copy(ys_hbm, dest_ref[0, 0, tm + r], y2_s, r, sem.at[1]).start()
        return carry

    lax.fori_loop(0, tm, issue, 0, unroll=ROW_DMA_UNROLL)
    pltpu.make_async_copy(ys_hbm.at[pl.ds(0, tm)], y1_s, sem.at[0]).wait()
    pltpu.make_async_copy(ys_hbm.at[pl.ds(0, tm)], y2_s, sem.at[1]).wait()
    info = info_ref[...]
    y = info[:, 2:3] * y1_s[...] + info[:, 3:4] * y2_s[...]
    out = x_ref[...] + gate_ref[0] * y
    if final_norm:
        out = _rms(out, fg_ref[...])
    o_ref[...] = out


def _combine(x2, mods, row, info, dest_tiles, ys, final_g, final_norm):
    t, d = x2.shape
    tm = ROW_TILE
    return pl.pallas_call(
        functools.partial(_combine_kernel, tm=tm, final_norm=final_norm),
        grid=(t // tm,),
        in_specs=[pl.BlockSpec((1, 1, 2 * tm), lambda i: (i, 0, 0), memory_space=pltpu.SMEM),
                  pl.BlockSpec((tm, d), lambda i: (i, 0)),
                  _mod_spec(d, 5, row),
                  pl.BlockSpec((tm, LANES), lambda i: (i, 0)),
                  pl.BlockSpec((1, d), lambda i: (0, 0)),
                  pl.BlockSpec(memory_space=pl.ANY)],
        out_specs=pl.BlockSpec((tm, d), lambda i: (i, 0)),
        out_shape=jax.ShapeDtypeStruct((t, d), F32),
        scratch_shapes=[pltpu.VMEM((tm, d), F32), pltpu.VMEM((tm, d), F32), pltpu.SemaphoreType.DMA((2,))],
        compiler_params=_cparams("arbitrary"),
        name="moe_combine",
    )(dest_tiles, x2, mods, info, final_g, ys)


def _moe(x2, g, mods, row, router_w, w1, w2, final_g, final_norm):
    t, d = x2.shape
    tm = ROW_TILE
    te = EXPERT_TILE
    rw = jnp.pad(router_w, ((0, 0), (0, LANES - N_EXPERTS)))
    h, info, counts = _router(x2, g, mods, row, rw)
    cnt = counts[0, :N_EXPERTS].astype(jnp.int32)
    gsz = ((cnt + te - 1) // te) * te
    ends = jnp.cumsum(gsz)
    offs = ends - gsz
    e1 = info[:, 0].astype(jnp.int32)
    e2 = info[:, 1].astype(jnp.int32)
    dest1 = offs[e1] + info[:, 4].astype(jnp.int32)
    dest2 = offs[e2] + info[:, 5].astype(jnp.int32)
    dest_tiles = jnp.concatenate([dest1.reshape(t // tm, 1, tm), dest2.reshape(t // tm, 1, tm)], axis=2)
    n_sorted = 2 * t + N_EXPERTS * te
    n_tiles_max = n_sorted // te
    starts = jnp.arange(n_tiles_max, dtype=jnp.int32) * te
    tile_expert = jnp.minimum(jnp.sum((starts[:, None] >= ends[None, :]).astype(jnp.int32), axis=1), N_EXPERTS - 1)
    n_used = (ends[-1] // te).astype(jnp.int32).reshape(1)
    hs = _dispatch(h, dest_tiles, n_sorted)
    ys = _experts(hs, tile_expert, n_used, w1, w2)
    return _combine(x2, mods, row, info, dest_tiles, ys, final_g, final_norm)


_ROPE_SWAP = tuple(list(range(8, 16)) + list(range(0, 8)) + list(range(24, 32)) + list(range(16, 24)))


def _even_weights(w_in, w_uq, w_ukv):
    d = w_in.shape[0]
    n0 = 3 * CONV_CH + Q_RANK + KV_RANK
    kr = w_in[:, n0:n0 + QK_ROPE]
    kr_sw = kr[:, jnp.array(_ROPE_SWAP)]
    z = lambda n: jnp.zeros((d, n), w_in.dtype)
    w_in_x = jnp.concatenate([w_in[:, :n0], z(QK_NOPE), kr, z(LANES - QK_NOPE - QK_ROPE),
                              z(QK_NOPE), kr_sw, z(LANES - QK_NOPE - QK_ROPE)], axis=1).astype(BF16)
    uq = w_uq.reshape(Q_RANK, MLA_HEADS, QK_NOPE + QK_ROPE)
    pad = LANES - QK_NOPE - QK_ROPE
    uq_plain = jnp.pad(uq, ((0, 0), (0, 0), (0, pad))).reshape(Q_RANK, MLA_HEADS * LANES)
    uq_rope_sw = uq[:, :, QK_NOPE:][:, :, jnp.array(_ROPE_SWAP)]
    uq_sw = jnp.pad(uq_rope_sw, ((0, 0), (0, 0), (QK_NOPE, pad))).reshape(Q_RANK, MLA_HEADS * LANES)
    wq = jnp.concatenate([uq_plain, uq_sw], axis=1).astype(BF16)
    ukv = w_ukv.reshape(KV_RANK, MLA_HEADS, QK_NOPE + V_HEAD)
    k_part = jnp.pad(ukv[:, :, :QK_NOPE], ((0, 0), (0, 0), (0, LANES - QK_NOPE))).reshape(KV_RANK, MLA_HEADS * LANES)
    v_part = ukv[:, :, QK_NOPE:].reshape(KV_RANK, MLA_HEADS * V_HEAD)
    wkv = jnp.concatenate([k_part, v_part], axis=1).astype(BF16)
    return w_in_x, wq, wkv


def _rope_tables(seq_len, ctx_rows):
    n = QK_ROPE // 4
    t = jnp.arange(seq_len)
    inv = ROPE_BASE ** (-jnp.arange(n, dtype=F32) / n)
    ang_r = (t // GRID_W).astype(F32)[:, None] * inv
    ang_c = (t % GRID_W).astype(F32)[:, None] * inv
    cr, sr, cc, sc = jnp.cos(ang_r), jnp.sin(ang_r), jnp.cos(ang_c), jnp.sin(ang_c)
    cos32 = jnp.concatenate([cr, cr, cc, cc], axis=1)
    sin32 = jnp.concatenate([-sr, sr, -sc, sc], axis=1)
    pad = LANES - QK_NOPE - QK_ROPE
    cos_x = jnp.concatenate([jnp.ones((seq_len, QK_NOPE), F32), cos32, jnp.zeros((seq_len, pad), F32)], axis=1)
    sin_x = jnp.concatenate([jnp.zeros((seq_len, QK_NOPE), F32), sin32, jnp.zeros((seq_len, pad), F32)], axis=1)
    cos_c = jnp.concatenate([jnp.ones((ctx_rows, QK_NOPE + QK_ROPE), F32), jnp.zeros((ctx_rows, pad), F32)], axis=1)
    sin_c = jnp.zeros((ctx_rows, LANES), F32)
    return jnp.concatenate([cos_c, cos_x], axis=0), jnp.concatenate([sin_c, sin_x], axis=0)


def _even_layer(xa, mods, dims, tabs, p):
    batch, ctx_len, seq_len = dims
    tm = ROW_TILE
    t = xa.shape[0]
    n_ctx_tiles = batch * ctx_len // tm
    x_tps = seq_len // tm
    row = _mod_row_fn(n_ctx_tiles, x_tps, batch)
    w_in_x, wq, wkv = _even_weights(p["w_in"], p["w_uq"], p["w_ukv"])
    cos_t, sin_t = tabs
    tab_idx = lambda i: jnp.where(i < n_ctx_tiles, 0, 1 + (i - n_ctx_tiles) % x_tps)
    conv_y, q, k, v = _even_proj(xa, p["norm1_g"].reshape(1, -1), mods, row, t // tm, w_in_x, p["conv_w"],
                                 p["q_norm_g"].reshape(1, -1), wq, p["kv_norm_g"].reshape(1, -1), wkv,
                                 cos_t, sin_t, tab_idx, n_ctx_tiles, ctx_len // tm, x_tps)
    att = _attention(q, k, v, batch, ctx_len, seq_len)
    w_out = p["w_out"].astype(BF16)
    xa = _resid_mm(xa, mods, 2, row, [conv_y, att], [w_out[:CONV_CH], w_out[CONV_CH:]])
    return _ffn(xa, p["norm2_g"].reshape(1, -1), mods, row, p["ffn_w1"].astype(BF16), p["ffn_w2"].astype(BF16))


def _odd_layer(xa, mods, dims, p, ctx_out, final_g, final_norm):
    batch, ctx_len, seq_len = dims
    tm = ROW_TILE
    t = xa.shape[0]
    n_ctx_tiles = batch * ctx_len // tm
    x_tps = seq_len // tm
    row = _mod_row_fn(n_ctx_tiles, x_tps, batch)
    n_w = p["w_in"].shape[1]
    w_in = jnp.pad(p["w_in"], ((0, 0), (0, 4 * GDN_QK + LANES - n_w))).astype(BF16)
    qkvn, z, abt = _odd_proj(xa, p["norm1_g"].reshape(1, -1), mods, row, w_in, p["qkv_conv_w"],
                             n_ctx_tiles, ctx_len // tm, x_tps)
    abt_chunks = abt[:, :4 * GDN_HEADS].reshape(t // GDN_STEP, GDN_STEP, 4 * GDN_HEADS).transpose(0, 2, 1)
    alog_b = jnp.broadcast_to(p["a_log"].reshape(2 * GDN_HEADS, 1), (2 * GDN_HEADS, GDN_STEP))
    dtb_b = jnp.broadcast_to(p["dt_bias"].reshape(2 * GDN_HEADS, 1), (2 * GDN_HEADS, GDN_STEP))
    o_dirs = _gdn(qkvn, abt_chunks, alog_b, dtb_b, batch, ctx_len, seq_len)
    row_off = 0 if ctx_out else n_ctx_tiles
    n_rows = t - row_off * tm
    xo = _odd_out(xa, mods, row, o_dirs, z, p["o_norm_g"].reshape(1, -1), p["w_out"].astype(BF16), row_off, n_rows)
    row_o = row if ctx_out else _mod_row_fn(0, x_tps, batch)
    return _moe(xo, p["norm2_g"].reshape(1, -1), mods, row_o, p["router_w"],
                p["moe_w1"].astype(BF16), p["moe_w2"].astype(BF16), final_g, final_norm)


_EVEN_NAMES = ("mod_w", "mod_b", "norm1_g", "w_in", "conv_w", "q_norm_g", "w_uq", "kv_norm_g", "w_ukv", "w_out",
               "norm2_g", "ffn_w1", "ffn_w2")
_ODD_NAMES = ("mod_w", "mod_b", "norm1_g", "w_in", "qkv_conv_w", "a_log", "dt_bias", "o_norm_g", "w_out",
              "norm2_g", "router_w", "moe_w1", "moe_w2")


def kernel(x, c, ctx, c_ctx, l0_mod_w, l0_mod_b, l0_norm1_g, l0_w_in, l0_conv_w, l0_q_norm_g, l0_w_uq, l0_kv_norm_g, l0_w_ukv, l0_w_out, l0_norm2_g, l0_ffn_w1, l0_ffn_w2, l1_mod_w, l1_mod_b, l1_norm1_g, l1_w_in, l1_qkv_conv_w, l1_a_log, l1_dt_bias, l1_o_norm_g, l1_w_out, l1_norm2_g, l1_router_w, l1_moe_w1, l1_moe_w2, l2_mod_w, l2_mod_b, l2_norm1_g, l2_w_in, l2_conv_w, l2_q_norm_g, l2_w_uq, l2_kv_norm_g, l2_w_ukv, l2_w_out, l2_norm2_g, l2_ffn_w1, l2_ffn_w2, l3_mod_w, l3_mod_b, l3_norm1_g, l3_w_in, l3_qkv_conv_w, l3_a_log, l3_dt_bias, l3_o_norm_g, l3_w_out, l3_norm2_g, l3_router_w, l3_moe_w1, l3_moe_w2, final_norm_g):
    batch, seq_len, d = x.shape
    ctx_len = ctx.shape[1]
    assert seq_len % ROW_TILE == 0 and ctx_len % ROW_TILE == 0 and seq_len % GRID_W == 0
    dims = (batch, ctx_len, seq_len)
    layers = (
        dict(zip(_EVEN_NAMES, (l0_mod_w, l0_mod_b, l0_norm1_g, l0_w_in, l0_conv_w, l0_q_norm_g, l0_w_uq,
                               l0_kv_norm_g, l0_w_ukv, l0_w_out, l0_norm2_g, l0_ffn_w1, l0_ffn_w2))),
        dict(zip(_ODD_NAMES, (l1_mod_w, l1_mod_b, l1_norm1_g, l1_w_in, l1_qkv_conv_w, l1_a_log, l1_dt_bias,
                              l1_o_norm_g, l1_w_out, l1_norm2_g, l1_router_w, l1_moe_w1, l1_moe_w2))),
        dict(zip(_EVEN_NAMES, (l2_mod_w, l2_mod_b, l2_norm1_g, l2_w_in, l2_conv_w, l2_q_norm_g, l2_w_uq,
                               l2_kv_norm_g, l2_w_ukv, l2_w_out, l2_norm2_g, l2_ffn_w1, l2_ffn_w2))),
        dict(zip(_ODD_NAMES, (l3_mod_w, l3_mod_b, l3_norm1_g, l3_w_in, l3_qkv_conv_w, l3_a_log, l3_dt_bias,
                              l3_o_norm_g, l3_w_out, l3_norm2_g, l3_router_w, l3_moe_w1, l3_moe_w2))),
    )
    xa = jnp.concatenate([ctx.reshape(batch * ctx_len, d), x.reshape(batch * seq_len, d)], axis=0)
    mod_rows = 8 * ((batch + 1 + 7) // 8)
    cc = jnp.zeros((mod_rows, d), F32).at[:batch].set(c).at[batch].set(c_ctx)
    tabs = _rope_tables(seq_len, ROW_TILE)
    n_layers = len(layers)
    assert n_layers % 2 == 0
    final_g = final_norm_g.reshape(1, -1)
    for li, p in enumerate(layers):
        mods = _mods(cc, p["mod_w"], p["mod_b"])
        last = li == n_layers - 1
        if li % 2 == 0:
            xa = _even_layer(xa, mods, dims, tabs, p)
        else:
            xa = _odd_layer(xa, mods, dims, p, not last, final_g, last)
    return xa.reshape(batch, seq_len, d)
```

```python
import functools
import math

import jax
import jax.numpy as jnp
from jax import lax
from jax.experimental import pallas as pl
from jax.experimental.pallas import tpu as pltpu

F32 = jnp.float32
BF16 = jnp.bfloat16

NORM_EPS = 1e-6
N_MOD = 6
GRID_W = 64
ROPE_BASE = 10000.0

CONV_CH = 512
MLA_HEADS = 8
Q_RANK = 256
KV_RANK = 128
QK_NOPE = 64
QK_ROPE = 32
V_HEAD = 64
MLA_SCALE = (QK_NOPE + QK_ROPE) ** -0.5

GDN_HEADS = 8
GDN_DK = 128
GDN_CHUNK = 64
GDN_QK = GDN_HEADS * GDN_DK

N_EXPERTS = 8

LANES = 128
ROW_TILE = 256
ATTN_HEAD_GROUP = 4
GDN_STEP = 256
GDN_HEAD_GROUP = 8
ROW_DMA_UNROLL = 8
EXPERT_TILE = 512
EXPERT_FCHUNK = 512
VMEM_LIMIT_BYTES = 56 * 1024 * 1024
EXPERT_VMEM_LIMIT_BYTES = 62 * 1024 * 1024


def _cparams(*sem):
    return pltpu.CompilerParams(dimension_semantics=sem, vmem_limit_bytes=VMEM_LIMIT_BYTES)


def _silu(x):
    return x * (1.0 / (1.0 + jnp.exp(-x)))


def _rms(x, g):
    return x * lax.rsqrt(jnp.mean(x * x, axis=-1, keepdims=True) + NORM_EPS) * g


def _dot(a, b):
    return jnp.dot(a, b, preferred_element_type=F32)


def _dot_nt(a, b):
    return lax.dot_general(a, b, (((1,), (1,)), ((), ())), preferred_element_type=F32)


def _dot_tn(a, b):
    return lax.dot_general(a, b, (((0,), (0,)), ((), ())), preferred_element_type=F32)


def _split3(x):
    x1 = x.astype(BF16)
    r1 = x - x1.astype(F32)
    x2 = r1.astype(BF16)
    x3 = (r1 - x2.astype(F32)).astype(BF16)
    return x1, x2, x3


def _dot_f32(a, b):
    a1, a2, a3 = _split3(a)
    b1, b2, b3 = _split3(b)
    acc = _dot(a1, b3) + _dot(a2, b2) + _dot(a3, b1)
    acc = acc + _dot(a1, b2) + _dot(a2, b1)
    return acc + _dot(a1, b1)


def _dot_f32_exact_rhs(a, m_bf16):
    a1, a2, a3 = _split3(a)
    return _dot(a3, m_bf16) + _dot(a2, m_bf16) + _dot(a1, m_bf16)


def _mod_row_fn(n_ctx_tiles, tiles_per_seq, batch):
    def row(i):
        return jnp.where(i < n_ctx_tiles, batch, (i - n_ctx_tiles) // tiles_per_seq)
    return row


def _mods_kernel(c_ref, w_ref, b_ref, o_ref):
    o_ref[...] = _dot_f32(_silu(c_ref[...]), w_ref[...]) + b_ref[...]


def _mods(cc, mod_w, mod_b):
    rows, d = cc.shape
    n = mod_w.shape[1]
    out = pl.pallas_call(
        _mods_kernel,
        grid=(n // d,),
        in_specs=[pl.BlockSpec((rows, d), lambda j: (0, 0)),
                  pl.BlockSpec((d, d), lambda j: (0, j)),
                  pl.BlockSpec((1, d), lambda j: (0, j))],
        out_specs=pl.BlockSpec((rows, d), lambda j: (0, j)),
        out_shape=jax.ShapeDtypeStruct((rows, n), F32),
        compiler_params=_cparams("arbitrary"),
        name="adaln_mods",
    )(cc, mod_w, mod_b.reshape(1, n))
    return out.reshape(rows, 1, n)


def _mod_spec(d, k, row, off=0):
    return pl.BlockSpec((1, 1, d), lambda i: (row(i + off), 0, k))


def _even_proj_kernel(x_ref, xp_ref, xn_ref, g_ref, sh_ref, sc_ref, win_ref, cw_ref, qg_ref, wq_ref, kvg_ref, wkv_ref,
                      cos_ref, sin_ref, conv_ref, q_ref, k_ref, v_ref, *, n_ctx_tiles, ctx_tps, x_tps):
    i = pl.program_id(0)
    keep_prev, keep_next = _seq_edges(i, n_ctx_tiles, ctx_tps, x_tps)
    tm = x_ref.shape[0]
    xs = jnp.concatenate([x_ref[...], xp_ref[...], xn_ref[...]], axis=0)
    hb_all = (_rms(xs, g_ref[...]) * (1.0 + sc_ref[0]) + sh_ref[0]).astype(BF16)
    hb = hb_all[:tm]
    c = CONV_CH
    n_gbu = 3 * c
    z_all = _dot(hb_all, win_ref[:, c:2 * c]) * _dot(hb_all, win_ref[:, 2 * c:n_gbu])
    z = z_all[:tm]
    zm, zp = _shifted(z, z_all[tm + X_HALO - 1:tm + X_HALO] * keep_prev, z_all[tm + X_HALO:tm + X_HALO + 1] * keep_next)
    cw = cw_ref[...]
    conv = zm * cw[0:1] + z * cw[1:2] + zp * cw[2:3]
    conv_ref[...] = (_dot(hb, win_ref[:, :c]) * conv).astype(conv_ref.dtype)
    rest = _dot(hb, win_ref[:, n_gbu:])
    cq = rest[:, :Q_RANK]
    ckv = rest[:, Q_RANK:Q_RANK + KV_RANK]
    kr = rest[:, Q_RANK + KV_RANK:Q_RANK + KV_RANK + LANES]
    kr_sw = rest[:, Q_RANK + KV_RANK + LANES:]
    cos = cos_ref[...]
    sin = sin_ref[...]
    kr_rot = kr * cos + kr_sw * sin
    qq = _dot(_rms(cq, qg_ref[...]).astype(BF16), wq_ref[...])
    kv = _dot(_rms(ckv, kvg_ref[...]).astype(BF16), wkv_ref[...])
    hw = MLA_HEADS * LANES
    for h in range(MLA_HEADS):
        sl = slice(h * LANES, (h + 1) * LANES)
        qh = (qq[:, sl] * cos + qq[:, hw + h * LANES:hw + (h + 1) * LANES] * sin) * MLA_SCALE
        q_ref[:, sl] = qh.astype(q_ref.dtype)
        k_ref[:, sl] = (kv[:, sl] + kr_rot).astype(k_ref.dtype)
    v_ref[...] = kv[:, hw:].astype(v_ref.dtype)


def _even_proj(x2, g, mods, row, n_tiles, w_in, conv_w, qg, wq, kvg, wkv, cos_t, sin_t, tab_idx,
               n_ctx_tiles, ctx_tps, x_tps):
    t, d = x2.shape
    tm = ROW_TILE
    n_in = w_in.shape[1]
    hw = MLA_HEADS * LANES
    const = lambda i: (0, 0)
    r = tm // X_HALO
    nb = t // X_HALO
    return pl.pallas_call(
        functools.partial(_even_proj_kernel, n_ctx_tiles=n_ctx_tiles, ctx_tps=ctx_tps, x_tps=x_tps),
        grid=(n_tiles,),
        in_specs=[pl.BlockSpec((tm, d), lambda i: (i, 0)),
                  pl.BlockSpec((X_HALO, d), lambda i: (jnp.maximum(i * r - 1, 0), 0)),
                  pl.BlockSpec((X_HALO, d), lambda i: (jnp.minimum((i + 1) * r, nb - 1), 0)),
                  pl.BlockSpec((1, d), const),
                  _mod_spec(d, 0, row), _mod_spec(d, 1, row),
                  pl.BlockSpec((d, n_in), const),
                  pl.BlockSpec(conv_w.shape, const),
                  pl.BlockSpec((1, Q_RANK), const),
                  pl.BlockSpec(wq.shape, const),
                  pl.BlockSpec((1, KV_RANK), const),
                  pl.BlockSpec(wkv.shape, const),
                  pl.BlockSpec((tm, LANES), lambda i: (tab_idx(i), 0)),
                  pl.BlockSpec((tm, LANES), lambda i: (tab_idx(i), 0))],
        out_specs=[pl.BlockSpec((tm, CONV_CH), lambda i: (i, 0)),
                   pl.BlockSpec((tm, hw), lambda i: (i, 0)),
                   pl.BlockSpec((tm, hw), lambda i: (i, 0)),
                   pl.BlockSpec((tm, MLA_HEADS * V_HEAD), lambda i: (i, 0))],
        out_shape=[jax.ShapeDtypeStruct((t, CONV_CH), BF16),
                   jax.ShapeDtypeStruct((t, hw), BF16),
                   jax.ShapeDtypeStruct((t, hw), BF16),
                   jax.ShapeDtypeStruct((t, MLA_HEADS * V_HEAD), BF16)],
        compiler_params=_cparams("arbitrary"),
        name="even_proj",
    )(x2, x2, x2, g, mods, mods, w_in, conv_w, qg, wq, kvg, wkv, cos_t, sin_t)


def _shifted(z, prev_row, next_row):
    tm = z.shape[0]
    ridx = lax.broadcasted_iota(jnp.int32, z.shape, 0)
    zm = jnp.where(ridx == 0, prev_row, pltpu.roll(z, 1, 0))
    zp = jnp.where(ridx == tm - 1, next_row, pltpu.roll(z, tm - 1, 0))
    return zm, zp


def _seq_edges(i, n_ctx_tiles, ctx_tps, x_tps):
    j = jnp.where(i < n_ctx_tiles, i % ctx_tps, (i - n_ctx_tiles) % x_tps)
    tps = jnp.where(i < n_ctx_tiles, ctx_tps, x_tps)
    keep_prev = (j != 0).astype(F32)
    keep_next = (j != tps - 1).astype(F32)
    return keep_prev, keep_next


def _attn_kernel(q_ref, kc_ref, kx_ref, vc_ref, vx_ref, o_ref, *, n_ctx_q):
    qi = pl.program_id(1)

    def run(use_x):
        for g0 in range(0, MLA_HEADS, ATTN_HEAD_GROUP):
            hs = range(g0, g0 + ATTN_HEAD_GROUP)
            ks = {h: slice(h * LANES, (h + 1) * LANES) for h in hs}
            vs = {h: slice(h * V_HEAD, (h + 1) * V_HEAD) for h in hs}
            q = {h: q_ref[:, ks[h]] for h in hs}
            sc = {h: _dot_nt(q[h], kc_ref[:, ks[h]]) for h in hs}
            m = {h: jnp.max(sc[h], axis=1, keepdims=True) for h in hs}
            if use_x:
                sx = {h: _dot_nt(q[h], kx_ref[:, ks[h]]) for h in hs}
                m = {h: jnp.maximum(m[h], jnp.max(sx[h], axis=1, keepdims=True)) for h in hs}
            pc = {h: jnp.exp(sc[h] - m[h]) for h in hs}
            l = {h: jnp.sum(pc[h], axis=1, keepdims=True) for h in hs}
            o = {h: _dot(pc[h].astype(BF16), vc_ref[:, vs[h]]) for h in hs}
            if use_x:
                px = {h: jnp.exp(sx[h] - m[h]) for h in hs}
                l = {h: l[h] + jnp.sum(px[h], axis=1, keepdims=True) for h in hs}
                o = {h: o[h] + _dot(px[h].astype(BF16), vx_ref[:, vs[h]]) for h in hs}
            outs = [o[h] * (1.0 / l[h]) for h in hs]
            for j in range(0, ATTN_HEAD_GROUP, 2):
                c0 = (g0 + j) * V_HEAD
                o_ref[:, c0:c0 + 2 * V_HEAD] = jnp.concatenate(outs[j:j + 2], axis=1).astype(o_ref.dtype)

    @pl.when(qi < n_ctx_q)
    def _():
        run(False)

    @pl.when(qi >= n_ctx_q)
    def _():
        run(True)


def _attention(q, k, v, batch, ctx_len, seq_len):
    tq = ROW_TILE
    n_ctx_q = ctx_len // tq
    n_x_q = seq_len // tq
    ctx_blocks = batch * ctx_len // tq
    assert (batch * ctx_len) % seq_len == 0
    x_blk0 = batch * ctx_len // seq_len
    hw = MLA_HEADS * LANES
    vw = MLA_HEADS * V_HEAD

    def qrow(b, qi):
        return jnp.where(qi < n_ctx_q, b * n_ctx_q + qi, ctx_blocks + b * n_x_q + (qi - n_ctx_q))

    return pl.pallas_call(
        functools.partial(_attn_kernel, n_ctx_q=n_ctx_q),
        grid=(batch, n_ctx_q + n_x_q),
        in_specs=[pl.BlockSpec((tq, hw), lambda b, qi: (qrow(b, qi), 0)),
                  pl.BlockSpec((ctx_len, hw), lambda b, qi: (b, 0)),
                  pl.BlockSpec((seq_len, hw), lambda b, qi: (x_blk0 + b, 0)),
                  pl.BlockSpec((ctx_len, vw), lambda b, qi: (b, 0)),
                  pl.BlockSpec((seq_len, vw), lambda b, qi: (x_blk0 + b, 0))],
        out_specs=pl.BlockSpec((tq, vw), lambda b, qi: (qrow(b, qi), 0)),
        out_shape=jax.ShapeDtypeStruct((q.shape[0], vw), BF16),
        compiler_params=_cparams("arbitrary", "arbitrary"),
        name="mla_attention",
    )(q, k, k, v, v)


def _resid_mm_kernel(*refs, n_lhs):
    x_ref, gate_ref = refs[0], refs[1]
    lhs = refs[2:2 + n_lhs]
    ws = refs[2 + n_lhs:2 + 2 * n_lhs]
    o_ref = refs[2 + 2 * n_lhs]
    acc = _dot(lhs[0][...], ws[0][...])
    for a_ref, w_ref in zip(lhs[1:], ws[1:]):
        acc = acc + _dot(a_ref[...], w_ref[...])
    o_ref[...] = x_ref[...] + gate_ref[0] * acc


def _resid_mm(x2, mods, k_gate, row, lhs_list, w_list, tm=512):
    t, d = x2.shape
    n = len(lhs_list)
    in_specs = [pl.BlockSpec((tm, d), lambda i: (i, 0)), _mod_spec(d, k_gate, lambda i: row(i * tm // ROW_TILE))]
    in_specs += [pl.BlockSpec((tm, a.shape[1]), lambda i: (i, 0)) for a in lhs_list]
    in_specs += [pl.BlockSpec(w.shape, lambda i: (0, 0)) for w in w_list]
    return pl.pallas_call(
        functools.partial(_resid_mm_kernel, n_lhs=n),
        grid=(t // tm,),
        in_specs=in_specs,
        out_specs=pl.BlockSpec((tm, d), lambda i: (i, 0)),
        out_shape=jax.ShapeDtypeStruct((t, d), F32),
        compiler_params=_cparams("arbitrary"),
        name="resid_matmul",
    )(x2, mods, *lhs_list, *w_list)


def _ffn_up_kernel(x_ref, g_ref, sh_ref, sc_ref, w_ref, o_ref, *, ffn, chunk):
    hn = _rms(x_ref[...], g_ref[...]) * (1.0 + sc_ref[0]) + sh_ref[0]
    hb = hn.astype(BF16)
    for c0 in range(0, ffn, chunk):
        gate = _dot(hb, w_ref[:, c0:c0 + chunk])
        up = _dot(hb, w_ref[:, ffn + c0:ffn + c0 + chunk])
        o_ref[:, c0:c0 + chunk] = (_silu(gate) * up).astype(o_ref.dtype)


def _ffn_up(x2, g, mods, row, w1):
    t, d = x2.shape
    tm = ROW_TILE
    ffn = w1.shape[1] // 2
    chunk = 256
    assert ffn % chunk == 0
    return pl.pallas_call(
        functools.partial(_ffn_up_kernel, ffn=ffn, chunk=chunk),
        grid=(t // tm,),
        in_specs=[pl.BlockSpec((tm, d), lambda i: (i, 0)),
                  pl.BlockSpec((1, d), lambda i: (0, 0)),
                  _mod_spec(d, 3, row), _mod_spec(d, 4, row),
                  pl.BlockSpec(w1.shape, lambda i: (0, 0))],
        out_specs=pl.BlockSpec((tm, ffn), lambda i: (i, 0)),
        out_shape=jax.ShapeDtypeStruct((t, ffn), BF16),
        compiler_params=_cparams("arbitrary"),
        name="ffn_up",
    )(x2, g, mods, mods, w1)


X_HALO = 8


def _odd_proj_kernel(x_ref, xp_ref, xn_ref, g_ref, sh_ref, sc_ref, w_ref, cw_ref, qkv_ref, gate_ref, abt_ref,
                     *, chunk, n_ctx_tiles, ctx_tps, x_tps):
    i = pl.program_id(0)
    keep_prev, keep_next = _seq_edges(i, n_ctx_tiles, ctx_tps, x_tps)
    tm = x_ref.shape[0]
    xs = jnp.concatenate([x_ref[...], xp_ref[...], xn_ref[...]], axis=0)
    hb = (_rms(xs, g_ref[...]) * (1.0 + sc_ref[0]) + sh_ref[0]).astype(BF16)
    n_qkv = qkv_ref.shape[0] * LANES
    n_gate = gate_ref.shape[1]
    cw = cw_ref[...]
    for c0 in range(0, n_qkv, chunk):
        z_all = _dot(hb, w_ref[:, c0:c0 + chunk])
        z = z_all[:tm]
        zprev = z_all[tm + X_HALO - 1:tm + X_HALO] * keep_prev
        znext = z_all[tm + X_HALO:tm + X_HALO + 1] * keep_next
        zm, zp = _shifted(z, zprev, znext)
        y = _silu(zm * cw[0:1, c0:c0 + chunk] + z * cw[1:2, c0:c0 + chunk] + zp * cw[2:3, c0:c0 + chunk])
        for hh in range(chunk // LANES):
            head = c0 // LANES + hh
            yh = y[:, hh * LANES:(hh + 1) * LANES]
            if head < 2 * GDN_HEADS:
                nrm = lax.rsqrt(jnp.sum(yh * yh, axis=-1, keepdims=True) + NORM_EPS)
                if head < GDN_HEADS:
                    nrm = nrm * (GDN_DK ** -0.5)
                yh = yh * nrm
            qkv_ref[head] = yh
    hb_cur = hb[:tm]
    for c0 in range(0, n_gate, chunk):
        gate_ref[:, c0:c0 + chunk] = _dot(hb_cur, w_ref[:, n_qkv + c0:n_qkv + c0 + chunk]).astype(gate_ref.dtype)
    abt_ref[...] = _dot(hb_cur, w_ref[:, n_qkv + n_gate:])


def _odd_proj(x2, g, mods, row, w_in, conv_w, n_ctx_tiles, ctx_tps, x_tps):
    t, d = x2.shape
    tm = ROW_TILE
    n_qkv = 3 * GDN_QK
    n_gate = GDN_QK
    r = tm // X_HALO
    nb = t // X_HALO
    return pl.pallas_call(
        functools.partial(_odd_proj_kernel, chunk=512, n_ctx_tiles=n_ctx_tiles, ctx_tps=ctx_tps, x_tps=x_tps),
        grid=(t // tm,),
        in_specs=[pl.BlockSpec((tm, d), lambda i: (i, 0)),
                  pl.BlockSpec((X_HALO, d), lambda i: (jnp.maximum(i * r - 1, 0), 0)),
                  pl.BlockSpec((X_HALO, d), lambda i: (jnp.minimum((i + 1) * r, nb - 1), 0)),
                  pl.BlockSpec((1, d), lambda i: (0, 0)),
                  _mod_spec(d, 0, row), _mod_spec(d, 1, row),
                  pl.BlockSpec(w_in.shape, lambda i: (0, 0)),
                  pl.BlockSpec(conv_w.shape, lambda i: (0, 0))],
        out_specs=[pl.BlockSpec((n_qkv // LANES, tm, LANES), lambda i: (0, i, 0)),
                   pl.BlockSpec((tm, n_gate), lambda i: (i, 0)),
                   pl.BlockSpec((tm, LANES), lambda i: (i, 0))],
        out_shape=[jax.ShapeDtypeStruct((n_qkv // LANES, t, LANES), F32),
                   jax.ShapeDtypeStruct((t, n_gate), BF16),
                   jax.ShapeDtypeStruct((t, LANES), F32)],
        compiler_params=_cparams("arbitrary"),
        name="odd_proj",
    )(x2, x2, x2, g, mods, mods, w_in, conv_w)


def _gdn_kernel(qkv_ref, abt_ref, alog_ref, dtb_ref, o_ref,
                s_ref, gc_s, beta_s, gtot_s, uw_s, qg_s, kd_s, attn_s, vnew_s, gtc_s):
    d = pl.program_id(1)
    step = pl.program_id(2)
    n = GDN_STEP
    c = GDN_CHUNK
    nchunk = n // c
    h_n = GDN_HEADS
    sgn = 1 - 2 * d

    @pl.when(step == 0)
    def _():
        s_ref[...] = jnp.zeros(s_ref.shape, s_ref.dtype)

    ri = lax.broadcasted_iota(jnp.int32, (n, n), 0)
    ci = lax.broadcasted_iota(jnp.int32, (n, n), 1)
    same = (ri // c) == (ci // c)
    rel = (ri - ci) * sgn
    incl = same & (rel >= 0)
    strict = same & (rel > 0)
    eye = (ri == ci).astype(F32)

    hrow = pl.ds(d * h_n, h_n)
    a_all = abt_ref[0, hrow, :]
    bt_all = abt_ref[0, pl.ds(2 * h_n + d * h_n, h_n), :]
    sp_in = a_all + dtb_ref[hrow, :]
    softplus = jnp.maximum(sp_in, 0.0) + jnp.log(1.0 + jnp.exp(-jnp.abs(sp_in)))
    g_all = -jnp.exp(alog_ref[hrow, :]) * softplus
    beta_s[...] = 1.0 / (1.0 + jnp.exp(-bt_all))
    cum_m = jnp.where(same & ((ci - ri) * sgn >= 0), 1.0, 0.0).astype(BF16)
    gc_s[...] = _dot_f32_exact_rhs(g_all, cum_m)
    gtot_s[...] = _dot_f32_exact_rhs(g_all, same.astype(BF16))

    grp = uw_s.shape[0]
    n_sq = (c - 1).bit_length() - 1

    def head_group(gi, carry):
        heads = [gi * grp + u for u in range(grp)]
        us = range(grp)
        q = [qkv_ref[h] for h in heads]
        k = [qkv_ref[h_n + h] for h in heads]
        v = [qkv_ref[2 * h_n + h] for h in heads]
        gc_row = [gc_s[pl.ds(h, 1), :] for h in heads]
        gc_col = [jnp.sum(eye * gc_row[u], axis=1, keepdims=True) for u in us]
        beta_col = [jnp.sum(eye * beta_s[pl.ds(h, 1), :], axis=1, keepdims=True) for h in heads]
        gtot_col = [jnp.sum(eye * gtot_s[pl.ds(h, 1), :], axis=1, keepdims=True) for h in heads]
        decay = [jnp.exp(jnp.where(incl, gc_col[u] - gc_row[u], -1e30)) for u in us]
        kb = [k[u] * beta_col[u] for u in us]
        k16 = [k[u].astype(BF16) for u in us]
        p = [jnp.where(strict, -(_dot_nt(kb[u].astype(BF16), k16[u]) * decay[u]), 0.0) for u in us]
        for u in us:
            attn_s[u] = (_dot_nt(q[u].astype(BF16), k16[u]) * decay[u]).astype(attn_s.dtype)
            gtc_s[u] = jnp.broadcast_to(gtot_col[u], gtc_s.shape[1:])
        tinv = [eye + p[u] for u in us]
        p16 = [p[u].astype(BF16) for u in us]
        p = [_dot(p16[u], p16[u]) for u in us]
        for _ in range(n_sq - 1):
            p16 = [p[u].astype(BF16) for u in us]
            tp = [_dot(jnp.concatenate([tinv[u].astype(BF16), p16[u]], axis=0), p16[u]) for u in us]
            tinv = [tinv[u] + tp[u][:n] for u in us]
            p = [tp[u][n:] for u in us]
        tinv = [tinv[u] + _dot(tinv[u].astype(BF16), p[u].astype(BF16)) for u in us]
        egc = [jnp.exp(gc_col[u]) for u in us]
        for u in us:
            rhs = jnp.concatenate([v[u] * beta_col[u], kb[u] * egc[u]], axis=1).astype(BF16)
            uw_s[u] = _dot(tinv[u].astype(BF16), rhs)
            qg_s[u] = q[u] * egc[u]
            kd_s[u] = k[u] * jnp.exp(gtot_col[u] - gc_col[u])
            vnew_s[u] = jnp.zeros(vnew_s.shape[1:], vnew_s.dtype)
        s = [s_ref[h] for h in heads]
        for cstep in range(nchunk):
            cc = jnp.where(d == 0, cstep, nchunk - 1 - cstep)
            r0 = pl.multiple_of(cc * c, c)
            rows = pl.ds(r0, c)
            ws_qs = [_dot(jnp.concatenate([uw_s[u, rows, LANES:], qg_s[u, rows, :]], axis=0).astype(BF16),
                          s[u].astype(BF16)) for u in us]
            v_new = [uw_s[u, rows, :LANES] - ws_qs[u][:c] for u in us]
            for u in us:
                vnew_s[u, rows, :] = v_new[u].astype(vnew_s.dtype)
            for u in us:
                o_ref[0, heads[u], rows, :] = ws_qs[u][c:] + _dot(attn_s[u, rows, :], vnew_s[u])
            s = [s[u] * jnp.exp(gtc_s[u, pl.ds(r0, 1), :])
                 + _dot_tn(kd_s[u, rows, :].astype(BF16), v_new[u].astype(BF16)) for u in us]
        for u in us:
            s_ref[heads[u]] = s[u]
        return carry

    lax.fori_loop(0, h_n // grp, head_group, 0)


def _gdn(qkvn, abt_chunks, alog_b, dtb_b, batch, ctx_len, seq_len):
    nh3, t, _ = qkvn.shape
    n = GDN_STEP
    nc = ctx_len // n
    nx = seq_len // n
    ctx_blocks = batch * nc

    def rowblk(b, d, s):
        sc = jnp.where(d == 0, s, nc - 1 - s)
        sx = jnp.where(d == 0, s - nc, nx - 1 - (s - nc))
        return jnp.where(s < nc, b * nc + sc, ctx_blocks + b * nx + sx)

    vm = pltpu.VMEM
    g = GDN_HEAD_GROUP
    return pl.pallas_call(
        _gdn_kernel,
        grid=(batch, 2, nc + nx),
        in_specs=[pl.BlockSpec((nh3, n, LANES), lambda b, d, s: (0, rowblk(b, d, s), 0)),
                  pl.BlockSpec((1, 4 * GDN_HEADS, n), lambda b, d, s: (rowblk(b, d, s), 0, 0)),
                  pl.BlockSpec(alog_b.shape, lambda b, d, s: (0, 0)),
                  pl.BlockSpec(dtb_b.shape, lambda b, d, s: (0, 0))],
        out_specs=pl.BlockSpec((1, GDN_HEADS, n, LANES), lambda b, d, s: (d, 0, rowblk(b, d, s), 0)),
        out_shape=jax.ShapeDtypeStruct((2, GDN_HEADS, t, LANES), F32),
        scratch_shapes=[vm((GDN_HEADS, GDN_DK, LANES), F32),
                        vm((GDN_HEADS, n), F32),
                        vm((GDN_HEADS, n), F32),
                        vm((GDN_HEADS, n), F32),
                        vm((g, n, 2 * LANES), F32),
                        vm((g, n, LANES), F32),
                        vm((g, n, LANES), F32),
                        vm((g, n, n), BF16),
                        vm((g, n, LANES), BF16),
                        vm((g, n, LANES), F32)],
        compiler_params=_cparams("arbitrary", "arbitrary", "arbitrary"),
        name="gated_delta",
    )(qkvn, abt_chunks, alog_b, dtb_b)


def _odd_out_kernel(x_ref, gate_ref, o_ref_in, z_ref, og_ref, w_ref, g2_ref, sh_ref, sc_ref, rw_ref,
                    out_ref, h_ref, info_ref, cnt_ref, carry_s):
    parts = []
    og = og_ref[...]
    for h in range(GDN_HEADS):
        o = o_ref_in[0, h] + o_ref_in[1, h]
        y = _rms(o, og) * _silu(z_ref[:, h * LANES:(h + 1) * LANES].astype(F32))
        parts.append(y.astype(BF16))
    y_all = jnp.concatenate(parts, axis=1)
    x_new = x_ref[...] + gate_ref[0] * _dot(y_all, w_ref[...])
    out_ref[...] = x_new
    hn = _rms(x_new, g2_ref[...]) * (1.0 + sc_ref[0]) + sh_ref[0]
    h_ref[...] = hn
    _route(hn, rw_ref, info_ref, cnt_ref, carry_s)


def _odd_out(x2, mods, row, o_dirs, z, o_norm_g, w_out, norm2_g, router_w_pad, row_off, n_rows):
    d = x2.shape[1]
    tm = ROW_TILE
    return pl.pallas_call(
        _odd_out_kernel,
        grid=(n_rows // tm,),
        in_specs=[pl.BlockSpec((tm, d), lambda i: (i + row_off, 0)),
                  _mod_spec(d, 2, row, row_off),
                  pl.BlockSpec((2, GDN_HEADS, tm, LANES), lambda i: (0, 0, i + row_off, 0)),
                  pl.BlockSpec((tm, z.shape[1]), lambda i: (i + row_off, 0)),
                  pl.BlockSpec((1, LANES), lambda i: (0, 0)),
                  pl.BlockSpec(w_out.shape, lambda i: (0, 0)),
                  pl.BlockSpec((1, d), lambda i: (0, 0)),
                  _mod_spec(d, 3, row, row_off), _mod_spec(d, 4, row, row_off),
                  pl.BlockSpec(router_w_pad.shape, lambda i: (0, 0))],
        out_specs=[pl.BlockSpec((tm, d), lambda i: (i, 0)),
                   pl.BlockSpec((tm, d), lambda i: (i, 0)),
                   pl.BlockSpec((tm, LANES), lambda i: (i, 0)),
                   pl.BlockSpec((8, LANES), lambda i: (0, 0))],
        out_shape=[jax.ShapeDtypeStruct((n_rows, d), F32),
                   jax.ShapeDtypeStruct((n_rows, d), F32),
                   jax.ShapeDtypeStruct((n_rows, LANES), F32),
                   jax.ShapeDtypeStruct((8, LANES), F32)],
        scratch_shapes=[pltpu.VMEM((8, LANES), F32)],
        compiler_params=_cparams("arbitrary"),
        name="odd_out_router",
    )(x2, mods, o_dirs, z, o_norm_g, w_out, norm2_g, mods, mods, router_w_pad)


def _route(hn, rw_ref, info_ref, cnt_ref, carry_s):
    i = pl.program_id(0)

    @pl.when(i == 0)
    def _():
        carry_s[...] = jnp.zeros(carry_s.shape, carry_s.dtype)

    tm = hn.shape[0]
    lane = lax.broadcasted_iota(jnp.int32, (tm, LANES), 1)
    logits = jnp.where(lane < N_EXPERTS, _dot_f32(hn, rw_ref[...]), -jnp.inf)
    m1 = jnp.max(logits, axis=1, keepdims=True)
    i1 = jnp.min(jnp.where(logits == m1, lane, LANES), axis=1, keepdims=True)
    rest = jnp.where(lane == i1, -jnp.inf, logits)
    m2 = jnp.max(rest, axis=1, keepdims=True)
    i2 = jnp.min(jnp.where(rest == m2, lane, LANES), axis=1, keepdims=True)
    e2 = jnp.exp(m2 - m1)
    w1 = 1.0 / (1.0 + e2)
    w2 = e2 / (1.0 + e2)
    oh1 = (lane == i1).astype(F32)
    oh2 = (lane == i2).astype(F32)
    cnt = oh1 + oh2
    ri = lax.broadcasted_iota(jnp.int32, (tm, tm), 0)
    ci = lax.broadcasted_iota(jnp.int32, (tm, tm), 1)
    before = (ci < ri).astype(BF16)
    pre = _dot(before, cnt.astype(BF16)) + carry_s[0:1, :]
    r1 = jnp.sum(pre * oh1, axis=1, keepdims=True)
    r2 = jnp.sum(pre * oh2, axis=1, keepdims=True)
    info = jnp.where(lane == 0, i1.astype(F32), 0.0)
    info = jnp.where(lane == 1, i2.astype(F32), info)
    info = jnp.where(lane == 2, w1, info)
    info = jnp.where(lane == 3, w2, info)
    info = jnp.where(lane == 4, r1, info)
    info = jnp.where(lane == 5, r2, info)
    info_ref[...] = info
    total = carry_s[0:1, :] + jnp.sum(cnt, axis=0, keepdims=True)
    carry_s[...] = jnp.broadcast_to(total, carry_s.shape)
    cnt_ref[...] = jnp.broadcast_to(total, cnt_ref.shape)


def _row_copy(src_hbm, src_row, dst_ref, dst_row, sem):
    return pltpu.make_async_copy(src_hbm.at[pl.ds(src_row, 1)], dst_ref.at[pl.ds(dst_row, 1)], sem)


def _dispatch_kernel(dest_ref, h_ref, init_hbm, hs_hbm, sem, *, tm):
    del init_hbm

    def issue(r, carry):
        _row_copy(h_ref, r, hs_hbm, dest_ref[0, 0, r], sem.at[0]).start()
        _row_copy(h_ref, r, hs_hbm, dest_ref[0, 0, tm + r], sem.at[1]).start()
        return carry

    lax.fori_loop(0, tm, issue, 0, unroll=ROW_DMA_UNROLL)
    for k in range(2):
        pltpu.make_async_copy(h_ref, hs_hbm.at[pl.ds(0, tm)], sem.at[k]).wait()


def _dispatch(h, dest_tiles, n_sorted_rows):
    t, d = h.shape
    tm = ROW_TILE
    init = jnp.zeros((n_sorted_rows, d), h.dtype)
    return pl.pallas_call(
        functools.partial(_dispatch_kernel, tm=tm),
        grid=(t // tm,),
        in_specs=[pl.BlockSpec((1, 1, 2 * tm), lambda i: (i, 0, 0), memory_space=pltpu.SMEM),
                  pl.BlockSpec((tm, d), lambda i: (i, 0)),
                  pl.BlockSpec(memory_space=pl.ANY)],
        out_specs=pl.BlockSpec(memory_space=pl.ANY),
        out_shape=jax.ShapeDtypeStruct((n_sorted_rows, d), h.dtype),
        scratch_shapes=[pltpu.SemaphoreType.DMA((2,))],
        input_output_aliases={2: 0},
        compiler_params=_cparams("arbitrary"),
        name="moe_dispatch",
    )(dest_tiles, h, init)


def _expert_kernel(te_ref, nt_ref, hs_ref, w1_ref, w2_ref, ys_ref, *, ffn, chunk):
    del te_ref
    j = pl.program_id(0)

    @pl.when(j < nt_ref[0])
    def _():
        hb = hs_ref[...].astype(BF16)
        acc = None
        for c0 in range(0, ffn, chunk):
            gate = _dot(hb, w1_ref[:, c0:c0 + chunk])
            up = _dot(hb, w1_ref[:, ffn + c0:ffn + c0 + chunk])
            part = _dot((_silu(gate) * up).astype(BF16), w2_ref[c0:c0 + chunk, :])
            acc = part if acc is None else acc + part
        ys_ref[...] = acc

    @pl.when(j >= nt_ref[0])
    def _():
        ys_ref[...] = jnp.zeros(ys_ref.shape, ys_ref.dtype)


def _experts(hs, tile_expert, n_tiles_used, w1, w2):
    r, d = hs.shape
    tm = EXPERT_TILE
    ffn = w2.shape[1]
    assert ffn % EXPERT_FCHUNK == 0

    def tile(j, nt):
        return jnp.minimum(j, nt[0] - 1)

    grid_spec = pltpu.PrefetchScalarGridSpec(
        num_scalar_prefetch=2,
        grid=(r // tm,),
        in_specs=[pl.BlockSpec((tm, d), lambda j, te, nt: (tile(j, nt), 0)),
                  pl.BlockSpec((None, d, 2 * ffn), lambda j, te, nt: (te[tile(j, nt)], 0, 0)),
                  pl.BlockSpec((None, ffn, d), lambda j, te, nt: (te[tile(j, nt)], 0, 0))],
        out_specs=pl.BlockSpec((tm, d), lambda j, te, nt: (j, 0)),
    )
    return pl.pallas_call(
        functools.partial(_expert_kernel, ffn=ffn, chunk=EXPERT_FCHUNK),
        grid_spec=grid_spec,
        out_shape=jax.ShapeDtypeStruct((r, d), F32),
        compiler_params=pltpu.CompilerParams(dimension_semantics=("arbitrary",),
                                             vmem_limit_bytes=EXPERT_VMEM_LIMIT_BYTES),
        name="moe_experts",
    )(tile_expert, n_tiles_used, hs, w1, w2)


def _combine_kernel(dest_ref, x_ref, gate_ref, info_ref, fg_ref, ys_hbm, o_ref, y1_s, y2_s, sem, *, tm, final_norm):
    def issue(r, carry):
        _row_copy(ys_hbm, dest_ref[0, 0, r], y1_s, r, sem.at[0]).start()
        _row_copy(ys_hbm, dest_ref[0, 0, tm + r], y2_s, r, sem.at[1]).start()
        return carry

    lax.fori_loop(0, tm, issue, 0, unroll=ROW_DMA_UNROLL)
    pltpu.make_async_copy(ys_hbm.at[pl.ds(0, tm)], y1_s, sem.at[0]).wait()
    pltpu.make_async_copy(ys_hbm.at[pl.ds(0, tm)], y2_s, sem.at[1]).wait()
    info = info_ref[...]
    y = info[:, 2:3] * y1_s[...] + info[:, 3:4] * y2_s[...]
    out = x_ref[...] + gate_ref[0] * y
    if final_norm:
        out = _rms(out, fg_ref[...])
    o_ref[...] = out


def _combine(x2, mods, row, info, dest_tiles, ys, final_g, final_norm):
    t, d = x2.shape
    tm = ROW_TILE
    return pl.pallas_call(
        functools.partial(_combine_kernel, tm=tm, final_norm=final_norm),
        grid=(t // tm,),
        in_specs=[pl.BlockSpec((1, 1, 2 * tm), lambda i: (i, 0, 0), memory_space=pltpu.SMEM),
                  pl.BlockSpec((tm, d), lambda i: (i, 0)),
                  _mod_spec(d, 5, row),
                  pl.BlockSpec((tm, LANES), lambda i: (i, 0)),
                  pl.BlockSpec((1, d), lambda i: (0, 0)),
                  pl.BlockSpec(memory_space=pl.ANY)],
        out_specs=pl.BlockSpec((tm, d), lambda i: (i, 0)),
        out_shape=jax.ShapeDtypeStruct((t, d), F32),
        scratch_shapes=[pltpu.VMEM((tm, d), F32), pltpu.VMEM((tm, d), F32), pltpu.SemaphoreType.DMA((2,))],
        compiler_params=_cparams("arbitrary"),
        name="moe_combine",
    )(dest_tiles, x2, mods, info, final_g, ys)


def _moe(x2, h, info, counts, mods, row, w1, w2, final_g, final_norm):
    t, d = x2.shape
    tm = ROW_TILE
    te = EXPERT_TILE
    cnt = counts[0, :N_EXPERTS].astype(jnp.int32)
    gsz = ((cnt + te - 1) // te) * te
    ends = jnp.cumsum(gsz)
    offs = ends - gsz
    e1 = info[:, 0].astype(jnp.int32)
    e2 = info[:, 1].astype(jnp.int32)
    dest1 = offs[e1] + info[:, 4].astype(jnp.int32)
    dest2 = offs[e2] + info[:, 5].astype(jnp.int32)
    dest_tiles = jnp.concatenate([dest1.reshape(t // tm, 1, tm), dest2.reshape(t // tm, 1, tm)], axis=2)
    n_sorted = 2 * t + N_EXPERTS * te
    n_tiles_max = n_sorted // te
    starts = jnp.arange(n_tiles_max, dtype=jnp.int32) * te
    tile_expert = jnp.minimum(jnp.sum((starts[:, None] >= ends[None, :]).astype(jnp.int32), axis=1), N_EXPERTS - 1)
    n_used = (ends[-1] // te).astype(jnp.int32).reshape(1)
    hs = _dispatch(h, dest_tiles, n_sorted)
    ys = _experts(hs, tile_expert, n_used, w1, w2)
    return _combine(x2, mods, row, info, dest_tiles, ys, final_g, final_norm)


_ROPE_SWAP = tuple(list(range(8, 16)) + list(range(0, 8)) + list(range(24, 32)) + list(range(16, 24)))


def _even_weights(w_in, w_uq, w_ukv):
    d = w_in.shape[0]
    n0 = 3 * CONV_CH + Q_RANK + KV_RANK
    kr = w_in[:, n0:n0 + QK_ROPE]
    kr_sw = kr[:, jnp.array(_ROPE_SWAP)]
    z = lambda n: jnp.zeros((d, n), w_in.dtype)
    w_in_x = jnp.concatenate([w_in[:, :n0], z(QK_NOPE), kr, z(LANES - QK_NOPE - QK_ROPE),
                              z(QK_NOPE), kr_sw, z(LANES - QK_NOPE - QK_ROPE)], axis=1).astype(BF16)
    uq = w_uq.reshape(Q_RANK, MLA_HEADS, QK_NOPE + QK_ROPE)
    pad = LANES - QK_NOPE - QK_ROPE
    uq_plain = jnp.pad(uq, ((0, 0), (0, 0), (0, pad))).reshape(Q_RANK, MLA_HEADS * LANES)
    uq_rope_sw = uq[:, :, QK_NOPE:][:, :, jnp.array(_ROPE_SWAP)]
    uq_sw = jnp.pad(uq_rope_sw, ((0, 0), (0, 0), (QK_NOPE, pad))).reshape(Q_RANK, MLA_HEADS * LANES)
    wq = jnp.concatenate([uq_plain, uq_sw], axis=1).astype(BF16)
    ukv = w_ukv.reshape(KV_RANK, MLA_HEADS, QK_NOPE + V_HEAD)
    k_part = jnp.pad(ukv[:, :, :QK_NOPE], ((0, 0), (0, 0), (0, LANES - QK_NOPE))).reshape(KV_RANK, MLA_HEADS * LANES)
    v_part = ukv[:, :, QK_NOPE:].reshape(KV_RANK, MLA_HEADS * V_HEAD)
    wkv = jnp.concatenate([k_part, v_part], axis=1).astype(BF16)
    return w_in_x, wq, wkv


def _rope_tables(seq_len, ctx_rows):
    n = QK_ROPE // 4
    t = jnp.arange(seq_len)
    inv = ROPE_BASE ** (-jnp.arange(n, dtype=F32) / n)
    ang_r = (t // GRID_W).astype(F32)[:, None] * inv
    ang_c = (t % GRID_W).astype(F32)[:, None] * inv
    cr, sr, cc, sc = jnp.cos(ang_r), jnp.sin(ang_r), jnp.cos(ang_c), jnp.sin(ang_c)
    cos32 = jnp.concatenate([cr, cr, cc, cc], axis=1)
    sin32 = jnp.concatenate([-sr, sr, -sc, sc], axis=1)
    pad = LANES - QK_NOPE - QK_ROPE
    cos_x = jnp.concatenate([jnp.ones((seq_len, QK_NOPE), F32), cos32, jnp.zeros((seq_len, pad), F32)], axis=1)
    sin_x = jnp.concatenate([jnp.zeros((seq_len, QK_NOPE), F32), sin32, jnp.zeros((seq_len, pad), F32)], axis=1)
    cos_c = jnp.concatenate([jnp.ones((ctx_rows, QK_NOPE + QK_ROPE), F32), jnp.zeros((ctx_rows, pad), F32)], axis=1)
    sin_c = jnp.zeros((ctx_rows, LANES), F32)
    return jnp.concatenate([cos_c, cos_x], axis=0), jnp.concatenate([sin_c, sin_x], axis=0)


def _even_layer(xa, mods, dims, tabs, p):
    batch, ctx_len, seq_len = dims
    tm = ROW_TILE
    t = xa.shape[0]
    n_ctx_tiles = batch * ctx_len // tm
    x_tps = seq_len // tm
    row = _mod_row_fn(n_ctx_tiles, x_tps, batch)
    w_in_x, wq, wkv = _even_weights(p["w_in"], p["w_uq"], p["w_ukv"])
    cos_t, sin_t = tabs
    tab_idx = lambda i: jnp.where(i < n_ctx_tiles, 0, 1 + (i - n_ctx_tiles) % x_tps)
    conv_y, q, k, v = _even_proj(xa, p["norm1_g"].reshape(1, -1), mods, row, t // tm, w_in_x, p["conv_w"],
                                 p["q_norm_g"].reshape(1, -1), wq, p["kv_norm_g"].reshape(1, -1), wkv,
                                 cos_t, sin_t, tab_idx, n_ctx_tiles, ctx_len // tm, x_tps)
    att = _attention(q, k, v, batch, ctx_len, seq_len)
    w_out = p["w_out"].astype(BF16)
    xa = _resid_mm(xa, mods, 2, row, [conv_y, att], [w_out[:CONV_CH], w_out[CONV_CH:]])
    hmid = _ffn_up(xa, p["norm2_g"].reshape(1, -1), mods, row, p["ffn_w1"].astype(BF16))
    return _resid_mm(xa, mods, 5, row, [hmid], [p["ffn_w2"].astype(BF16)])


def _odd_layer(xa, mods, dims, p, ctx_out, final_g, final_norm):
    batch, ctx_len, seq_len = dims
    tm = ROW_TILE
    t = xa.shape[0]
    n_ctx_tiles = batch * ctx_len // tm
    x_tps = seq_len // tm
    row = _mod_row_fn(n_ctx_tiles, x_tps, batch)
    n_w = p["w_in"].shape[1]
    w_in = jnp.pad(p["w_in"], ((0, 0), (0, 4 * GDN_QK + LANES - n_w))).astype(BF16)
    qkvn, z, abt = _odd_proj(xa, p["norm1_g"].reshape(1, -1), mods, row, w_in, p["qkv_conv_w"],
                             n_ctx_tiles, ctx_len // tm, x_tps)
    abt_chunks = abt[:, :4 * GDN_HEADS].reshape(t // GDN_STEP, GDN_STEP, 4 * GDN_HEADS).transpose(0, 2, 1)
    alog_b = jnp.broadcast_to(p["a_log"].reshape(2 * GDN_HEADS, 1), (2 * GDN_HEADS, GDN_STEP))
    dtb_b = jnp.broadcast_to(p["dt_bias"].reshape(2 * GDN_HEADS, 1), (2 * GDN_HEADS, GDN_STEP))
    o_dirs = _gdn(qkvn, abt_chunks, alog_b, dtb_b, batch, ctx_len, seq_len)
    row_off = 0 if ctx_out else n_ctx_tiles
    n_rows = t - row_off * tm
    rw = jnp.pad(p["router_w"], ((0, 0), (0, LANES - N_EXPERTS)))
    xo, h, info, counts = _odd_out(xa, mods, row, o_dirs, z, p["o_norm_g"].reshape(1, -1), p["w_out"].astype(BF16),
                                   p["norm2_g"].reshape(1, -1), rw, row_off, n_rows)
    row_o = row if ctx_out else _mod_row_fn(0, x_tps, batch)
    return _moe(xo, h, info, counts, mods, row_o, p["moe_w1"].astype(BF16), p["moe_w2"].astype(BF16),
                final_g, final_norm)


_EVEN_NAMES = ("mod_w", "mod_b", "norm1_g", "w_in", "conv_w", "q_norm_g", "w_uq", "kv_norm_g", "w_ukv", "w_out",
               "norm2_g", "ffn_w1", "ffn_w2")
_ODD_NAMES = ("mod_w", "mod_b", "norm1_g", "w_in", "qkv_conv_w", "a_log", "dt_bias", "o_norm_g", "w_out",
              "norm2_g", "router_w", "moe_w1", "moe_w2")


def kernel(x, c, ctx, c_ctx, l0_mod_w, l0_mod_b, l0_norm1_g, l0_w_in, l0_conv_w, l0_q_norm_g, l0_w_uq, l0_kv_norm_g, l0_w_ukv, l0_w_out, l0_norm2_g, l0_ffn_w1, l0_ffn_w2, l1_mod_w, l1_mod_b, l1_norm1_g, l1_w_in, l1_qkv_conv_w, l1_a_log, l1_dt_bias, l1_o_norm_g, l1_w_out, l1_norm2_g, l1_router_w, l1_moe_w1, l1_moe_w2, l2_mod_w, l2_mod_b, l2_norm1_g, l2_w_in, l2_conv_w, l2_q_norm_g, l2_w_uq, l2_kv_norm_g, l2_w_ukv, l2_w_out, l2_norm2_g, l2_ffn_w1, l2_ffn_w2, l3_mod_w, l3_mod_b, l3_norm1_g, l3_w_in, l3_qkv_conv_w, l3_a_log, l3_dt_bias, l3_o_norm_g, l3_w_out, l3_norm2_g, l3_router_w, l3_moe_w1, l3_moe_w2, final_norm_g):
    batch, seq_len, d = x.shape
    ctx_len = ctx.shape[1]
    assert seq_len % ROW_TILE == 0 and ctx_len % ROW_TILE == 0 and seq_len % GRID_W == 0
    dims = (batch, ctx_len, seq_len)
    layers = (
        dict(zip(_EVEN_NAMES, (l0_mod_w, l0_mod_b, l0_norm1_g, l0_w_in, l0_conv_w, l0_q_norm_g, l0_w_uq,
                               l0_kv_norm_g, l0_w_ukv, l0_w_out, l0_norm2_g, l0_ffn_w1, l0_ffn_w2))),
        dict(zip(_ODD_NAMES, (l1_mod_w, l1_mod_b, l1_norm1_g, l1_w_in, l1_qkv_conv_w, l1_a_log, l1_dt_bias,
                              l1_o_norm_g, l1_w_out, l1_norm2_g, l1_router_w, l1_moe_w1, l1_moe_w2))),
        dict(zip(_EVEN_NAMES, (l2_mod_w, l2_mod_b, l2_norm1_g, l2_w_in, l2_conv_w, l2_q_norm_g, l2_w_uq,
                               l2_kv_norm_g, l2_w_ukv, l2_w_out, l2_norm2_g, l2_ffn_w1, l2_ffn_w2))),
        dict(zip(_ODD_NAMES, (l3_mod_w, l3_mod_b, l3_norm1_g, l3_w_in, l3_qkv_conv_w, l3_a_log, l3_dt_bias,
                              l3_o_norm_g, l3_w_out, l3_norm2_g, l3_router_w, l3_moe_w1, l3_moe_w2))),
    )
    xa = jnp.concatenate([ctx.reshape(batch * ctx_len, d), x.reshape(batch * seq_len, d)], axis=0)
    mod_rows = 8 * ((batch + 1 + 7) // 8)
    cc = jnp.zeros((mod_rows, d), F32).at[:batch].set(c).at[batch].set(c_ctx)
    tabs = _rope_tables(seq_len, ROW_TILE)
    n_layers = len(layers)
    assert n_layers % 2 == 0
    final_g = final_norm_g.reshape(1, -1)
    for li, p in enumerate(layers):
        mods = _mods(cc, p["mod_w"], p["mod_b"])
        last = li == n_layers - 1
        if li % 2 == 0:
            xa = _even_layer(xa, mods, dims, tabs, p)
        else:
            xa = _odd_layer(xa, mods, dims, p, not last, final_g, last)
    return xa.reshape(batch, seq_len, d)
```

```python
import functools
import math

import jax
import jax.numpy as jnp
from jax import lax
from jax.experimental import pallas as pl
from jax.experimental.pallas import tpu as pltpu

F32 = jnp.float32
BF16 = jnp.bfloat16

NORM_EPS = 1e-6
N_MOD = 6
GRID_W = 64
ROPE_BASE = 10000.0

CONV_CH = 512
MLA_HEADS = 8
Q_RANK = 256
KV_RANK = 128
QK_NOPE = 64
QK_ROPE = 32
V_HEAD = 64
MLA_SCALE = (QK_NOPE + QK_ROPE) ** -0.5

GDN_HEADS = 8
GDN_DK = 128
GDN_CHUNK = 64
GDN_QK = GDN_HEADS * GDN_DK

N_EXPERTS = 8

LANES = 128
ROW_TILE = 256
ATTN_HEAD_GROUP = 4
GDN_STEP = 256
GDN_HEAD_GROUP = 8
ROW_DMA_UNROLL = 8
EXPERT_TILE = 512
EXPERT_FCHUNK = 512
VMEM_LIMIT_BYTES = 56 * 1024 * 1024
EXPERT_VMEM_LIMIT_BYTES = 62 * 1024 * 1024


def _cparams(*sem):
    return pltpu.CompilerParams(dimension_semantics=sem, vmem_limit_bytes=VMEM_LIMIT_BYTES)


def _silu(x):
    return x * (1.0 / (1.0 + jnp.exp(-x)))


def _rms(x, g):
    return x * lax.rsqrt(jnp.mean(x * x, axis=-1, keepdims=True) + NORM_EPS) * g


def _dot(a, b):
    return jnp.dot(a, b, preferred_element_type=F32)


def _dot_nt(a, b):
    return lax.dot_general(a, b, (((1,), (1,)), ((), ())), preferred_element_type=F32)


def _dot_tn(a, b):
    return lax.dot_general(a, b, (((0,), (0,)), ((), ())), preferred_element_type=F32)


def _split3(x):
    x1 = x.astype(BF16)
    r1 = x - x1.astype(F32)
    x2 = r1.astype(BF16)
    x3 = (r1 - x2.astype(F32)).astype(BF16)
    return x1, x2, x3


def _dot_f32(a, b):
    a1, a2, a3 = _split3(a)
    b1, b2, b3 = _split3(b)
    acc = _dot(a1, b3) + _dot(a2, b2) + _dot(a3, b1)
    acc = acc + _dot(a1, b2) + _dot(a2, b1)
    return acc + _dot(a1, b1)


def _dot_f32_exact_rhs(a, m_bf16):
    a1, a2, a3 = _split3(a)
    return _dot(a3, m_bf16) + _dot(a2, m_bf16) + _dot(a1, m_bf16)


def _mod_row_fn(n_ctx_tiles, tiles_per_seq, batch):
    def row(i):
        return jnp.where(i < n_ctx_tiles, batch, (i - n_ctx_tiles) // tiles_per_seq)
    return row


def _mods_kernel(c_ref, w_ref, b_ref, o_ref):
    o_ref[...] = _dot_f32(_silu(c_ref[...]), w_ref[...]) + b_ref[...]


def _mods(cc, mod_w, mod_b):
    rows, d = cc.shape
    n = mod_w.shape[1]
    out = pl.pallas_call(
        _mods_kernel,
        grid=(n // d,),
        in_specs=[pl.BlockSpec((rows, d), lambda j: (0, 0)),
                  pl.BlockSpec((d, d), lambda j: (0, j)),
                  pl.BlockSpec((1, d), lambda j: (0, j))],
        out_specs=pl.BlockSpec((rows, d), lambda j: (0, j)),
        out_shape=jax.ShapeDtypeStruct((rows, n), F32),
        compiler_params=_cparams("arbitrary"),
        name="adaln_mods",
    )(cc, mod_w, mod_b.reshape(1, n))
    return out.reshape(rows, 1, n)


def _mod_spec(d, k, row, off=0):
    return pl.BlockSpec((1, 1, d), lambda i: (row(i + off), 0, k))


def _even_proj_kernel(x_ref, xp_ref, xn_ref, g_ref, sh_ref, sc_ref, win_ref, cw_ref, qg_ref, wq_ref, kvg_ref, wkv_ref,
                      cos_ref, sin_ref, conv_ref, q_ref, k_ref, v_ref, *, n_ctx_tiles, ctx_tps, x_tps):
    i = pl.program_id(0)
    keep_prev, keep_next = _seq_edges(i, n_ctx_tiles, ctx_tps, x_tps)
    tm = x_ref.shape[0]
    xs = jnp.concatenate([x_ref[...], xp_ref[...], xn_ref[...]], axis=0)
    hb_all = (_rms(xs, g_ref[...]) * (1.0 + sc_ref[0]) + sh_ref[0]).astype(BF16)
    hb = hb_all[:tm]
    c = CONV_CH
    n_gbu = 3 * c
    z_all = _dot(hb_all, win_ref[:, c:2 * c]) * _dot(hb_all, win_ref[:, 2 * c:n_gbu])
    z = z_all[:tm]
    zm, zp = _shifted(z, z_all[tm + X_HALO - 1:tm + X_HALO] * keep_prev, z_all[tm + X_HALO:tm + X_HALO + 1] * keep_next)
    cw = cw_ref[...]
    conv = zm * cw[0:1] + z * cw[1:2] + zp * cw[2:3]
    conv_ref[...] = (_dot(hb, win_ref[:, :c]) * conv).astype(conv_ref.dtype)
    rest = _dot(hb, win_ref[:, n_gbu:])
    cq = rest[:, :Q_RANK]
    ckv = rest[:, Q_RANK:Q_RANK + KV_RANK]
    kr = rest[:, Q_RANK + KV_RANK:Q_RANK + KV_RANK + LANES]
    kr_sw = rest[:, Q_RANK + KV_RANK + LANES:]
    cos = cos_ref[...]
    sin = sin_ref[...]
    kr_rot = kr * cos + kr_sw * sin
    qq = _dot(_rms(cq, qg_ref[...]).astype(BF16), wq_ref[...])
    kv = _dot(_rms(ckv, kvg_ref[...]).astype(BF16), wkv_ref[...])
    hw = MLA_HEADS * LANES
    for h in range(MLA_HEADS):
        sl = slice(h * LANES, (h + 1) * LANES)
        qh = (qq[:, sl] * cos + qq[:, hw + h * LANES:hw + (h + 1) * LANES] * sin) * MLA_SCALE
        q_ref[:, sl] = qh.astype(q_ref.dtype)
        k_ref[:, sl] = (kv[:, sl] + kr_rot).astype(k_ref.dtype)
    v_ref[...] = kv[:, hw:].astype(v_ref.dtype)


def _even_proj(x2, g, mods, row, n_tiles, w_in, conv_w, qg, wq, kvg, wkv, cos_t, sin_t, tab_idx,
               n_ctx_tiles, ctx_tps, x_tps):
    t, d = x2.shape
    tm = ROW_TILE
    n_in = w_in.shape[1]
    hw = MLA_HEADS * LANES
    const = lambda i: (0, 0)
    r = tm // X_HALO
    nb = t // X_HALO
    return pl.pallas_call(
        functools.partial(_even_proj_kernel, n_ctx_tiles=n_ctx_tiles, ctx_tps=ctx_tps, x_tps=x_tps),
        grid=(n_tiles,),
        in_specs=[pl.BlockSpec((tm, d), lambda i: (i, 0)),
                  pl.BlockSpec((X_HALO, d), lambda i: (jnp.maximum(i * r - 1, 0), 0)),
                  pl.BlockSpec((X_HALO, d), lambda i: (jnp.minimum((i + 1) * r, nb - 1), 0)),
                  pl.BlockSpec((1, d), const),
                  _mod_spec(d, 0, row), _mod_spec(d, 1, row),
                  pl.BlockSpec((d, n_in), const),
                  pl.BlockSpec(conv_w.shape, const),
                  pl.BlockSpec((1, Q_RANK), const),
                  pl.BlockSpec(wq.shape, const),
                  pl.BlockSpec((1, KV_RANK), const),
                  pl.BlockSpec(wkv.shape, const),
                  pl.BlockSpec((tm, LANES), lambda i: (tab_idx(i), 0)),
                  pl.BlockSpec((tm, LANES), lambda i: (tab_idx(i), 0))],
        out_specs=[pl.BlockSpec((tm, CONV_CH), lambda i: (i, 0)),
                   pl.BlockSpec((tm, hw), lambda i: (i, 0)),
                   pl.BlockSpec((tm, hw), lambda i: (i, 0)),
                   pl.BlockSpec((tm, MLA_HEADS * V_HEAD), lambda i: (i, 0))],
        out_shape=[jax.ShapeDtypeStruct((t, CONV_CH), BF16),
                   jax.ShapeDtypeStruct((t, hw), BF16),
                   jax.ShapeDtypeStruct((t, hw), BF16),
                   jax.ShapeDtypeStruct((t, MLA_HEADS * V_HEAD), BF16)],
        compiler_params=_cparams("arbitrary"),
        name="even_proj",
    )(x2, x2, x2, g, mods, mods, w_in, conv_w, qg, wq, kvg, wkv, cos_t, sin_t)


def _shifted(z, prev_row, next_row):
    tm = z.shape[0]
    ridx = lax.broadcasted_iota(jnp.int32, z.shape, 0)
    zm = jnp.where(ridx == 0, prev_row, pltpu.roll(z, 1, 0))
    zp = jnp.where(ridx == tm - 1, next_row, pltpu.roll(z, tm - 1, 0))
    return zm, zp


def _seq_edges(i, n_ctx_tiles, ctx_tps, x_tps):
    j = jnp.where(i < n_ctx_tiles, i % ctx_tps, (i - n_ctx_tiles) % x_tps)
    tps = jnp.where(i < n_ctx_tiles, ctx_tps, x_tps)
    keep_prev = (j != 0).astype(F32)
    keep_next = (j != tps - 1).astype(F32)
    return keep_prev, keep_next


def _attn_kernel(q_ref, kc_ref, kx_ref, vc_ref, vx_ref, o_ref, *, n_ctx_q):
    qi = pl.program_id(1)

    def run(use_x):
        for g0 in range(0, MLA_HEADS, ATTN_HEAD_GROUP):
            hs = range(g0, g0 + ATTN_HEAD_GROUP)
            ks = {h: slice(h * LANES, (h + 1) * LANES) for h in hs}
            vs = {h: slice(h * V_HEAD, (h + 1) * V_HEAD) for h in hs}
            q = {h: q_ref[:, ks[h]] for h in hs}
            sc = {h: _dot_nt(q[h], kc_ref[:, ks[h]]) for h in hs}
            m = {h: jnp.max(sc[h], axis=1, keepdims=True) for h in hs}
            if use_x:
                sx = {h: _dot_nt(q[h], kx_ref[:, ks[h]]) for h in hs}
                m = {h: jnp.maximum(m[h], jnp.max(sx[h], axis=1, keepdims=True)) for h in hs}
            pc = {h: jnp.exp(sc[h] - m[h]) for h in hs}
            l = {h: jnp.sum(pc[h], axis=1, keepdims=True) for h in hs}
            o = {h: _dot(pc[h].astype(BF16), vc_ref[:, vs[h]]) for h in hs}
            if use_x:
                px = {h: jnp.exp(sx[h] - m[h]) for h in hs}
                l = {h: l[h] + jnp.sum(px[h], axis=1, keepdims=True) for h in hs}
                o = {h: o[h] + _dot(px[h].astype(BF16), vx_ref[:, vs[h]]) for h in hs}
            outs = [o[h] * (1.0 / l[h]) for h in hs]
            for j in range(0, ATTN_HEAD_GROUP, 2):
                c0 = (g0 + j) * V_HEAD
                o_ref[:, c0:c0 + 2 * V_HEAD] = jnp.concatenate(outs[j:j + 2], axis=1).astype(o_ref.dtype)

    @pl.when(qi < n_ctx_q)
    def _():
        run(False)

    @pl.when(qi >= n_ctx_q)
    def _():
        run(True)


def _attention(q, k, v, batch, ctx_len, seq_len):
    tq = ROW_TILE
    n_ctx_q = ctx_len // tq
    n_x_q = seq_len // tq
    ctx_blocks = batch * ctx_len // tq
    assert (batch * ctx_len) % seq_len == 0
    x_blk0 = batch * ctx_len // seq_len
    hw = MLA_HEADS * LANES
    vw = MLA_HEADS * V_HEAD

    def qrow(b, qi):
        return jnp.where(qi < n_ctx_q, b * n_ctx_q + qi, ctx_blocks + b * n_x_q + (qi - n_ctx_q))

    return pl.pallas_call(
        functools.partial(_attn_kernel, n_ctx_q=n_ctx_q),
        grid=(batch, n_ctx_q + n_x_q),
        in_specs=[pl.BlockSpec((tq, hw), lambda b, qi: (qrow(b, qi), 0)),
                  pl.BlockSpec((ctx_len, hw), lambda b, qi: (b, 0)),
                  pl.BlockSpec((seq_len, hw), lambda b, qi: (x_blk0 + b, 0)),
                  pl.BlockSpec((ctx_len, vw), lambda b, qi: (b, 0)),
                  pl.BlockSpec((seq_len, vw), lambda b, qi: (x_blk0 + b, 0))],
        out_specs=pl.BlockSpec((tq, vw), lambda b, qi: (qrow(b, qi), 0)),
        out_shape=jax.ShapeDtypeStruct((q.shape[0], vw), BF16),
        compiler_params=_cparams("arbitrary", "arbitrary"),
        name="mla_attention",
    )(q, k, k, v, v)


def _resid_mm_kernel(*refs, n_lhs):
    x_ref, gate_ref = refs[0], refs[1]
    lhs = refs[2:2 + n_lhs]
    ws = refs[2 + n_lhs:2 + 2 * n_lhs]
    o_ref = refs[2 + 2 * n_lhs]
    acc = _dot(lhs[0][...], ws[0][...])
    for a_ref, w_ref in zip(lhs[1:], ws[1:]):
        acc = acc + _dot(a_ref[...], w_ref[...])
    o_ref[...] = x_ref[...] + gate_ref[0] * acc


def _resid_mm(x2, mods, k_gate, row, lhs_list, w_list, tm=512):
    t, d = x2.shape
    n = len(lhs_list)
    in_specs = [pl.BlockSpec((tm, d), lambda i: (i, 0)), _mod_spec(d, k_gate, lambda i: row(i * tm // ROW_TILE))]
    in_specs += [pl.BlockSpec((tm, a.shape[1]), lambda i: (i, 0)) for a in lhs_list]
    in_specs += [pl.BlockSpec(w.shape, lambda i: (0, 0)) for w in w_list]
    return pl.pallas_call(
        functools.partial(_resid_mm_kernel, n_lhs=n),
        grid=(t // tm,),
        in_specs=in_specs,
        out_specs=pl.BlockSpec((tm, d), lambda i: (i, 0)),
        out_shape=jax.ShapeDtypeStruct((t, d), F32),
        compiler_params=_cparams("arbitrary"),
        name="resid_matmul",
    )(x2, mods, *lhs_list, *w_list)


def _ffn_up_kernel(x_ref, g_ref, sh_ref, sc_ref, w_ref, o_ref, *, ffn, chunk):
    hn = _rms(x_ref[...], g_ref[...]) * (1.0 + sc_ref[0]) + sh_ref[0]
    hb = hn.astype(BF16)
    for c0 in range(0, ffn, chunk):
        gate = _dot(hb, w_ref[:, c0:c0 + chunk])
        up = _dot(hb, w_ref[:, ffn + c0:ffn + c0 + chunk])
        o_ref[:, c0:c0 + chunk] = (_silu(gate) * up).astype(o_ref.dtype)


def _ffn_up(x2, g, mods, row, w1):
    t, d = x2.shape
    tm = ROW_TILE
    ffn = w1.shape[1] // 2
    chunk = 256
    assert ffn % chunk == 0
    return pl.pallas_call(
        functools.partial(_ffn_up_kernel, ffn=ffn, chunk=chunk),
        grid=(t // tm,),
        in_specs=[pl.BlockSpec((tm, d), lambda i: (i, 0)),
                  pl.BlockSpec((1, d), lambda i: (0, 0)),
                  _mod_spec(d, 3, row), _mod_spec(d, 4, row),
                  pl.BlockSpec(w1.shape, lambda i: (0, 0))],
        out_specs=pl.BlockSpec((tm, ffn), lambda i: (i, 0)),
        out_shape=jax.ShapeDtypeStruct((t, ffn), BF16),
        compiler_params=_cparams("arbitrary"),
        name="ffn_up",
    )(x2, g, mods, mods, w1)


X_HALO = 8


def _odd_proj_kernel(x_ref, xp_ref, xn_ref, g_ref, sh_ref, sc_ref, w_ref, cw_ref, qkv_ref, gate_ref, abt_ref,
                     *, chunk, n_ctx_tiles, ctx_tps, x_tps):
    i = pl.program_id(0)
    keep_prev, keep_next = _seq_edges(i, n_ctx_tiles, ctx_tps, x_tps)
    tm = x_ref.shape[0]
    xs = jnp.concatenate([x_ref[...], xp_ref[...], xn_ref[...]], axis=0)
    hb = (_rms(xs, g_ref[...]) * (1.0 + sc_ref[0]) + sh_ref[0]).astype(BF16)
    n_qkv = qkv_ref.shape[0] * LANES
    n_gate = gate_ref.shape[1]
    cw = cw_ref[...]
    for c0 in range(0, n_qkv, chunk):
        z_all = _dot(hb, w_ref[:, c0:c0 + chunk])
        z = z_all[:tm]
        zprev = z_all[tm + X_HALO - 1:tm + X_HALO] * keep_prev
        znext = z_all[tm + X_HALO:tm + X_HALO + 1] * keep_next
        zm, zp = _shifted(z, zprev, znext)
        y = _silu(zm * cw[0:1, c0:c0 + chunk] + z * cw[1:2, c0:c0 + chunk] + zp * cw[2:3, c0:c0 + chunk])
        for hh in range(chunk // LANES):
            head = c0 // LANES + hh
            yh = y[:, hh * LANES:(hh + 1) * LANES]
            if head < 2 * GDN_HEADS:
                nrm = lax.rsqrt(jnp.sum(yh * yh, axis=-1, keepdims=True) + NORM_EPS)
                if head < GDN_HEADS:
                    nrm = nrm * (GDN_DK ** -0.5)
                yh = yh * nrm
            qkv_ref[head] = yh
    hb_cur = hb[:tm]
    for c0 in range(0, n_gate, chunk):
        gate_ref[:, c0:c0 + chunk] = _dot(hb_cur, w_ref[:, n_qkv + c0:n_qkv + c0 + chunk]).astype(gate_ref.dtype)
    abt_ref[...] = _dot(hb_cur, w_ref[:, n_qkv + n_gate:])


def _odd_proj(x2, g, mods, row, w_in, conv_w, n_ctx_tiles, ctx_tps, x_tps):
    t, d = x2.shape
    tm = ROW_TILE
    n_qkv = 3 * GDN_QK
    n_gate = GDN_QK
    r = tm // X_HALO
    nb = t // X_HALO
    return pl.pallas_call(
        functools.partial(_odd_proj_kernel, chunk=512, n_ctx_tiles=n_ctx_tiles, ctx_tps=ctx_tps, x_tps=x_tps),
        grid=(t // tm,),
        in_specs=[pl.BlockSpec((tm, d), lambda i: (i, 0)),
                  pl.BlockSpec((X_HALO, d), lambda i: (jnp.maximum(i * r - 1, 0), 0)),
                  pl.BlockSpec((X_HALO, d), lambda i: (jnp.minimum((i + 1) * r, nb - 1), 0)),
                  pl.BlockSpec((1, d), lambda i: (0, 0)),
                  _mod_spec(d, 0, row), _mod_spec(d, 1, row),
                  pl.BlockSpec(w_in.shape, lambda i: (0, 0)),
                  pl.BlockSpec(conv_w.shape, lambda i: (0, 0))],
        out_specs=[pl.BlockSpec((n_qkv // LANES, tm, LANES), lambda i: (0, i, 0)),
                   pl.BlockSpec((tm, n_gate), lambda i: (i, 0)),
                   pl.BlockSpec((tm, LANES), lambda i: (i, 0))],
        out_shape=[jax.ShapeDtypeStruct((n_qkv // LANES, t, LANES), F32),
                   jax.ShapeDtypeStruct((t, n_gate), BF16),
                   jax.ShapeDtypeStruct((t, LANES), F32)],
        compiler_params=_cparams("arbitrary"),
        name="odd_proj",
    )(x2, x2, x2, g, mods, mods, w_in, conv_w)


def _gdn_kernel(qkvf_ref, abtf_ref, qkvb_ref, abtb_ref, alog_ref, dtb_ref, of_ref, ob_ref, *scratch):
    fwd, bwd = scratch[:len(scratch) // 2], scratch[len(scratch) // 2:]

    @pl.when(pl.program_id(1) == 0)
    def _():
        fwd[0][...] = jnp.zeros(fwd[0].shape, fwd[0].dtype)
        bwd[0][...] = jnp.zeros(bwd[0].shape, bwd[0].dtype)

    _gdn_body(0, qkvf_ref, abtf_ref, alog_ref, dtb_ref, of_ref, *fwd)
    _gdn_body(1, qkvb_ref, abtb_ref, alog_ref, dtb_ref, ob_ref, *bwd)


def _gdn_body(d, qkv_ref, abt_ref, alog_ref, dtb_ref, o_ref,
              s_ref, gc_s, beta_s, gtot_s, uw_s, qg_s, kd_s, attn_s, vnew_s, gtc_s):
    n = GDN_STEP
    c = GDN_CHUNK
    nchunk = n // c
    h_n = GDN_HEADS
    sgn = 1 - 2 * d
    ri = lax.broadcasted_iota(jnp.int32, (n, n), 0)
    ci = lax.broadcasted_iota(jnp.int32, (n, n), 1)
    same = (ri // c) == (ci // c)
    rel = (ri - ci) * sgn
    incl = same & (rel >= 0)
    strict = same & (rel > 0)
    eye = (ri == ci).astype(F32)

    hrow = pl.ds(d * h_n, h_n)
    a_all = abt_ref[0, hrow, :]
    bt_all = abt_ref[0, pl.ds(2 * h_n + d * h_n, h_n), :]
    sp_in = a_all + dtb_ref[hrow, :]
    softplus = jnp.maximum(sp_in, 0.0) + jnp.log(1.0 + jnp.exp(-jnp.abs(sp_in)))
    g_all = -jnp.exp(alog_ref[hrow, :]) * softplus
    beta_s[...] = 1.0 / (1.0 + jnp.exp(-bt_all))
    cum_m = jnp.where(same & ((ci - ri) * sgn >= 0), 1.0, 0.0).astype(BF16)
    gc_s[...] = _dot_f32_exact_rhs(g_all, cum_m)
    gtot_s[...] = _dot_f32_exact_rhs(g_all, same.astype(BF16))

    grp = uw_s.shape[0]
    n_sq = (c - 1).bit_length() - 1

    def head_group(gi, carry):
        heads = [gi * grp + u for u in range(grp)]
        us = range(grp)
        q = [qkv_ref[h] for h in heads]
        k = [qkv_ref[h_n + h] for h in heads]
        v = [qkv_ref[2 * h_n + h] for h in heads]
        gc_row = [gc_s[pl.ds(h, 1), :] for h in heads]
        gc_col = [jnp.sum(eye * gc_row[u], axis=1, keepdims=True) for u in us]
        beta_col = [jnp.sum(eye * beta_s[pl.ds(h, 1), :], axis=1, keepdims=True) for h in heads]
        gtot_col = [jnp.sum(eye * gtot_s[pl.ds(h, 1), :], axis=1, keepdims=True) for h in heads]
        decay = [jnp.exp(jnp.where(incl, gc_col[u] - gc_row[u], -1e30)) for u in us]
        kb = [k[u] * beta_col[u] for u in us]
        k16 = [k[u].astype(BF16) for u in us]
        p = [jnp.where(strict, -(_dot_nt(kb[u].astype(BF16), k16[u]) * decay[u]), 0.0) for u in us]
        for u in us:
            attn_s[u] = (_dot_nt(q[u].astype(BF16), k16[u]) * decay[u]).astype(attn_s.dtype)
            gtc_s[u] = jnp.broadcast_to(gtot_col[u], gtc_s.shape[1:])
        tinv = [eye + p[u] for u in us]
        p16 = [p[u].astype(BF16) for u in us]
        p = [_dot(p16[u], p16[u]) for u in us]
        for _ in range(n_sq - 1):
            p16 = [p[u].astype(BF16) for u in us]
            tp = [_dot(jnp.concatenate([tinv[u].astype(BF16), p16[u]], axis=0), p16[u]) for u in us]
            tinv = [tinv[u] + tp[u][:n] for u in us]
            p = [tp[u][n:] for u in us]
        tinv = [tinv[u] + _dot(tinv[u].astype(BF16), p[u].astype(BF16)) for u in us]
        egc = [jnp.exp(gc_col[u]) for u in us]
        for u in us:
            rhs = jnp.concatenate([v[u] * beta_col[u], kb[u] * egc[u]], axis=1).astype(BF16)
            uw_s[u] = _dot(tinv[u].astype(BF16), rhs)
            qg_s[u] = q[u] * egc[u]
            kd_s[u] = k[u] * jnp.exp(gtot_col[u] - gc_col[u])
            vnew_s[u] = jnp.zeros(vnew_s.shape[1:], vnew_s.dtype)
        s = [s_ref[h] for h in heads]
        for cstep in range(nchunk):
            r0 = (cstep if d == 0 else nchunk - 1 - cstep) * c
            rows = pl.ds(r0, c)
            ws_qs = [_dot(jnp.concatenate([uw_s[u, rows, LANES:], qg_s[u, rows, :]], axis=0).astype(BF16),
                          s[u].astype(BF16)) for u in us]
            v_new = [uw_s[u, rows, :LANES] - ws_qs[u][:c] for u in us]
            for u in us:
                vnew_s[u, rows, :] = v_new[u].astype(vnew_s.dtype)
            for u in us:
                o_ref[heads[u], rows, :] = ws_qs[u][c:] + _dot(attn_s[u, rows, :], vnew_s[u])
            s = [s[u] * jnp.exp(gtc_s[u, pl.ds(r0, 1), :])
                 + _dot_tn(kd_s[u, rows, :].astype(BF16), v_new[u].astype(BF16)) for u in us]
        for u in us:
            s_ref[heads[u]] = s[u]
        return carry

    lax.fori_loop(0, h_n // grp, head_group, 0)


def _gdn(qkvn, abt_chunks, alog_b, dtb_b, batch, ctx_len, seq_len):
    nh3, t, _ = qkvn.shape
    n = GDN_STEP
    nc = ctx_len // n
    nx = seq_len // n
    ctx_blocks = batch * nc

    def rowblk(b, d, s):
        sc = jnp.where(d == 0, s, nc - 1 - s)
        sx = jnp.where(d == 0, s - nc, nx - 1 - (s - nc))
        return jnp.where(s < nc, b * nc + sc, ctx_blocks + b * nx + sx)

    vm = pltpu.VMEM
    g = GDN_HEAD_GROUP
    per_dir = [vm((GDN_HEADS, GDN_DK, LANES), F32),
               vm((GDN_HEADS, n), F32),
               vm((GDN_HEADS, n), F32),
               vm((GDN_HEADS, n), F32),
               vm((g, n, 2 * LANES), F32),
               vm((g, n, LANES), F32),
               vm((g, n, LANES), F32),
               vm((g, n, n), BF16),
               vm((g, n, LANES), BF16),
               vm((g, n, LANES), F32)]
    qkv_spec = lambda d: pl.BlockSpec((nh3, n, LANES), lambda b, s: (0, rowblk(b, d, s), 0))
    abt_spec = lambda d: pl.BlockSpec((1, 4 * GDN_HEADS, n), lambda b, s: (rowblk(b, d, s), 0, 0))
    out_spec = lambda d: pl.BlockSpec((GDN_HEADS, n, LANES), lambda b, s: (0, rowblk(b, d, s), 0))
    out_sds = jax.ShapeDtypeStruct((GDN_HEADS, t, LANES), F32)
    return pl.pallas_call(
        _gdn_kernel,
        grid=(batch, nc + nx),
        in_specs=[qkv_spec(0), abt_spec(0), qkv_spec(1), abt_spec(1),
                  pl.BlockSpec(alog_b.shape, lambda b, s: (0, 0)),
                  pl.BlockSpec(dtb_b.shape, lambda b, s: (0, 0))],
        out_specs=[out_spec(0), out_spec(1)],
        out_shape=[out_sds, out_sds],
        scratch_shapes=per_dir + per_dir,
        compiler_params=_cparams("arbitrary", "arbitrary"),
        name="gated_delta",
    )(qkvn, abt_chunks, qkvn, abt_chunks, alog_b, dtb_b)


def _odd_out_kernel(x_ref, gate_ref, of_ref, ob_ref, z_ref, og_ref, w_ref, g2_ref, sh_ref, sc_ref, rw_ref,
                    out_ref, h_ref, info_ref, cnt_ref, carry_s):
    parts = []
    og = og_ref[...]
    for h in range(GDN_HEADS):
        o = of_ref[h] + ob_ref[h]
        y = _rms(o, og) * _silu(z_ref[:, h * LANES:(h + 1) * LANES].astype(F32))
        parts.append(y.astype(BF16))
    y_all = jnp.concatenate(parts, axis=1)
    x_new = x_ref[...] + gate_ref[0] * _dot(y_all, w_ref[...])
    out_ref[...] = x_new
    hn = _rms(x_new, g2_ref[...]) * (1.0 + sc_ref[0]) + sh_ref[0]
    h_ref[...] = hn
    _route(hn, rw_ref, info_ref, cnt_ref, carry_s)


def _odd_out(x2, mods, row, o_dirs, z, o_norm_g, w_out, norm2_g, router_w_pad, row_off, n_rows):
    d = x2.shape[1]
    tm = ROW_TILE
    return pl.pallas_call(
        _odd_out_kernel,
        grid=(n_rows // tm,),
        in_specs=[pl.BlockSpec((tm, d), lambda i: (i + row_off, 0)),
                  _mod_spec(d, 2, row, row_off),
                  pl.BlockSpec((GDN_HEADS, tm, LANES), lambda i: (0, i + row_off, 0)),
                  pl.BlockSpec((GDN_HEADS, tm, LANES), lambda i: (0, i + row_off, 0)),
                  pl.BlockSpec((tm, z.shape[1]), lambda i: (i + row_off, 0)),
                  pl.BlockSpec((1, LANES), lambda i: (0, 0)),
                  pl.BlockSpec(w_out.shape, lambda i: (0, 0)),
                  pl.BlockSpec((1, d), lambda i: (0, 0)),
                  _mod_spec(d, 3, row, row_off), _mod_spec(d, 4, row, row_off),
                  pl.BlockSpec(router_w_pad.shape, lambda i: (0, 0))],
        out_specs=[pl.BlockSpec((tm, d), lambda i: (i, 0)),
                   pl.BlockSpec((tm, d), lambda i: (i, 0)),
                   pl.BlockSpec((tm, LANES), lambda i: (i, 0)),
                   pl.BlockSpec((8, LANES), lambda i: (0, 0))],
        out_shape=[jax.ShapeDtypeStruct((n_rows, d), F32),
                   jax.ShapeDtypeStruct((n_rows, d), F32),
                   jax.ShapeDtypeStruct((n_rows, LANES), F32),
                   jax.ShapeDtypeStruct((8, LANES), F32)],
        scratch_shapes=[pltpu.VMEM((8, LANES), F32)],
        compiler_params=_cparams("arbitrary"),
        name="odd_out_router",
    )(x2, mods, o_dirs[0], o_dirs[1], z, o_norm_g, w_out, norm2_g, mods, mods, router_w_pad)


def _route(hn, rw_ref, info_ref, cnt_ref, carry_s):
    i = pl.program_id(0)

    @pl.when(i == 0)
    def _():
        carry_s[...] = jnp.zeros(carry_s.shape, carry_s.dtype)

    tm = hn.shape[0]
    lane = lax.broadcasted_iota(jnp.int32, (tm, LANES), 1)
    logits = jnp.where(lane < N_EXPERTS, _dot_f32(hn, rw_ref[...]), -jnp.inf)
    m1 = jnp.max(logits, axis=1, keepdims=True)
    i1 = jnp.min(jnp.where(logits == m1, lane, LANES), axis=1, keepdims=True)
    rest = jnp.where(lane == i1, -jnp.inf, logits)
    m2 = jnp.max(rest, axis=1, keepdims=True)
    i2 = jnp.min(jnp.where(rest == m2, lane, LANES), axis=1, keepdims=True)
    e2 = jnp.exp(m2 - m1)
    w1 = 1.0 / (1.0 + e2)
    w2 = e2 / (1.0 + e2)
    oh1 = (lane == i1).astype(F32)
    oh2 = (lane == i2).astype(F32)
    cnt = oh1 + oh2
    ri = lax.broadcasted_iota(jnp.int32, (tm, tm), 0)
    ci = lax.broadcasted_iota(jnp.int32, (tm, tm), 1)
    before = (ci < ri).astype(BF16)
    pre = _dot(before, cnt.astype(BF16)) + carry_s[0:1, :]
    r1 = jnp.sum(pre * oh1, axis=1, keepdims=True)
    r2 = jnp.sum(pre * oh2, axis=1, keepdims=True)
    info = jnp.where(lane == 0, i1.astype(F32), 0.0)
    info = jnp.where(lane == 1, i2.astype(F32), info)
    info = jnp.where(lane == 2, w1, info)
    info = jnp.where(lane == 3, w2, info)
    info = jnp.where(lane == 4, r1, info)
    info = jnp.where(lane == 5, r2, info)
    info_ref[...] = info
    total = carry_s[0:1, :] + jnp.sum(cnt, axis=0, keepdims=True)
    carry_s[...] = jnp.broadcast_to(total, carry_s.shape)
    cnt_ref[...] = jnp.broadcast_to(total, cnt_ref.shape)


def _row_copy(src_hbm, src_row, dst_ref, dst_row, sem):
    return pltpu.make_async_copy(src_hbm.at[pl.ds(src_row, 1)], dst_ref.at[pl.ds(dst_row, 1)], sem)


def _dispatch_kernel(dest_ref, h_ref, init_hbm, hs_hbm, sem, *, tm):
    del init_hbm

    def issue(r, carry):
        _row_copy(h_ref, r, hs_hbm, dest_ref[0, 0, r], sem.at[0]).start()
        _row_copy(h_ref, r, hs_hbm, dest_ref[0, 0, tm + r], sem.at[1]).start()
        return carry

    lax.fori_loop(0, tm, issue, 0, unroll=ROW_DMA_UNROLL)
    for k in range(2):
        pltpu.make_async_copy(h_ref, hs_hbm.at[pl.ds(0, tm)], sem.at[k]).wait()


def _dispatch(h, dest_tiles, n_sorted_rows):
    t, d = h.shape
    tm = ROW_TILE
    init = jnp.zeros((n_sorted_rows, d), h.dtype)
    return pl.pallas_call(
        functools.partial(_dispatch_kernel, tm=tm),
        grid=(t // tm,),
        in_specs=[pl.BlockSpec((1, 1, 2 * tm), lambda i: (i, 0, 0), memory_space=pltpu.SMEM),
                  pl.BlockSpec((tm, d), lambda i: (i, 0)),
                  pl.BlockSpec(memory_space=pl.ANY)],
        out_specs=pl.BlockSpec(memory_space=pl.ANY),
        out_shape=jax.ShapeDtypeStruct((n_sorted_rows, d), h.dtype),
        scratch_shapes=[pltpu.SemaphoreType.DMA((2,))],
        input_output_aliases={2: 0},
        compiler_params=_cparams("arbitrary"),
        name="moe_dispatch",
    )(dest_tiles, h, init)


def _expert_kernel(te_ref, nt_ref, hs_ref, w1_ref, w2_ref, ys_ref, *, ffn, chunk):
    del te_ref
    j = pl.program_id(0)

    @pl.when(j < nt_ref[0])
    def _():
        hb = hs_ref[...].astype(BF16)
        acc = None
        for c0 in range(0, ffn, chunk):
            gate = _dot(hb, w1_ref[:, c0:c0 + chunk])
            up = _dot(hb, w1_ref[:, ffn + c0:ffn + c0 + chunk])
            part = _dot((_silu(gate) * up).astype(BF16), w2_ref[c0:c0 + chunk, :])
            acc = part if acc is None else acc + part
        ys_ref[...] = acc

    @pl.when(j >= nt_ref[0])
    def _():
        ys_ref[...] = jnp.zeros(ys_ref.shape, ys_ref.dtype)


def _experts(hs, tile_expert, n_tiles_used, w1, w2):
    r, d = hs.shape
    tm = EXPERT_TILE
    ffn = w2.shape[1]
    assert ffn % EXPERT_FCHUNK == 0

    def tile(j, nt):
        return jnp.minimum(j, nt[0] - 1)

    grid_spec = pltpu.PrefetchScalarGridSpec(
        num_scalar_prefetch=2,
        grid=(r // tm,),
        in_specs=[pl.BlockSpec((tm, d), lambda j, te, nt: (tile(j, nt), 0)),
                  pl.BlockSpec((None, d, 2 * ffn), lambda j, te, nt: (te[tile(j, nt)], 0, 0)),
                  pl.BlockSpec((None, ffn, d), lambda j, te, nt: (te[tile(j, nt)], 0, 0))],
        out_specs=pl.BlockSpec((tm, d), lambda j, te, nt: (j, 0)),
    )
    return pl.pallas_call(
        functools.partial(_expert_kernel, ffn=ffn, chunk=EXPERT_FCHUNK),
        grid_spec=grid_spec,
        out_shape=jax.ShapeDtypeStruct((r, d), F32),
        compiler_params=pltpu.CompilerParams(dimension_semantics=("arbitrary",),
                                             vmem_limit_bytes=EXPERT_VMEM_LIMIT_BYTES),
        name="moe_experts",
    )(tile_expert, n_tiles_used, hs, w1, w2)


def _combine_kernel(dest_ref, x_ref, gate_ref, info_ref, fg_ref, ys_hbm, o_ref, y1_s, y2_s, sem, *, tm, final_norm):
    def issue(r, carry):
        _row_copy(ys_hbm, dest_ref[0, 0, r], y1_s, r, sem.at[0]).start()
        _row_copy(ys_hbm, dest_ref[0, 0, tm + r], y2_s, r, sem.at[1]).start()
        return carry

    lax.fori_loop(0, tm, issue, 0, unroll=ROW_DMA_UNROLL)
    pltpu.make_async_copy(ys_hbm.at[pl.ds(0, tm)], y1_s, sem.at[0]).wait()
    pltpu.make_async_copy(ys_hbm.at[pl.ds(0, tm)], y2_s, sem.at[1]).wait()
    info = info_ref[...]
    y = info[:, 2:3] * y1_s[...] + info[:, 3:4] * y2_s[...]
    out = x_ref[...] + gate_ref[0] * y
    if final_norm:
        out = _rms(out, fg_ref[...])
    o_ref[...] = out


def _combine(x2, mods, row, info, dest_tiles, ys, final_g, final_norm):
    t, d = x2.shape
    tm = ROW_TILE
    return pl.pallas_call(
        functools.partial(_combine_kernel, tm=tm, final_norm=final_norm),
        grid=(t // tm,),
        in_specs=[pl.BlockSpec((1, 1, 2 * tm), lambda i: (i, 0, 0), memory_space=pltpu.SMEM),
                  pl.BlockSpec((tm, d), lambda i: (i, 0)),
                  _mod_spec(d, 5, row),
                  pl.BlockSpec((tm, LANES), lambda i: (i, 0)),
                  pl.BlockSpec((1, d), lambda i: (0, 0)),
                  pl.BlockSpec(memory_space=pl.ANY)],
        out_specs=pl.BlockSpec((tm, d), lambda i: (i, 0)),
        out_shape=jax.ShapeDtypeStruct((t, d), F32),
        scratch_shapes=[pltpu.VMEM((tm, d), F32), pltpu.VMEM((tm, d), F32), pltpu.SemaphoreType.DMA((2,))],
        compiler_params=_cparams("arbitrary"),
        name="moe_combine",
    )(dest_tiles, x2, mods, info, final_g, ys)


def _moe(x2, h, info, counts, mods, row, w1, w2, final_g, final_norm):
    t, d = x2.shape
    tm = ROW_TILE
    te = EXPERT_TILE
    cnt = counts[0, :N_EXPERTS].astype(jnp.int32)
    gsz = ((cnt + te - 1) // te) * te
    ends = jnp.cumsum(gsz)
    offs = ends - gsz
    e1 = info[:, 0].astype(jnp.int32)
    e2 = info[:, 1].astype(jnp.int32)
    dest1 = offs[e1] + info[:, 4].astype(jnp.int32)
    dest2 = offs[e2] + info[:, 5].astype(jnp.int32)
    dest_tiles = jnp.concatenate([dest1.reshape(t // tm, 1, tm), dest2.reshape(t // tm, 1, tm)], axis=2)
    n_sorted = 2 * t + N_EXPERTS * te
    n_tiles_max = n_sorted // te
    starts = jnp.arange(n_tiles_max, dtype=jnp.int32) * te
    tile_expert = jnp.minimum(jnp.sum((starts[:, None] >= ends[None, :]).astype(jnp.int32), axis=1), N_EXPERTS - 1)
    n_used = (ends[-1] // te).astype(jnp.int32).reshape(1)
    hs = _dispatch(h, dest_tiles, n_sorted)
    ys = _experts(hs, tile_expert, n_used, w1, w2)
    return _combine(x2, mods, row, info, dest_tiles, ys, final_g, final_norm)


_ROPE_SWAP = tuple(list(range(8, 16)) + list(range(0, 8)) + list(range(24, 32)) + list(range(16, 24)))


def _even_weights(w_in, w_uq, w_ukv):
    d = w_in.shape[0]
    n0 = 3 * CONV_CH + Q_RANK + KV_RANK
    kr = w_in[:, n0:n0 + QK_ROPE]
    kr_sw = kr[:, jnp.array(_ROPE_SWAP)]
    z = lambda n: jnp.zeros((d, n), w_in.dtype)
    w_in_x = jnp.concatenate([w_in[:, :n0], z(QK_NOPE), kr, z(LANES - QK_NOPE - QK_ROPE),
                              z(QK_NOPE), kr_sw, z(LANES - QK_NOPE - QK_ROPE)], axis=1).astype(BF16)
    uq = w_uq.reshape(Q_RANK, MLA_HEADS, QK_NOPE + QK_ROPE)
    pad = LANES - QK_NOPE - QK_ROPE
    uq_plain = jnp.pad(uq, ((0, 0), (0, 0), (0, pad))).reshape(Q_RANK, MLA_HEADS * LANES)
    uq_rope_sw = uq[:, :, QK_NOPE:][:, :, jnp.array(_ROPE_SWAP)]
    uq_sw = jnp.pad(uq_rope_sw, ((0, 0), (0, 0), (QK_NOPE, pad))).reshape(Q_RANK, MLA_HEADS * LANES)
    wq = jnp.concatenate([uq_plain, uq_sw], axis=1).astype(BF16)
    ukv = w_ukv.reshape(KV_RANK, MLA_HEADS, QK_NOPE + V_HEAD)
    k_part = jnp.pad(ukv[:, :, :QK_NOPE], ((0, 0), (0, 0), (0, LANES - QK_NOPE))).reshape(KV_RANK, MLA_HEADS * LANES)
    v_part = ukv[:, :, QK_NOPE:].reshape(KV_RANK, MLA_HEADS * V_HEAD)
    wkv = jnp.concatenate([k_part, v_part], axis=1).astype(BF16)
    return w_in_x, wq, wkv


def _rope_tables(seq_len, ctx_rows):
    n = QK_ROPE // 4
    t = jnp.arange(seq_len)
    inv = ROPE_BASE ** (-jnp.arange(n, dtype=F32) / n)
    ang_r = (t // GRID_W).astype(F32)[:, None] * inv
    ang_c = (t % GRID_W).astype(F32)[:, None] * inv
    cr, sr, cc, sc = jnp.cos(ang_r), jnp.sin(ang_r), jnp.cos(ang_c), jnp.sin(ang_c)
    cos32 = jnp.concatenate([cr, cr, cc, cc], axis=1)
    sin32 = jnp.concatenate([-sr, sr, -sc, sc], axis=1)
    pad = LANES - QK_NOPE - QK_ROPE
    cos_x = jnp.concatenate([jnp.ones((seq_len, QK_NOPE), F32), cos32, jnp.zeros((seq_len, pad), F32)], axis=1)
    sin_x = jnp.concatenate([jnp.zeros((seq_len, QK_NOPE), F32), sin32, jnp.zeros((seq_len, pad), F32)], axis=1)
    cos_c = jnp.concatenate([jnp.ones((ctx_rows, QK_NOPE + QK_ROPE), F32), jnp.zeros((ctx_rows, pad), F32)], axis=1)
    sin_c = jnp.zeros((ctx_rows, LANES), F32)
    return jnp.concatenate([cos_c, cos_x], axis=0), jnp.concatenate([sin_c, sin_x], axis=0)


def _even_layer(xa, mods, dims, tabs, p):
    batch, ctx_len, seq_len = dims
    tm = ROW_TILE
    t = xa.shape[0]
    n_ctx_tiles = batch * ctx_len // tm
    x_tps = seq_len // tm
    row = _mod_row_fn(n_ctx_tiles, x_tps, batch)
    w_in_x, wq, wkv = _even_weights(p["w_in"], p["w_uq"], p["w_ukv"])
    cos_t, sin_t = tabs
    tab_idx = lambda i: jnp.where(i < n_ctx_tiles, 0, 1 + (i - n_ctx_tiles) % x_tps)
    conv_y, q, k, v = _even_proj(xa, p["norm1_g"].reshape(1, -1), mods, row, t // tm, w_in_x, p["conv_w"],
                                 p["q_norm_g"].reshape(1, -1), wq, p["kv_norm_g"].reshape(1, -1), wkv,
                                 cos_t, sin_t, tab_idx, n_ctx_tiles, ctx_len // tm, x_tps)
    att = _attention(q, k, v, batch, ctx_len, seq_len)
    w_out = p["w_out"].astype(BF16)
    xa = _resid_mm(xa, mods, 2, row, [conv_y, att], [w_out[:CONV_CH], w_out[CONV_CH:]])
    hmid = _ffn_up(xa, p["norm2_g"].reshape(1, -1), mods, row, p["ffn_w1"].astype(BF16))
    return _resid_mm(xa, mods, 5, row, [hmid], [p["ffn_w2"].astype(BF16)])


def _odd_layer(xa, mods, dims, p, ctx_out, final_g, final_norm):
    batch, ctx_len, seq_len = dims
    tm = ROW_TILE
    t = xa.shape[0]
    n_ctx_tiles = batch * ctx_len // tm
    x_tps = seq_len // tm
    row = _mod_row_fn(n_ctx_tiles, x_tps, batch)
    n_w = p["w_in"].shape[1]
    w_in = jnp.pad(p["w_in"], ((0, 0), (0, 4 * GDN_QK + LANES - n_w))).astype(BF16)
    qkvn, z, abt = _odd_proj(xa, p["norm1_g"].reshape(1, -1), mods, row, w_in, p["qkv_conv_w"],
                             n_ctx_tiles, ctx_len // tm, x_tps)
    abt_chunks = abt[:, :4 * GDN_HEADS].reshape(t // GDN_STEP, GDN_STEP, 4 * GDN_HEADS).transpose(0, 2, 1)
    alog_b = jnp.broadcast_to(p["a_log"].reshape(2 * GDN_HEADS, 1), (2 * GDN_HEADS, GDN_STEP))
    dtb_b = jnp.broadcast_to(p["dt_bias"].reshape(2 * GDN_HEADS, 1), (2 * GDN_HEADS, GDN_STEP))
    o_dirs = _gdn(qkvn, abt_chunks, alog_b, dtb_b, batch, ctx_len, seq_len)
    row_off = 0 if ctx_out else n_ctx_tiles
    n_rows = t - row_off * tm
    rw = jnp.pad(p["router_w"], ((0, 0), (0, LANES - N_EXPERTS)))
    xo, h, info, counts = _odd_out(xa, mods, row, o_dirs, z, p["o_norm_g"].reshape(1, -1), p["w_out"].astype(BF16),
                                   p["norm2_g"].reshape(1, -1), rw, row_off, n_rows)
    row_o = row if ctx_out else _mod_row_fn(0, x_tps, batch)
    return _moe(xo, h, info, counts, mods, row_o, p["moe_w1"].astype(BF16), p["moe_w2"].astype(BF16),
                final_g, final_norm)


_EVEN_NAMES = ("mod_w", "mod_b", "norm1_g", "w_in", "conv_w", "q_norm_g", "w_uq", "kv_norm_g", "w_ukv", "w_out",
               "norm2_g", "ffn_w1", "ffn_w2")
_ODD_NAMES = ("mod_w", "mod_b", "norm1_g", "w_in", "qkv_conv_w", "a_log", "dt_bias", "o_norm_g", "w_out",
              "norm2_g", "router_w", "moe_w1", "moe_w2")


def kernel(x, c, ctx, c_ctx, l0_mod_w, l0_mod_b, l0_norm1_g, l0_w_in, l0_conv_w, l0_q_norm_g, l0_w_uq, l0_kv_norm_g, l0_w_ukv, l0_w_out, l0_norm2_g, l0_ffn_w1, l0_ffn_w2, l1_mod_w, l1_mod_b, l1_norm1_g, l1_w_in, l1_qkv_conv_w, l1_a_log, l1_dt_bias, l1_o_norm_g, l1_w_out, l1_norm2_g, l1_router_w, l1_moe_w1, l1_moe_w2, l2_mod_w, l2_mod_b, l2_norm1_g, l2_w_in, l2_conv_w, l2_q_norm_g, l2_w_uq, l2_kv_norm_g, l2_w_ukv, l2_w_out, l2_norm2_g, l2_ffn_w1, l2_ffn_w2, l3_mod_w, l3_mod_b, l3_norm1_g, l3_w_in, l3_qkv_conv_w, l3_a_log, l3_dt_bias, l3_o_norm_g, l3_w_out, l3_norm2_g, l3_router_w, l3_moe_w1, l3_moe_w2, final_norm_g):
    batch, seq_len, d = x.shape
    ctx_len = ctx.shape[1]
    assert seq_len % ROW_TILE == 0 and ctx_len % ROW_TILE == 0 and seq_len % GRID_W == 0
    dims = (batch, ctx_len, seq_len)
    layers = (
        dict(zip(_EVEN_NAMES, (l0_mod_w, l0_mod_b, l0_norm1_g, l0_w_in, l0_conv_w, l0_q_norm_g, l0_w_uq,
                               l0_kv_norm_g, l0_w_ukv, l0_w_out, l0_norm2_g, l0_ffn_w1, l0_ffn_w2))),
        dict(zip(_ODD_NAMES, (l1_mod_w, l1_mod_b, l1_norm1_g, l1_w_in, l1_qkv_conv_w, l1_a_log, l1_dt_bias,
                              l1_o_norm_g, l1_w_out, l1_norm2_g, l1_router_w, l1_moe_w1, l1_moe_w2))),
        dict(zip(_EVEN_NAMES, (l2_mod_w, l2_mod_b, l2_norm1_g, l2_w_in, l2_conv_w, l2_q_norm_g, l2_w_uq,
                               l2_kv_norm_g, l2_w_ukv, l2_w_out, l2_norm2_g, l2_ffn_w1, l2_ffn_w2))),
        dict(zip(_ODD_NAMES, (l3_mod_w, l3_mod_b, l3_norm1_g, l3_w_in, l3_qkv_conv_w, l3_a_log, l3_dt_bias,
                              l3_o_norm_g, l3_w_out, l3_norm2_g, l3_router_w, l3_moe_w1, l3_moe_w2))),
    )
    xa = jnp.concatenate([ctx.reshape(batch * ctx_len, d), x.reshape(batch * seq_len, d)], axis=0)
    mod_rows = 8 * ((batch + 1 + 7) // 8)
    cc = jnp.zeros((mod_rows, d), F32).at[:batch].set(c).at[batch].set(c_ctx)
    tabs = _rope_tables(seq_len, ROW_TILE)
    n_layers = len(layers)
    assert n_layers % 2 == 0
    final_g = final_norm_g.reshape(1, -1)
    for li, p in enumerate(layers):
        mods = _mods(cc, p["mod_w"], p["mod_b"])
        last = li == n_layers - 1
        if li % 2 == 0:
            xa = _even_layer(xa, mods, dims, tabs, p)
        else:
            xa = _odd_layer(xa, mods, dims, p, not last, final_g, last)
    return xa.reshape(batch, seq_len, d)
```

```python
import functools
import math

import jax
import jax.numpy as jnp
from jax import lax
from jax.experimental import pallas as pl
from jax.experimental.pallas import tpu as pltpu

F32 = jnp.float32
BF16 = jnp.bfloat16

NORM_EPS = 1e-6
N_MOD = 6
GRID_W = 64
ROPE_BASE = 10000.0

CONV_CH = 512
MLA_HEADS = 8
Q_RANK = 256
KV_RANK = 128
QK_NOPE = 64
QK_ROPE = 32
V_HEAD = 64
MLA_SCALE = (QK_NOPE + QK_ROPE) ** -0.5

GDN_HEADS = 8
GDN_DK = 128
GDN_CHUNK = 64
GDN_QK = GDN_HEADS * GDN_DK

N_EXPERTS = 8

LANES = 128
ROW_TILE = 256
ATTN_HEAD_GROUP = 4
GDN_STEP = 256
GDN_INV_BASE = 8
GDN_HEAD_GROUP = 8
ROW_DMA_UNROLL = 8
EXPERT_TILE = 512
EXPERT_FCHUNK = 512
VMEM_LIMIT_BYTES = 56 * 1024 * 1024
EXPERT_VMEM_LIMIT_BYTES = 62 * 1024 * 1024


def _cparams(*sem):
    return pltpu.CompilerParams(dimension_semantics=sem, vmem_limit_bytes=VMEM_LIMIT_BYTES)


def _silu(x):
    return x * (1.0 / (1.0 + jnp.exp(-x)))


def _rms(x, g):
    return x * lax.rsqrt(jnp.mean(x * x, axis=-1, keepdims=True) + NORM_EPS) * g


def _dot(a, b):
    return jnp.dot(a, b, preferred_element_type=F32)


def _dot_nt(a, b):
    return lax.dot_general(a, b, (((1,), (1,)), ((), ())), preferred_element_type=F32)


def _dot_tn(a, b):
    return lax.dot_general(a, b, (((0,), (0,)), ((), ())), preferred_element_type=F32)


def _split3(x):
    x1 = x.astype(BF16)
    r1 = x - x1.astype(F32)
    x2 = r1.astype(BF16)
    x3 = (r1 - x2.astype(F32)).astype(BF16)
    return x1, x2, x3


def _dot_f32(a, b):
    a1, a2, a3 = _split3(a)
    b1, b2, b3 = _split3(b)
    acc = _dot(a1, b3) + _dot(a2, b2) + _dot(a3, b1)
    acc = acc + _dot(a1, b2) + _dot(a2, b1)
    return acc + _dot(a1, b1)


def _dot_f32_exact_rhs(a, m_bf16):
    a1, a2, a3 = _split3(a)
    return _dot(a3, m_bf16) + _dot(a2, m_bf16) + _dot(a1, m_bf16)


def _mod_row_fn(n_ctx_tiles, tiles_per_seq, batch):
    def row(i):
        return jnp.where(i < n_ctx_tiles, batch, (i - n_ctx_tiles) // tiles_per_seq)
    return row


def _mods_kernel(c_ref, w_ref, b_ref, o_ref):
    o_ref[...] = _dot_f32(_silu(c_ref[...]), w_ref[...]) + b_ref[...]


def _mods(cc, mod_w, mod_b):
    rows, d = cc.shape
    n = mod_w.shape[1]
    out = pl.pallas_call(
        _mods_kernel,
        grid=(n // d,),
        in_specs=[pl.BlockSpec((rows, d), lambda j: (0, 0)),
                  pl.BlockSpec((d, d), lambda j: (0, j)),
                  pl.BlockSpec((1, d), lambda j: (0, j))],
        out_specs=pl.BlockSpec((rows, d), lambda j: (0, j)),
        out_shape=jax.ShapeDtypeStruct((rows, n), F32),
        compiler_params=_cparams("arbitrary"),
        name="adaln_mods",
    )(cc, mod_w, mod_b.reshape(1, n))
    return out.reshape(rows, 1, n)


def _mod_spec(d, k, row, off=0):
    return pl.BlockSpec((1, 1, d), lambda i: (row(i + off), 0, k))


def _even_proj_kernel(x_ref, xp_ref, xn_ref, g_ref, sh_ref, sc_ref, win_ref, cw_ref, qg_ref, wq_ref, kvg_ref, wkv_ref,
                      cos_ref, sin_ref, conv_ref, q_ref, k_ref, v_ref, *, n_ctx_tiles, ctx_tps, x_tps):
    i = pl.program_id(0)
    keep_prev, keep_next = _seq_edges(i, n_ctx_tiles, ctx_tps, x_tps)
    tm = x_ref.shape[0]
    xs = jnp.concatenate([x_ref[...], xp_ref[...], xn_ref[...]], axis=0)
    hb_all = (_rms(xs, g_ref[...]) * (1.0 + sc_ref[0]) + sh_ref[0]).astype(BF16)
    hb = hb_all[:tm]
    c = CONV_CH
    n_gbu = 3 * c
    z_all = _dot(hb_all, win_ref[:, c:2 * c]) * _dot(hb_all, win_ref[:, 2 * c:n_gbu])
    z = z_all[:tm]
    zm, zp = _shifted(z, z_all[tm + X_HALO - 1:tm + X_HALO] * keep_prev, z_all[tm + X_HALO:tm + X_HALO + 1] * keep_next)
    cw = cw_ref[...]
    conv = zm * cw[0:1] + z * cw[1:2] + zp * cw[2:3]
    conv_ref[...] = (_dot(hb, win_ref[:, :c]) * conv).astype(conv_ref.dtype)
    rest = _dot(hb, win_ref[:, n_gbu:])
    cq = rest[:, :Q_RANK]
    ckv = rest[:, Q_RANK:Q_RANK + KV_RANK]
    kr = rest[:, Q_RANK + KV_RANK:Q_RANK + KV_RANK + LANES]
    kr_sw = rest[:, Q_RANK + KV_RANK + LANES:]
    cos = cos_ref[...]
    sin = sin_ref[...]
    kr_rot = kr * cos + kr_sw * sin
    qq = _dot(_rms(cq, qg_ref[...]).astype(BF16), wq_ref[...])
    kv = _dot(_rms(ckv, kvg_ref[...]).astype(BF16), wkv_ref[...])
    hw = MLA_HEADS * LANES
    for h in range(MLA_HEADS):
        sl = slice(h * LANES, (h + 1) * LANES)
        qh = (qq[:, sl] * cos + qq[:, hw + h * LANES:hw + (h + 1) * LANES] * sin) * MLA_SCALE
        q_ref[:, sl] = qh.astype(q_ref.dtype)
        k_ref[:, sl] = (kv[:, sl] + kr_rot).astype(k_ref.dtype)
    v_ref[...] = kv[:, hw:].astype(v_ref.dtype)


def _even_proj(x2, g, mods, row, n_tiles, w_in, conv_w, qg, wq, kvg, wkv, cos_t, sin_t, tab_idx,
               n_ctx_tiles, ctx_tps, x_tps):
    t, d = x2.shape
    tm = ROW_TILE
    n_in = w_in.shape[1]
    hw = MLA_HEADS * LANES
    const = lambda i: (0, 0)
    r = tm // X_HALO
    nb = t // X_HALO
    return pl.pallas_call(
        functools.partial(_even_proj_kernel, n_ctx_tiles=n_ctx_tiles, ctx_tps=ctx_tps, x_tps=x_tps),
        grid=(n_tiles,),
        in_specs=[pl.BlockSpec((tm, d), lambda i: (i, 0)),
                  pl.BlockSpec((X_HALO, d), lambda i: (jnp.maximum(i * r - 1, 0), 0)),
                  pl.BlockSpec((X_HALO, d), lambda i: (jnp.minimum((i + 1) * r, nb - 1), 0)),
                  pl.BlockSpec((1, d), const),
                  _mod_spec(d, 0, row), _mod_spec(d, 1, row),
                  pl.BlockSpec((d, n_in), const),
                  pl.BlockSpec(conv_w.shape, const),
                  pl.BlockSpec((1, Q_RANK), const),
                  pl.BlockSpec(wq.shape, const),
                  pl.BlockSpec((1, KV_RANK), const),
                  pl.BlockSpec(wkv.shape, const),
                  pl.BlockSpec((tm, LANES), lambda i: (tab_idx(i), 0)),
                  pl.BlockSpec((tm, LANES), lambda i: (tab_idx(i), 0))],
        out_specs=[pl.BlockSpec((tm, CONV_CH), lambda i: (i, 0)),
                   pl.BlockSpec((tm, hw), lambda i: (i, 0)),
                   pl.BlockSpec((tm, hw), lambda i: (i, 0)),
                   pl.BlockSpec((tm, MLA_HEADS * V_HEAD), lambda i: (i, 0))],
        out_shape=[jax.ShapeDtypeStruct((t, CONV_CH), BF16),
                   jax.ShapeDtypeStruct((t, hw), BF16),
                   jax.ShapeDtypeStruct((t, hw), BF16),
                   jax.ShapeDtypeStruct((t, MLA_HEADS * V_HEAD), BF16)],
        compiler_params=_cparams("arbitrary"),
        name="even_proj",
    )(x2, x2, x2, g, mods, mods, w_in, conv_w, qg, wq, kvg, wkv, cos_t, sin_t)


def _shifted(z, prev_row, next_row):
    tm = z.shape[0]
    ridx = lax.broadcasted_iota(jnp.int32, z.shape, 0)
    zm = jnp.where(ridx == 0, prev_row, pltpu.roll(z, 1, 0))
    zp = jnp.where(ridx == tm - 1, next_row, pltpu.roll(z, tm - 1, 0))
    return zm, zp


def _seq_edges(i, n_ctx_tiles, ctx_tps, x_tps):
    j = jnp.where(i < n_ctx_tiles, i % ctx_tps, (i - n_ctx_tiles) % x_tps)
    tps = jnp.where(i < n_ctx_tiles, ctx_tps, x_tps)
    keep_prev = (j != 0).astype(F32)
    keep_next = (j != tps - 1).astype(F32)
    return keep_prev, keep_next


def _attn_kernel(q_ref, kc_ref, kx_ref, vc_ref, vx_ref, o_ref, *, n_ctx_q):
    qi = pl.program_id(1)

    def run(use_x):
        for g0 in range(0, MLA_HEADS, ATTN_HEAD_GROUP):
            hs = range(g0, g0 + ATTN_HEAD_GROUP)
            ks = {h: slice(h * LANES, (h + 1) * LANES) for h in hs}
            vs = {h: slice(h * V_HEAD, (h + 1) * V_HEAD) for h in hs}
            q = {h: q_ref[:, ks[h]] for h in hs}
            sc = {h: _dot_nt(q[h], kc_ref[:, ks[h]]) for h in hs}
            m = {h: jnp.max(sc[h], axis=1, keepdims=True) for h in hs}
            if use_x:
                sx = {h: _dot_nt(q[h], kx_ref[:, ks[h]]) for h in hs}
                m = {h: jnp.maximum(m[h], jnp.max(sx[h], axis=1, keepdims=True)) for h in hs}
            pc = {h: jnp.exp(sc[h] - m[h]) for h in hs}
            l = {h: jnp.sum(pc[h], axis=1, keepdims=True) for h in hs}
            o = {h: _dot(pc[h].astype(BF16), vc_ref[:, vs[h]]) for h in hs}
            if use_x:
                px = {h: jnp.exp(sx[h] - m[h]) for h in hs}
                l = {h: l[h] + jnp.sum(px[h], axis=1, keepdims=True) for h in hs}
                o = {h: o[h] + _dot(px[h].astype(BF16), vx_ref[:, vs[h]]) for h in hs}
            outs = [o[h] * (1.0 / l[h]) for h in hs]
            for j in range(0, ATTN_HEAD_GROUP, 2):
                c0 = (g0 + j) * V_HEAD
                o_ref[:, c0:c0 + 2 * V_HEAD] = jnp.concatenate(outs[j:j + 2], axis=1).astype(o_ref.dtype)

    @pl.when(qi < n_ctx_q)
    def _():
        run(False)

    @pl.when(qi >= n_ctx_q)
    def _():
        run(True)


def _attention(q, k, v, batch, ctx_len, seq_len):
    tq = ROW_TILE
    n_ctx_q = ctx_len // tq
    n_x_q = seq_len // tq
    ctx_blocks = batch * ctx_len // tq
    assert (batch * ctx_len) % seq_len == 0
    x_blk0 = batch * ctx_len // seq_len
    hw = MLA_HEADS * LANES
    vw = MLA_HEADS * V_HEAD

    def qrow(b, qi):
        return jnp.where(qi < n_ctx_q, b * n_ctx_q + qi, ctx_blocks + b * n_x_q + (qi - n_ctx_q))

    return pl.pallas_call(
        functools.partial(_attn_kernel, n_ctx_q=n_ctx_q),
        grid=(batch, n_ctx_q + n_x_q),
        in_specs=[pl.BlockSpec((tq, hw), lambda b, qi: (qrow(b, qi), 0)),
                  pl.BlockSpec((ctx_len, hw), lambda b, qi: (b, 0)),
                  pl.BlockSpec((seq_len, hw), lambda b, qi: (x_blk0 + b, 0)),
                  pl.BlockSpec((ctx_len, vw), lambda b, qi: (b, 0)),
                  pl.BlockSpec((seq_len, vw), lambda b, qi: (x_blk0 + b, 0))],
        out_specs=pl.BlockSpec((tq, vw), lambda b, qi: (qrow(b, qi), 0)),
        out_shape=jax.ShapeDtypeStruct((q.shape[0], vw), BF16),
        compiler_params=_cparams("arbitrary", "arbitrary"),
        name="mla_attention",
    )(q, k, k, v, v)


def _resid_mm_kernel(*refs, n_lhs):
    x_ref, gate_ref = refs[0], refs[1]
    lhs = refs[2:2 + n_lhs]
    ws = refs[2 + n_lhs:2 + 2 * n_lhs]
    o_ref = refs[2 + 2 * n_lhs]
    acc = _dot(lhs[0][...], ws[0][...])
    for a_ref, w_ref in zip(lhs[1:], ws[1:]):
        acc = acc + _dot(a_ref[...], w_ref[...])
    o_ref[...] = x_ref[...] + gate_ref[0] * acc


def _resid_mm(x2, mods, k_gate, row, lhs_list, w_list, tm=512):
    t, d = x2.shape
    n = len(lhs_list)
    in_specs = [pl.BlockSpec((tm, d), lambda i: (i, 0)), _mod_spec(d, k_gate, lambda i: row(i * tm // ROW_TILE))]
    in_specs += [pl.BlockSpec((tm, a.shape[1]), lambda i: (i, 0)) for a in lhs_list]
    in_specs += [pl.BlockSpec(w.shape, lambda i: (0, 0)) for w in w_list]
    return pl.pallas_call(
        functools.partial(_resid_mm_kernel, n_lhs=n),
        grid=(t // tm,),
        in_specs=in_specs,
        out_specs=pl.BlockSpec((tm, d), lambda i: (i, 0)),
        out_shape=jax.ShapeDtypeStruct((t, d), F32),
        compiler_params=_cparams("arbitrary"),
        name="resid_matmul",
    )(x2, mods, *lhs_list, *w_list)


def _ffn_up_kernel(x_ref, g_ref, sh_ref, sc_ref, w_ref, o_ref, *, ffn, chunk):
    hn = _rms(x_ref[...], g_ref[...]) * (1.0 + sc_ref[0]) + sh_ref[0]
    hb = hn.astype(BF16)
    for c0 in range(0, ffn, chunk):
        gate = _dot(hb, w_ref[:, c0:c0 + chunk])
        up = _dot(hb, w_ref[:, ffn + c0:ffn + c0 + chunk])
        o_ref[:, c0:c0 + chunk] = (_silu(gate) * up).astype(o_ref.dtype)


def _ffn_up(x2, g, mods, row, w1):
    t, d = x2.shape
    tm = ROW_TILE
    ffn = w1.shape[1] // 2
    chunk = 256
    assert ffn % chunk == 0
    return pl.pallas_call(
        functools.partial(_ffn_up_kernel, ffn=ffn, chunk=chunk),
        grid=(t // tm,),
        in_specs=[pl.BlockSpec((tm, d), lambda i: (i, 0)),
                  pl.BlockSpec((1, d), lambda i: (0, 0)),
                  _mod_spec(d, 3, row), _mod_spec(d, 4, row),
                  pl.BlockSpec(w1.shape, lambda i: (0, 0))],
        out_specs=pl.BlockSpec((tm, ffn), lambda i: (i, 0)),
        out_shape=jax.ShapeDtypeStruct((t, ffn), BF16),
        compiler_params=_cparams("arbitrary"),
        name="ffn_up",
    )(x2, g, mods, mods, w1)


X_HALO = 8


def _odd_proj_kernel(x_ref, xp_ref, xn_ref, g_ref, sh_ref, sc_ref, w_ref, cw_ref, qkv_ref, gate_ref, abt_ref,
                     *, chunk, n_ctx_tiles, ctx_tps, x_tps):
    i = pl.program_id(0)
    keep_prev, keep_next = _seq_edges(i, n_ctx_tiles, ctx_tps, x_tps)
    tm = x_ref.shape[0]
    xs = jnp.concatenate([x_ref[...], xp_ref[...], xn_ref[...]], axis=0)
    hb = (_rms(xs, g_ref[...]) * (1.0 + sc_ref[0]) + sh_ref[0]).astype(BF16)
    n_qkv = qkv_ref.shape[0] * LANES
    n_gate = gate_ref.shape[1]
    cw = cw_ref[...]
    for c0 in range(0, n_qkv, chunk):
        z_all = _dot(hb, w_ref[:, c0:c0 + chunk])
        z = z_all[:tm]
        zprev = z_all[tm + X_HALO - 1:tm + X_HALO] * keep_prev
        znext = z_all[tm + X_HALO:tm + X_HALO + 1] * keep_next
        zm, zp = _shifted(z, zprev, znext)
        y = _silu(zm * cw[0:1, c0:c0 + chunk] + z * cw[1:2, c0:c0 + chunk] + zp * cw[2:3, c0:c0 + chunk])
        for hh in range(chunk // LANES):
            head = c0 // LANES + hh
            yh = y[:, hh * LANES:(hh + 1) * LANES]
            if head < 2 * GDN_HEADS:
                nrm = lax.rsqrt(jnp.sum(yh * yh, axis=-1, keepdims=True) + NORM_EPS)
                if head < GDN_HEADS:
                    nrm = nrm * (GDN_DK ** -0.5)
                yh = yh * nrm
            qkv_ref[head] = yh
    hb_cur = hb[:tm]
    for c0 in range(0, n_gate, chunk):
        gate_ref[:, c0:c0 + chunk] = _dot(hb_cur, w_ref[:, n_qkv + c0:n_qkv + c0 + chunk]).astype(gate_ref.dtype)
    abt_ref[...] = _dot(hb_cur, w_ref[:, n_qkv + n_gate:])


def _odd_proj(x2, g, mods, row, w_in, conv_w, n_ctx_tiles, ctx_tps, x_tps):
    t, d = x2.shape
    tm = ROW_TILE
    n_qkv = 3 * GDN_QK
    n_gate = GDN_QK
    r = tm // X_HALO
    nb = t // X_HALO
    return pl.pallas_call(
        functools.partial(_odd_proj_kernel, chunk=512, n_ctx_tiles=n_ctx_tiles, ctx_tps=ctx_tps, x_tps=x_tps),
        grid=(t // tm,),
        in_specs=[pl.BlockSpec((tm, d), lambda i: (i, 0)),
                  pl.BlockSpec((X_HALO, d), lambda i: (jnp.maximum(i * r - 1, 0), 0)),
                  pl.BlockSpec((X_HALO, d), lambda i: (jnp.minimum((i + 1) * r, nb - 1), 0)),
                  pl.BlockSpec((1, d), lambda i: (0, 0)),
                  _mod_spec(d, 0, row), _mod_spec(d, 1, row),
                  pl.BlockSpec(w_in.shape, lambda i: (0, 0)),
                  pl.BlockSpec(conv_w.shape, lambda i: (0, 0))],
        out_specs=[pl.BlockSpec((n_qkv // LANES, tm, LANES), lambda i: (0, i, 0)),
                   pl.BlockSpec((tm, n_gate), lambda i: (i, 0)),
                   pl.BlockSpec((tm, LANES), lambda i: (i, 0))],
        out_shape=[jax.ShapeDtypeStruct((n_qkv // LANES, t, LANES), F32),
                   jax.ShapeDtypeStruct((t, n_gate), BF16),
                   jax.ShapeDtypeStruct((t, LANES), F32)],
        compiler_params=_cparams("arbitrary"),
        name="odd_proj",
    )(x2, x2, x2, g, mods, mods, w_in, conv_w)


def _gdn_kernel(qkvf_ref, abtf_ref, qkvb_ref, abtb_ref, alog_ref, dtb_ref, of_ref, ob_ref, *scratch):
    fwd, bwd = scratch[:len(scratch) // 2], scratch[len(scratch) // 2:]

    @pl.when(pl.program_id(1) == 0)
    def _():
        fwd[0][...] = jnp.zeros(fwd[0].shape, fwd[0].dtype)
        bwd[0][...] = jnp.zeros(bwd[0].shape, bwd[0].dtype)

    _gdn_body(0, qkvf_ref, abtf_ref, alog_ref, dtb_ref, of_ref, *fwd)
    _gdn_body(1, qkvb_ref, abtb_ref, alog_ref, dtb_ref, ob_ref, *bwd)


def _gdn_body(d, qkv_ref, abt_ref, alog_ref, dtb_ref, o_ref,
              s_ref, gc_s, beta_s, gtot_s, uw_s, qg_s, kd_s, attn_s, vnew_s, gtc_s):
    n = GDN_STEP
    c = GDN_CHUNK
    nchunk = n // c
    h_n = GDN_HEADS
    sgn = 1 - 2 * d
    ri = lax.broadcasted_iota(jnp.int32, (n, n), 0)
    ci = lax.broadcasted_iota(jnp.int32, (n, n), 1)
    same = (ri // c) == (ci // c)
    rel = (ri - ci) * sgn
    incl = same & (rel >= 0)
    strict = same & (rel > 0)
    eye = (ri == ci).astype(F32)

    hrow = pl.ds(d * h_n, h_n)
    a_all = abt_ref[0, hrow, :]
    bt_all = abt_ref[0, pl.ds(2 * h_n + d * h_n, h_n), :]
    sp_in = a_all + dtb_ref[hrow, :]
    softplus = jnp.maximum(sp_in, 0.0) + jnp.log(1.0 + jnp.exp(-jnp.abs(sp_in)))
    g_all = -jnp.exp(alog_ref[hrow, :]) * softplus
    beta_s[...] = 1.0 / (1.0 + jnp.exp(-bt_all))
    cum_m = jnp.where(same & ((ci - ri) * sgn >= 0), 1.0, 0.0).astype(BF16)
    gc_s[...] = _dot_f32_exact_rhs(g_all, cum_m)
    gtot_s[...] = _dot_f32_exact_rhs(g_all, same.astype(BF16))

    grp = uw_s.shape[0]

    def head_group(gi, carry):
        heads = [gi * grp + u for u in range(grp)]
        us = range(grp)
        q = [qkv_ref[h] for h in heads]
        k = [qkv_ref[h_n + h] for h in heads]
        v = [qkv_ref[2 * h_n + h] for h in heads]
        gc_row = [gc_s[pl.ds(h, 1), :] for h in heads]
        gc_col = [jnp.sum(eye * gc_row[u], axis=1, keepdims=True) for u in us]
        beta_col = [jnp.sum(eye * beta_s[pl.ds(h, 1), :], axis=1, keepdims=True) for h in heads]
        gtot_col = [jnp.sum(eye * gtot_s[pl.ds(h, 1), :], axis=1, keepdims=True) for h in heads]
        decay = [jnp.exp(jnp.where(incl, gc_col[u] - gc_row[u], -1e30)) for u in us]
        kb = [k[u] * beta_col[u] for u in us]
        k16 = [k[u].astype(BF16) for u in us]
        p = [jnp.where(strict, -(_dot_nt(kb[u].astype(BF16), k16[u]) * decay[u]), 0.0) for u in us]
        for u in us:
            attn_s[u] = (_dot_nt(q[u].astype(BF16), k16[u]) * decay[u]).astype(attn_s.dtype)
            gtc_s[u] = jnp.broadcast_to(gtot_col[u], gtc_s.shape[1:])
        base = GDN_INV_BASE
        in_base = (ri // base) == (ci // base)
        pb = [jnp.where(in_base, p[u], 0.0) for u in us]
        tinv = [eye + pb[u] for u in us]
        pw = [pb[u].astype(BF16) for u in us]
        pw = [_dot(pw[u], pw[u]) for u in us]
        for _ in range((base - 1).bit_length() - 2):
            p16 = [pw[u].astype(BF16) for u in us]
            tp = [_dot(jnp.concatenate([tinv[u].astype(BF16), p16[u]], axis=0), p16[u]) for u in us]
            tinv = [tinv[u] + tp[u][:n] for u in us]
            pw = [tp[u][n:] for u in us]
        tinv = [tinv[u] + _dot(tinv[u].astype(BF16), pw[u].astype(BF16)) for u in us]
        size = base
        while size < c:
            off = ((ri // (2 * size)) == (ci // (2 * size))) & ((ri // size) != (ci // size))
            t16 = [tinv[u].astype(BF16) for u in us]
            x = [_dot(jnp.where(off, p[u], 0.0).astype(BF16), t16[u]) for u in us]
            tinv = [tinv[u] + _dot(t16[u], x[u].astype(BF16)) for u in us]
            size *= 2
        egc = [jnp.exp(gc_col[u]) for u in us]
        for u in us:
            rhs = jnp.concatenate([v[u] * beta_col[u], kb[u] * egc[u]], axis=1).astype(BF16)
            uw_s[u] = _dot(tinv[u].astype(BF16), rhs)
            qg_s[u] = q[u] * egc[u]
            kd_s[u] = k[u] * jnp.exp(gtot_col[u] - gc_col[u])
            vnew_s[u] = jnp.zeros(vnew_s.shape[1:], vnew_s.dtype)
        s = [s_ref[h] for h in heads]
        for cstep in range(nchunk):
            r0 = (cstep if d == 0 else nchunk - 1 - cstep) * c
            rows = pl.ds(r0, c)
            ws_qs = [_dot(jnp.concatenate([uw_s[u, rows, LANES:], qg_s[u, rows, :]], axis=0).astype(BF16),
                          s[u].astype(BF16)) for u in us]
            v_new = [uw_s[u, rows, :LANES] - ws_qs[u][:c] for u in us]
            for u in us:
                vnew_s[u, rows, :] = v_new[u].astype(vnew_s.dtype)
            for u in us:
                o_ref[heads[u], rows, :] = ws_qs[u][c:] + _dot(attn_s[u, rows, :], vnew_s[u])
            s = [s[u] * jnp.exp(gtc_s[u, pl.ds(r0, 1), :])
                 + _dot_tn(kd_s[u, rows, :].astype(BF16), v_new[u].astype(BF16)) for u in us]
        for u in us:
            s_ref[heads[u]] = s[u]
        return carry

    lax.fori_loop(0, h_n // grp, head_group, 0)


def _gdn(qkvn, abt_chunks, alog_b, dtb_b, batch, ctx_len, seq_len):
    nh3, t, _ = qkvn.shape
    n = GDN_STEP
    nc = ctx_len // n
    nx = seq_len // n
    ctx_blocks = batch * nc

    def rowblk(b, d, s):
        sc = jnp.where(d == 0, s, nc - 1 - s)
        sx = jnp.where(d == 0, s - nc, nx - 1 - (s - nc))
        return jnp.where(s < nc, b * nc + sc, ctx_blocks + b * nx + sx)

    vm = pltpu.VMEM
    g = GDN_HEAD_GROUP
    per_dir = [vm((GDN_HEADS, GDN_DK, LANES), F32),
               vm((GDN_HEADS, n), F32),
               vm((GDN_HEADS, n), F32),
               vm((GDN_HEADS, n), F32),
               vm((g, n, 2 * LANES), F32),
               vm((g, n, LANES), F32),
               vm((g, n, LANES), F32),
               vm((g, n, n), BF16),
               vm((g, n, LANES), BF16),
               vm((g, n, LANES), F32)]
    qkv_spec = lambda d: pl.BlockSpec((nh3, n, LANES), lambda b, s: (0, rowblk(b, d, s), 0))
    abt_spec = lambda d: pl.BlockSpec((1, 4 * GDN_HEADS, n), lambda b, s: (rowblk(b, d, s), 0, 0))
    out_spec = lambda d: pl.BlockSpec((GDN_HEADS, n, LANES), lambda b, s: (0, rowblk(b, d, s), 0))
    out_sds = jax.ShapeDtypeStruct((GDN_HEADS, t, LANES), F32)
    return pl.pallas_call(
        _gdn_kernel,
        grid=(batch, nc + nx),
        in_specs=[qkv_spec(0), abt_spec(0), qkv_spec(1), abt_spec(1),
                  pl.BlockSpec(alog_b.shape, lambda b, s: (0, 0)),
                  pl.BlockSpec(dtb_b.shape, lambda b, s: (0, 0))],
        out_specs=[out_spec(0), out_spec(1)],
        out_shape=[out_sds, out_sds],
        scratch_shapes=per_dir + per_dir,
        compiler_params=_cparams("arbitrary", "arbitrary"),
        name="gated_delta",
    )(qkvn, abt_chunks, qkvn, abt_chunks, alog_b, dtb_b)


def _odd_out_kernel(x_ref, gate_ref, of_ref, ob_ref, z_ref, og_ref, w_ref, g2_ref, sh_ref, sc_ref, rw_ref,
                    out_ref, h_ref, info_ref, cnt_ref, carry_s):
    parts = []
    og = og_ref[...]
    for h in range(GDN_HEADS):
        o = of_ref[h] + ob_ref[h]
        y = _rms(o, og) * _silu(z_ref[:, h * LANES:(h + 1) * LANES].astype(F32))
        parts.append(y.astype(BF16))
    y_all = jnp.concatenate(parts, axis=1)
    x_new = x_ref[...] + gate_ref[0] * _dot(y_all, w_ref[...])
    out_ref[...] = x_new
    hn = _rms(x_new, g2_ref[...]) * (1.0 + sc_ref[0]) + sh_ref[0]
    h_ref[...] = hn
    _route(hn, rw_ref, info_ref, cnt_ref, carry_s)


def _odd_out(x2, mods, row, o_dirs, z, o_norm_g, w_out, norm2_g, router_w_pad, row_off, n_rows):
    d = x2.shape[1]
    tm = ROW_TILE
    return pl.pallas_call(
        _odd_out_kernel,
        grid=(n_rows // tm,),
        in_specs=[pl.BlockSpec((tm, d), lambda i: (i + row_off, 0)),
                  _mod_spec(d, 2, row, row_off),
                  pl.BlockSpec((GDN_HEADS, tm, LANES), lambda i: (0, i + row_off, 0)),
                  pl.BlockSpec((GDN_HEADS, tm, LANES), lambda i: (0, i + row_off, 0)),
                  pl.BlockSpec((tm, z.shape[1]), lambda i: (i + row_off, 0)),
                  pl.BlockSpec((1, LANES), lambda i: (0, 0)),
                  pl.BlockSpec(w_out.shape, lambda i: (0, 0)),
                  pl.BlockSpec((1, d), lambda i: (0, 0)),
                  _mod_spec(d, 3, row, row_off), _mod_spec(d, 4, row, row_off),
                  pl.BlockSpec(router_w_pad.shape, lambda i: (0, 0))],
        out_specs=[pl.BlockSpec((tm, d), lambda i: (i, 0)),
                   pl.BlockSpec((tm, d), lambda i: (i, 0)),
                   pl.BlockSpec((tm, LANES), lambda i: (i, 0)),
                   pl.BlockSpec((8, LANES), lambda i: (0, 0))],
        out_shape=[jax.ShapeDtypeStruct((n_rows, d), F32),
                   jax.ShapeDtypeStruct((n_rows, d), F32),
                   jax.ShapeDtypeStruct((n_rows, LANES), F32),
                   jax.ShapeDtypeStruct((8, LANES), F32)],
        scratch_shapes=[pltpu.VMEM((8, LANES), F32)],
        compiler_params=_cparams("arbitrary"),
        name="odd_out_router",
    )(x2, mods, o_dirs[0], o_dirs[1], z, o_norm_g, w_out, norm2_g, mods, mods, router_w_pad)


def _route(hn, rw_ref, info_ref, cnt_ref, carry_s):
    i = pl.program_id(0)

    @pl.when(i == 0)
    def _():
        carry_s[...] = jnp.zeros(carry_s.shape, carry_s.dtype)

    tm = hn.shape[0]
    lane = lax.broadcasted_iota(jnp.int32, (tm, LANES), 1)
    logits = jnp.where(lane < N_EXPERTS, _dot_f32(hn, rw_ref[...]), -jnp.inf)
    m1 = jnp.max(logits, axis=1, keepdims=True)
    i1 = jnp.min(jnp.where(logits == m1, lane, LANES), axis=1, keepdims=True)
    rest = jnp.where(lane == i1, -jnp.inf, logits)
    m2 = jnp.max(rest, axis=1, keepdims=True)
    i2 = jnp.min(jnp.where(rest == m2, lane, LANES), axis=1, keepdims=True)
    e2 = jnp.exp(m2 - m1)
    w1 = 1.0 / (1.0 + e2)
    w2 = e2 / (1.0 + e2)
    oh1 = (lane == i1).astype(F32)
    oh2 = (lane == i2).astype(F32)
    cnt = oh1 + oh2
    ri = lax.broadcasted_iota(jnp.int32, (tm, tm), 0)
    ci = lax.broadcasted_iota(jnp.int32, (tm, tm), 1)
    before = (ci < ri).astype(BF16)
    pre = _dot(before, cnt.astype(BF16)) + carry_s[0:1, :]
    r1 = jnp.sum(pre * oh1, axis=1, keepdims=True)
    r2 = jnp.sum(pre * oh2, axis=1, keepdims=True)
    info = jnp.where(lane == 0, i1.astype(F32), 0.0)
    info = jnp.where(lane == 1, i2.astype(F32), info)
    info = jnp.where(lane == 2, w1, info)
    info = jnp.where(lane == 3, w2, info)
    info = jnp.where(lane == 4, r1, info)
    info = jnp.where(lane == 5, r2, info)
    info_ref[...] = info
    total = carry_s[0:1, :] + jnp.sum(cnt, axis=0, keepdims=True)
    carry_s[...] = jnp.broadcast_to(total, carry_s.shape)
    cnt_ref[...] = jnp.broadcast_to(total, cnt_ref.shape)


def _row_copy(src_hbm, src_row, dst_ref, dst_row, sem):
    return pltpu.make_async_copy(src_hbm.at[pl.ds(src_row, 1)], dst_ref.at[pl.ds(dst_row, 1)], sem)


def _dispatch_kernel(dest_ref, h_ref, init_hbm, hs_hbm, sem, *, tm):
    del init_hbm

    def issue(r, carry):
        _row_copy(h_ref, r, hs_hbm, dest_ref[0, 0, r], sem.at[0]).start()
        _row_copy(h_ref, r, hs_hbm, dest_ref[0, 0, tm + r], sem.at[1]).start()
        return carry

    lax.fori_loop(0, tm, issue, 0, unroll=ROW_DMA_UNROLL)
    for k in range(2):
        pltpu.make_async_copy(h_ref, hs_hbm.at[pl.ds(0, tm)], sem.at[k]).wait()


def _dispatch(h, dest_tiles, n_sorted_rows):
    t, d = h.shape
    tm = ROW_TILE
    init = jnp.zeros((n_sorted_rows, d), h.dtype)
    return pl.pallas_call(
        functools.partial(_dispatch_kernel, tm=tm),
        grid=(t // tm,),
        in_specs=[pl.BlockSpec((1, 1, 2 * tm), lambda i: (i, 0, 0), memory_space=pltpu.SMEM),
                  pl.BlockSpec((tm, d), lambda i: (i, 0)),
                  pl.BlockSpec(memory_space=pl.ANY)],
        out_specs=pl.BlockSpec(memory_space=pl.ANY),
        out_shape=jax.ShapeDtypeStruct((n_sorted_rows, d), h.dtype),
        scratch_shapes=[pltpu.SemaphoreType.DMA((2,))],
        input_output_aliases={2: 0},
        compiler_params=_cparams("arbitrary"),
        name="moe_dispatch",
    )(dest_tiles, h, init)


def _expert_kernel(te_ref, nt_ref, hs_ref, w1_ref, w2_ref, ys_ref, *, ffn, chunk):
    del te_ref
    j = pl.program_id(0)

    @pl.when(j < nt_ref[0])
    def _():
        hb = hs_ref[...].astype(BF16)
        acc = None
        for c0 in range(0, ffn, chunk):
            gate = _dot(hb, w1_ref[:, c0:c0 + chunk])
            up = _dot(hb, w1_ref[:, ffn + c0:ffn + c0 + chunk])
            part = _dot((_silu(gate) * up).astype(BF16), w2_ref[c0:c0 + chunk, :])
            acc = part if acc is None else acc + part
        ys_ref[...] = acc

    @pl.when(j >= nt_ref[0])
    def _():
        ys_ref[...] = jnp.zeros(ys_ref.shape, ys_ref.dtype)


def _experts(hs, tile_expert, n_tiles_used, w1, w2):
    r, d = hs.shape
    tm = EXPERT_TILE
    ffn = w2.shape[1]
    assert ffn % EXPERT_FCHUNK == 0

    def tile(j, nt):
        return jnp.minimum(j, nt[0] - 1)

    grid_spec = pltpu.PrefetchScalarGridSpec(
        num_scalar_prefetch=2,
        grid=(r // tm,),
        in_specs=[pl.BlockSpec((tm, d), lambda j, te, nt: (tile(j, nt), 0)),
                  pl.BlockSpec((None, d, 2 * ffn), lambda j, te, nt: (te[tile(j, nt)], 0, 0)),
                  pl.BlockSpec((None, ffn, d), lambda j, te, nt: (te[tile(j, nt)], 0, 0))],
        out_specs=pl.BlockSpec((tm, d), lambda j, te, nt: (j, 0)),
    )
    return pl.pallas_call(
        functools.partial(_expert_kernel, ffn=ffn, chunk=EXPERT_FCHUNK),
        grid_spec=grid_spec,
        out_shape=jax.ShapeDtypeStruct((r, d), F32),
        compiler_params=pltpu.CompilerParams(dimension_semantics=("arbitrary",),
                                             vmem_limit_bytes=EXPERT_VMEM_LIMIT_BYTES),
        name="moe_experts",
    )(tile_expert, n_tiles_used, hs, w1, w2)


def _combine_kernel(dest_ref, x_ref, gate_ref, info_ref, fg_ref, ys_hbm, o_ref, y1_s, y2_s, sem, *, tm, final_norm):
    def issue(r, carry):
        _row_copy(ys_hbm, dest_ref[0, 0, r], y1_s, r, sem.at[0]).start()
        _row_copy(ys_hbm, dest_ref[0, 0, tm + r], y2_s, r, sem.at[1]).start()
        return carry

    lax.fori_loop(0, tm, issue, 0, unroll=ROW_DMA_UNROLL)
    pltpu.make_async_copy(ys_hbm.at[pl.ds(0, tm)], y1_s, sem.at[0]).wait()
    pltpu.make_async_copy(ys_hbm.at[pl.ds(0, tm)], y2_s, sem.at[1]).wait()
    info = info_ref[...]
    y = info[:, 2:3] * y1_s[...] + info[:, 3:4] * y2_s[...]
    out = x_ref[...] + gate_ref[0] * y
    if final_norm:
        out = _rms(out, fg_ref[...])
    o_ref[...] = out


def _combine(x2, mods, row, info, dest_tiles, ys, final_g, final_norm):
    t, d = x2.shape
    tm = ROW_TILE
    return pl.pallas_call(
        functools.partial(_combine_kernel, tm=tm, final_norm=final_norm),
        grid=(t // tm,),
        in_specs=[pl.BlockSpec((1, 1, 2 * tm), lambda i: (i, 0, 0), memory_space=pltpu.SMEM),
                  pl.BlockSpec((tm, d), lambda i: (i, 0)),
                  _mod_spec(d, 5, row),
                  pl.BlockSpec((tm, LANES), lambda i: (i, 0)),
                  pl.BlockSpec((1, d), lambda i: (0, 0)),
                  pl.BlockSpec(memory_space=pl.ANY)],
        out_specs=pl.BlockSpec((tm, d), lambda i: (i, 0)),
        out_shape=jax.ShapeDtypeStruct((t, d), F32),
        scratch_shapes=[pltpu.VMEM((tm, d), F32), pltpu.VMEM((tm, d), F32), pltpu.SemaphoreType.DMA((2,))],
        compiler_params=_cparams("arbitrary"),
        name="moe_combine",
    )(dest_tiles, x2, mods, info, final_g, ys)


def _moe(x2, h, info, counts, mods, row, w1, w2, final_g, final_norm):
    t, d = x2.shape
    tm = ROW_TILE
    te = EXPERT_TILE
    cnt = counts[0, :N_EXPERTS].astype(jnp.int32)
    gsz = ((cnt + te - 1) // te) * te
    ends = jnp.cumsum(gsz)
    offs = ends - gsz
    e1 = info[:, 0].astype(jnp.int32)
    e2 = info[:, 1].astype(jnp.int32)
    dest1 = offs[e1] + info[:, 4].astype(jnp.int32)
    dest2 = offs[e2] + info[:, 5].astype(jnp.int32)
    dest_tiles = jnp.concatenate([dest1.reshape(t // tm, 1, tm), dest2.reshape(t // tm, 1, tm)], axis=2)
    n_sorted = 2 * t + N_EXPERTS * te
    n_tiles_max = n_sorted // te
    starts = jnp.arange(n_tiles_max, dtype=jnp.int32) * te
    tile_expert = jnp.minimum(jnp.sum((starts[:, None] >= ends[None, :]).astype(jnp.int32), axis=1), N_EXPERTS - 1)
    n_used = (ends[-1] // te).astype(jnp.int32).reshape(1)
    hs = _dispatch(h, dest_tiles, n_sorted)
    ys = _experts(hs, tile_expert, n_used, w1, w2)
    return _combine(x2, mods, row, info, dest_tiles, ys, final_g, final_norm)


_ROPE_SWAP = tuple(list(range(8, 16)) + list(range(0, 8)) + list(range(24, 32)) + list(range(16, 24)))


def _even_weights(w_in, w_uq, w_ukv):
    d = w_in.shape[0]
    n0 = 3 * CONV_CH + Q_RANK + KV_RANK
    kr = w_in[:, n0:n0 + QK_ROPE]
    kr_sw = kr[:, jnp.array(_ROPE_SWAP)]
    z = lambda n: jnp.zeros((d, n), w_in.dtype)
    w_in_x = jnp.concatenate([w_in[:, :n0], z(QK_NOPE), kr, z(LANES - QK_NOPE - QK_ROPE),
                              z(QK_NOPE), kr_sw, z(LANES - QK_NOPE - QK_ROPE)], axis=1).astype(BF16)
    uq = w_uq.reshape(Q_RANK, MLA_HEADS, QK_NOPE + QK_ROPE)
    pad = LANES - QK_NOPE - QK_ROPE
    uq_plain = jnp.pad(uq, ((0, 0), (0, 0), (0, pad))).reshape(Q_RANK, MLA_HEADS * LANES)
    uq_rope_sw = uq[:, :, QK_NOPE:][:, :, jnp.array(_ROPE_SWAP)]
    uq_sw = jnp.pad(uq_rope_sw, ((0, 0), (0, 0), (QK_NOPE, pad))).reshape(Q_RANK, MLA_HEADS * LANES)
    wq = jnp.concatenate([uq_plain, uq_sw], axis=1).astype(BF16)
    ukv = w_ukv.reshape(KV_RANK, MLA_HEADS, QK_NOPE + V_HEAD)
    k_part = jnp.pad(ukv[:, :, :QK_NOPE], ((0, 0), (0, 0), (0, LANES - QK_NOPE))).reshape(KV_RANK, MLA_HEADS * LANES)
    v_part = ukv[:, :, QK_NOPE:].reshape(KV_RANK, MLA_HEADS * V_HEAD)
    wkv = jnp.concatenate([k_part, v_part], axis=1).astype(BF16)
    return w_in_x, wq, wkv


def _rope_tables(seq_len, ctx_rows):
    n = QK_ROPE // 4
    t = jnp.arange(seq_len)
    inv = ROPE_BASE ** (-jnp.arange(n, dtype=F32) / n)
    ang_r = (t // GRID_W).astype(F32)[:, None] * inv
    ang_c = (t % GRID_W).astype(F32)[:, None] * inv
    cr, sr, cc, sc = jnp.cos(ang_r), jnp.sin(ang_r), jnp.cos(ang_c), jnp.sin(ang_c)
    cos32 = jnp.concatenate([cr, cr, cc, cc], axis=1)
    sin32 = jnp.concatenate([-sr, sr, -sc, sc], axis=1)
    pad = LANES - QK_NOPE - QK_ROPE
    cos_x = jnp.concatenate([jnp.ones((seq_len, QK_NOPE), F32), cos32, jnp.zeros((seq_len, pad), F32)], axis=1)
    sin_x = jnp.concatenate([jnp.zeros((seq_len, QK_NOPE), F32), sin32, jnp.zeros((seq_len, pad), F32)], axis=1)
    cos_c = jnp.concatenate([jnp.ones((ctx_rows, QK_NOPE + QK_ROPE), F32), jnp.zeros((ctx_rows, pad), F32)], axis=1)
    sin_c = jnp.zeros((ctx_rows, LANES), F32)
    return jnp.concatenate([cos_c, cos_x], axis=0), jnp.concatenate([sin_c, sin_x], axis=0)


def _even_layer(xa, mods, dims, tabs, p):
    batch, ctx_len, seq_len = dims
    tm = ROW_TILE
    t = xa.shape[0]
    n_ctx_tiles = batch * ctx_len // tm
    x_tps = seq_len // tm
    row = _mod_row_fn(n_ctx_tiles, x_tps, batch)
    w_in_x, wq, wkv = _even_weights(p["w_in"], p["w_uq"], p["w_ukv"])
    cos_t, sin_t = tabs
    tab_idx = lambda i: jnp.where(i < n_ctx_tiles, 0, 1 + (i - n_ctx_tiles) % x_tps)
    conv_y, q, k, v = _even_proj(xa, p["norm1_g"].reshape(1, -1), mods, row, t // tm, w_in_x, p["conv_w"],
                                 p["q_norm_g"].reshape(1, -1), wq, p["kv_norm_g"].reshape(1, -1), wkv,
                                 cos_t, sin_t, tab_idx, n_ctx_tiles, ctx_len // tm, x_tps)
    att = _attention(q, k, v, batch, ctx_len, seq_len)
    w_out = p["w_out"].astype(BF16)
    xa = _resid_mm(xa, mods, 2, row, [conv_y, att], [w_out[:CONV_CH], w_out[CONV_CH:]])
    hmid = _ffn_up(xa, p["norm2_g"].reshape(1, -1), mods, row, p["ffn_w1"].astype(BF16))
    return _resid_mm(xa, mods, 5, row, [hmid], [p["ffn_w2"].astype(BF16)])


def _odd_layer(xa, mods, dims, p, ctx_out, final_g, final_norm):
    batch, ctx_len, seq_len = dims
    tm = ROW_TILE
    t = xa.shape[0]
    n_ctx_tiles = batch * ctx_len // tm
    x_tps = seq_len // tm
    row = _mod_row_fn(n_ctx_tiles, x_tps, batch)
    n_w = p["w_in"].shape[1]
    w_in = jnp.pad(p["w_in"], ((0, 0), (0, 4 * GDN_QK + LANES - n_w))).astype(BF16)
    qkvn, z, abt = _odd_proj(xa, p["norm1_g"].reshape(1, -1), mods, row, w_in, p["qkv_conv_w"],
                             n_ctx_tiles, ctx_len // tm, x_tps)
    abt_chunks = abt[:, :4 * GDN_HEADS].reshape(t // GDN_STEP, GDN_STEP, 4 * GDN_HEADS).transpose(0, 2, 1)
    alog_b = jnp.broadcast_to(p["a_log"].reshape(2 * GDN_HEADS, 1), (2 * GDN_HEADS, GDN_STEP))
    dtb_b = jnp.broadcast_to(p["dt_bias"].reshape(2 * GDN_HEADS, 1), (2 * GDN_HEADS, GDN_STEP))
    o_dirs = _gdn(qkvn, abt_chunks, alog_b, dtb_b, batch, ctx_len, seq_len)
    row_off = 0 if ctx_out else n_ctx_tiles
    n_rows = t - row_off * tm
    rw = jnp.pad(p["router_w"], ((0, 0), (0, LANES - N_EXPERTS)))
    xo, h, info, counts = _odd_out(xa, mods, row, o_dirs, z, p["o_norm_g"].reshape(1, -1), p["w_out"].astype(BF16),
                                   p["norm2_g"].reshape(1, -1), rw, row_off, n_rows)
    row_o = row if ctx_out else _mod_row_fn(0, x_tps, batch)
    return _moe(xo, h, info, counts, mods, row_o, p["moe_w1"].astype(BF16), p["moe_w2"].astype(BF16),
                final_g, final_norm)


_EVEN_NAMES = ("mod_w", "mod_b", "norm1_g", "w_in", "conv_w", "q_norm_g", "w_uq", "kv_norm_g", "w_ukv", "w_out",
               "norm2_g", "ffn_w1", "ffn_w2")
_ODD_NAMES = ("mod_w", "mod_b", "norm1_g", "w_in", "qkv_conv_w", "a_log", "dt_bias", "o_norm_g", "w_out",
              "norm2_g", "router_w", "moe_w1", "moe_w2")


def kernel(x, c, ctx, c_ctx, l0_mod_w, l0_mod_b, l0_norm1_g, l0_w_in, l0_conv_w, l0_q_norm_g, l0_w_uq, l0_kv_norm_g, l0_w_ukv, l0_w_out, l0_norm2_g, l0_ffn_w1, l0_ffn_w2, l1_mod_w, l1_mod_b, l1_norm1_g, l1_w_in, l1_qkv_conv_w, l1_a_log, l1_dt_bias, l1_o_norm_g, l1_w_out, l1_norm2_g, l1_router_w, l1_moe_w1, l1_moe_w2, l2_mod_w, l2_mod_b, l2_norm1_g, l2_w_in, l2_conv_w, l2_q_norm_g, l2_w_uq, l2_kv_norm_g, l2_w_ukv, l2_w_out, l2_norm2_g, l2_ffn_w1, l2_ffn_w2, l3_mod_w, l3_mod_b, l3_norm1_g, l3_w_in, l3_qkv_conv_w, l3_a_log, l3_dt_bias, l3_o_norm_g, l3_w_out, l3_norm2_g, l3_router_w, l3_moe_w1, l3_moe_w2, final_norm_g):
    batch, seq_len, d = x.shape
    ctx_len = ctx.shape[1]
    assert seq_len % ROW_TILE == 0 and ctx_len % ROW_TILE == 0 and seq_len % GRID_W == 0
    dims = (batch, ctx_len, seq_len)
    layers = (
        dict(zip(_EVEN_NAMES, (l0_mod_w, l0_mod_b, l0_norm1_g, l0_w_in, l0_conv_w, l0_q_norm_g, l0_w_uq,
                               l0_kv_norm_g, l0_w_ukv, l0_w_out, l0_norm2_g, l0_ffn_w1, l0_ffn_w2))),
        dict(zip(_ODD_NAMES, (l1_mod_w, l1_mod_b, l1_norm1_g, l1_w_in, l1_qkv_conv_w, l1_a_log, l1_dt_bias,
                              l1_o_norm_g, l1_w_out, l1_norm2_g, l1_router_w, l1_moe_w1, l1_moe_w2))),
        dict(zip(_EVEN_NAMES, (l2_mod_w, l2_mod_b, l2_norm1_g, l2_w_in, l2_conv_w, l2_q_norm_g, l2_w_uq,
                               l2_kv_norm_g, l2_w_ukv, l2_w_out, l2_norm2_g, l2_ffn_w1, l2_ffn_w2))),
        dict(zip(_ODD_NAMES, (l3_mod_w, l3_mod_b, l3_norm1_g, l3_w_in, l3_qkv_conv_w, l3_a_log, l3_dt_bias,
                              l3_o_norm_g, l3_w_out, l3_norm2_g, l3_router_w, l3_moe_w1, l3_moe_w2))),
    )
    xa = jnp.concatenate([ctx.reshape(batch * ctx_len, d), x.reshape(batch * seq_len, d)], axis=0)
    mod_rows = 8 * ((batch + 1 + 7) // 8)
    cc = jnp.zeros((mod_rows, d), F32).at[:batch].set(c).at[batch].set(c_ctx)
    tabs = _rope_tables(seq_len, ROW_TILE)
    n_layers = len(layers)
    assert n_layers % 2 == 0
    final_g = final_norm_g.reshape(1, -1)
    for li, p in enumerate(layers):
        mods = _mods(cc, p["mod_w"], p["mod_b"])
        last = li == n_layers - 1
        if li % 2 == 0:
            xa = _even_layer(xa, mods, dims, tabs, p)
        else:
            xa = _odd_layer(xa, mods, dims, p, not last, final_g, last)
    return xa.reshape(batch, seq_len, d)
```

```python
import functools
import math

import jax
import jax.numpy as jnp
from jax import lax
from jax.experimental import pallas as pl
from jax.experimental.pallas import tpu as pltpu

F32 = jnp.float32
BF16 = jnp.bfloat16

NORM_EPS = 1e-6
N_MOD = 6
GRID_W = 64
ROPE_BASE = 10000.0

CONV_CH = 512
MLA_HEADS = 8
Q_RANK = 256
KV_RANK = 128
QK_NOPE = 64
QK_ROPE = 32
V_HEAD = 64
MLA_SCALE = (QK_NOPE + QK_ROPE) ** -0.5

GDN_HEADS = 8
GDN_DK = 128
GDN_CHUNK = 64
GDN_QK = GDN_HEADS * GDN_DK

N_EXPERTS = 8

LANES = 128
ROW_TILE = 256
ATTN_HEAD_GROUP = 4
GDN_STEP = 256
GDN_INV_BASE = 16
GDN_HEAD_GROUP = 8
ROW_DMA_UNROLL = 8
EXPERT_TILE = 512
EXPERT_FCHUNK = 512
VMEM_LIMIT_BYTES = 56 * 1024 * 1024
EXPERT_VMEM_LIMIT_BYTES = 62 * 1024 * 1024


def _cparams(*sem):
    return pltpu.CompilerParams(dimension_semantics=sem, vmem_limit_bytes=VMEM_LIMIT_BYTES)


def _silu(x):
    return x * (1.0 / (1.0 + jnp.exp(-x)))


def _rms(x, g):
    return x * lax.rsqrt(jnp.mean(x * x, axis=-1, keepdims=True) + NORM_EPS) * g


def _dot(a, b):
    return jnp.dot(a, b, preferred_element_type=F32)


def _dot_nt(a, b):
    return lax.dot_general(a, b, (((1,), (1,)), ((), ())), preferred_element_type=F32)


def _dot_tn(a, b):
    return lax.dot_general(a, b, (((0,), (0,)), ((), ())), preferred_element_type=F32)


def _split3(x):
    x1 = x.astype(BF16)
    r1 = x - x1.astype(F32)
    x2 = r1.astype(BF16)
    x3 = (r1 - x2.astype(F32)).astype(BF16)
    return x1, x2, x3


def _dot_f32(a, b):
    a1, a2, a3 = _split3(a)
    b1, b2, b3 = _split3(b)
    acc = _dot(a1, b3) + _dot(a2, b2) + _dot(a3, b1)
    acc = acc + _dot(a1, b2) + _dot(a2, b1)
    return acc + _dot(a1, b1)


def _dot_f32_exact_rhs(a, m_bf16):
    a1, a2, a3 = _split3(a)
    return _dot(a3, m_bf16) + _dot(a2, m_bf16) + _dot(a1, m_bf16)


def _mod_row_fn(n_ctx_tiles, tiles_per_seq, batch):
    def row(i):
        return jnp.where(i < n_ctx_tiles, batch, (i - n_ctx_tiles) // tiles_per_seq)
    return row


def _mods_kernel(c_ref, w_ref, b_ref, o_ref):
    o_ref[...] = _dot_f32(_silu(c_ref[...]), w_ref[...]) + b_ref[...]


def _mods(cc, mod_w, mod_b):
    rows, d = cc.shape
    n = mod_w.shape[1]
    out = pl.pallas_call(
        _mods_kernel,
        grid=(n // d,),
        in_specs=[pl.BlockSpec((rows, d), lambda j: (0, 0)),
                  pl.BlockSpec((d, d), lambda j: (0, j)),
                  pl.BlockSpec((1, d), lambda j: (0, j))],
        out_specs=pl.BlockSpec((rows, d), lambda j: (0, j)),
        out_shape=jax.ShapeDtypeStruct((rows, n), F32),
        compiler_params=_cparams("arbitrary"),
        name="adaln_mods",
    )(cc, mod_w, mod_b.reshape(1, n))
    return out.reshape(rows, 1, n)


def _mod_spec(d, k, row, off=0):
    return pl.BlockSpec((1, 1, d), lambda i: (row(i + off), 0, k))


def _even_proj_kernel(x_ref, xp_ref, xn_ref, g_ref, sh_ref, sc_ref, win_ref, cw_ref, qg_ref, wq_ref, kvg_ref, wkv_ref,
                      cos_ref, sin_ref, conv_ref, q_ref, k_ref, v_ref, *, n_ctx_tiles, ctx_tps, x_tps):
    i = pl.program_id(0)
    keep_prev, keep_next = _seq_edges(i, n_ctx_tiles, ctx_tps, x_tps)
    tm = x_ref.shape[0]
    xs = jnp.concatenate([x_ref[...], xp_ref[...], xn_ref[...]], axis=0)
    hb_all = (_rms(xs, g_ref[...]) * (1.0 + sc_ref[0]) + sh_ref[0]).astype(BF16)
    hb = hb_all[:tm]
    c = CONV_CH
    n_gbu = 3 * c
    z_all = _dot(hb_all, win_ref[:, c:2 * c]) * _dot(hb_all, win_ref[:, 2 * c:n_gbu])
    z = z_all[:tm]
    zm, zp = _shifted(z, z_all[tm + X_HALO - 1:tm + X_HALO] * keep_prev, z_all[tm + X_HALO:tm + X_HALO + 1] * keep_next)
    cw = cw_ref[...]
    conv = zm * cw[0:1] + z * cw[1:2] + zp * cw[2:3]
    conv_ref[...] = (_dot(hb, win_ref[:, :c]) * conv).astype(conv_ref.dtype)
    rest = _dot(hb, win_ref[:, n_gbu:])
    cq = rest[:, :Q_RANK]
    ckv = rest[:, Q_RANK:Q_RANK + KV_RANK]
    kr = rest[:, Q_RANK + KV_RANK:Q_RANK + KV_RANK + LANES]
    kr_sw = rest[:, Q_RANK + KV_RANK + LANES:]
    cos = cos_ref[...]
    sin = sin_ref[...]
    kr_rot = kr * cos + kr_sw * sin
    qq = _dot(_rms(cq, qg_ref[...]).astype(BF16), wq_ref[...])
    kv = _dot(_rms(ckv, kvg_ref[...]).astype(BF16), wkv_ref[...])
    hw = MLA_HEADS * LANES
    for h in range(MLA_HEADS):
        sl = slice(h * LANES, (h + 1) * LANES)
        qh = (qq[:, sl] * cos + qq[:, hw + h * LANES:hw + (h + 1) * LANES] * sin) * MLA_SCALE
        q_ref[:, sl] = qh.astype(q_ref.dtype)
        k_ref[:, sl] = (kv[:, sl] + kr_rot).astype(k_ref.dtype)
    v_ref[...] = kv[:, hw:].astype(v_ref.dtype)


def _even_proj(x2, g, mods, row, n_tiles, w_in, conv_w, qg, wq, kvg, wkv, cos_t, sin_t, tab_idx,
               n_ctx_tiles, ctx_tps, x_tps):
    t, d = x2.shape
    tm = ROW_TILE
    n_in = w_in.shape[1]
    hw = MLA_HEADS * LANES
    const = lambda i: (0, 0)
    r = tm // X_HALO
    nb = t // X_HALO
    return pl.pallas_call(
        functools.partial(_even_proj_kernel, n_ctx_tiles=n_ctx_tiles, ctx_tps=ctx_tps, x_tps=x_tps),
        grid=(n_tiles,),
        in_specs=[pl.BlockSpec((tm, d), lambda i: (i, 0)),
                  pl.BlockSpec((X_HALO, d), lambda i: (jnp.maximum(i * r - 1, 0), 0)),
                  pl.BlockSpec((X_HALO, d), lambda i: (jnp.minimum((i + 1) * r, nb - 1), 0)),
                  pl.BlockSpec((1, d), const),
                  _mod_spec(d, 0, row), _mod_spec(d, 1, row),
                  pl.BlockSpec((d, n_in), const),
                  pl.BlockSpec(conv_w.shape, const),
                  pl.BlockSpec((1, Q_RANK), const),
                  pl.BlockSpec(wq.shape, const),
                  pl.BlockSpec((1, KV_RANK), const),
                  pl.BlockSpec(wkv.shape, const),
                  pl.BlockSpec((tm, LANES), lambda i: (tab_idx(i), 0)),
                  pl.BlockSpec((tm, LANES), lambda i: (tab_idx(i), 0))],
        out_specs=[pl.BlockSpec((tm, CONV_CH), lambda i: (i, 0)),
                   pl.BlockSpec((tm, hw), lambda i: (i, 0)),
                   pl.BlockSpec((tm, hw), lambda i: (i, 0)),
                   pl.BlockSpec((tm, MLA_HEADS * V_HEAD), lambda i: (i, 0))],
        out_shape=[jax.ShapeDtypeStruct((t, CONV_CH), BF16),
                   jax.ShapeDtypeStruct((t, hw), BF16),
                   jax.ShapeDtypeStruct((t, hw), BF16),
                   jax.ShapeDtypeStruct((t, MLA_HEADS * V_HEAD), BF16)],
        compiler_params=_cparams("arbitrary"),
        name="even_proj",
    )(x2, x2, x2, g, mods, mods, w_in, conv_w, qg, wq, kvg, wkv, cos_t, sin_t)


def _shifted(z, prev_row, next_row):
    tm = z.shape[0]
    ridx = lax.broadcasted_iota(jnp.int32, z.shape, 0)
    zm = jnp.where(ridx == 0, prev_row, pltpu.roll(z, 1, 0))
    zp = jnp.where(ridx == tm - 1, next_row, pltpu.roll(z, tm - 1, 0))
    return zm, zp


def _seq_edges(i, n_ctx_tiles, ctx_tps, x_tps):
    j = jnp.where(i < n_ctx_tiles, i % ctx_tps, (i - n_ctx_tiles) % x_tps)
    tps = jnp.where(i < n_ctx_tiles, ctx_tps, x_tps)
    keep_prev = (j != 0).astype(F32)
    keep_next = (j != tps - 1).astype(F32)
    return keep_prev, keep_next


def _attn_kernel(q_ref, kc_ref, kx_ref, vc_ref, vx_ref, o_ref, *, n_ctx_q):
    qi = pl.program_id(1)

    def run(use_x):
        for g0 in range(0, MLA_HEADS, ATTN_HEAD_GROUP):
            hs = range(g0, g0 + ATTN_HEAD_GROUP)
            ks = {h: slice(h * LANES, (h + 1) * LANES) for h in hs}
            vs = {h: slice(h * V_HEAD, (h + 1) * V_HEAD) for h in hs}
            q = {h: q_ref[:, ks[h]] for h in hs}
            sc = {h: _dot_nt(q[h], kc_ref[:, ks[h]]) for h in hs}
            m = {h: jnp.max(sc[h], axis=1, keepdims=True) for h in hs}
            if use_x:
                sx = {h: _dot_nt(q[h], kx_ref[:, ks[h]]) for h in hs}
                m = {h: jnp.maximum(m[h], jnp.max(sx[h], axis=1, keepdims=True)) for h in hs}
            pc = {h: jnp.exp(sc[h] - m[h]) for h in hs}
            l = {h: jnp.sum(pc[h], axis=1, keepdims=True) for h in hs}
            o = {h: _dot(pc[h].astype(BF16), vc_ref[:, vs[h]]) for h in hs}
            if use_x:
                px = {h: jnp.exp(sx[h] - m[h]) for h in hs}
                l = {h: l[h] + jnp.sum(px[h], axis=1, keepdims=True) for h in hs}
                o = {h: o[h] + _dot(px[h].astype(BF16), vx_ref[:, vs[h]]) for h in hs}
            outs = [o[h] * (1.0 / l[h]) for h in hs]
            for j in range(0, ATTN_HEAD_GROUP, 2):
                c0 = (g0 + j) * V_HEAD
                o_ref[:, c0:c0 + 2 * V_HEAD] = jnp.concatenate(outs[j:j + 2], axis=1).astype(o_ref.dtype)

    @pl.when(qi < n_ctx_q)
    def _():
        run(False)

    @pl.when(qi >= n_ctx_q)
    def _():
        run(True)


def _attention(q, k, v, batch, ctx_len, seq_len):
    tq = ROW_TILE
    n_ctx_q = ctx_len // tq
    n_x_q = seq_len // tq
    ctx_blocks = batch * ctx_len // tq
    assert (batch * ctx_len) % seq_len == 0
    x_blk0 = batch * ctx_len // seq_len
    hw = MLA_HEADS * LANES
    vw = MLA_HEADS * V_HEAD

    def qrow(b, qi):
        return jnp.where(qi < n_ctx_q, b * n_ctx_q + qi, ctx_blocks + b * n_x_q + (qi - n_ctx_q))

    return pl.pallas_call(
        functools.partial(_attn_kernel, n_ctx_q=n_ctx_q),
        grid=(batch, n_ctx_q + n_x_q),
        in_specs=[pl.BlockSpec((tq, hw), lambda b, qi: (qrow(b, qi), 0)),
                  pl.BlockSpec((ctx_len, hw), lambda b, qi: (b, 0)),
                  pl.BlockSpec((seq_len, hw), lambda b, qi: (x_blk0 + b, 0)),
                  pl.BlockSpec((ctx_len, vw), lambda b, qi: (b, 0)),
                  pl.BlockSpec((seq_len, vw), lambda b, qi: (x_blk0 + b, 0))],
        out_specs=pl.BlockSpec((tq, vw), lambda b, qi: (qrow(b, qi), 0)),
        out_shape=jax.ShapeDtypeStruct((q.shape[0], vw), BF16),
        compiler_params=_cparams("arbitrary", "arbitrary"),
        name="mla_attention",
    )(q, k, k, v, v)


def _resid_mm_kernel(*refs, n_lhs):
    x_ref, gate_ref = refs[0], refs[1]
    lhs = refs[2:2 + n_lhs]
    ws = refs[2 + n_lhs:2 + 2 * n_lhs]
    o_ref = refs[2 + 2 * n_lhs]
    acc = _dot(lhs[0][...], ws[0][...])
    for a_ref, w_ref in zip(lhs[1:], ws[1:]):
        acc = acc + _dot(a_ref[...], w_ref[...])
    o_ref[...] = x_ref[...] + gate_ref[0] * acc


def _resid_mm(x2, mods, k_gate, row, lhs_list, w_list, tm=512):
    t, d = x2.shape
    n = len(lhs_list)
    in_specs = [pl.BlockSpec((tm, d), lambda i: (i, 0)), _mod_spec(d, k_gate, lambda i: row(i * tm // ROW_TILE))]
    in_specs += [pl.BlockSpec((tm, a.shape[1]), lambda i: (i, 0)) for a in lhs_list]
    in_specs += [pl.BlockSpec(w.shape, lambda i: (0, 0)) for w in w_list]
    return pl.pallas_call(
        functools.partial(_resid_mm_kernel, n_lhs=n),
        grid=(t // tm,),
        in_specs=in_specs,
        out_specs=pl.BlockSpec((tm, d), lambda i: (i, 0)),
        out_shape=jax.ShapeDtypeStruct((t, d), F32),
        compiler_params=_cparams("arbitrary"),
        name="resid_matmul",
    )(x2, mods, *lhs_list, *w_list)


def _ffn_up_kernel(x_ref, g_ref, sh_ref, sc_ref, w_ref, o_ref, *, ffn, chunk):
    hn = _rms(x_ref[...], g_ref[...]) * (1.0 + sc_ref[0]) + sh_ref[0]
    hb = hn.astype(BF16)
    for c0 in range(0, ffn, chunk):
        gate = _dot(hb, w_ref[:, c0:c0 + chunk])
        up = _dot(hb, w_ref[:, ffn + c0:ffn + c0 + chunk])
        o_ref[:, c0:c0 + chunk] = (_silu(gate) * up).astype(o_ref.dtype)


def _ffn_up(x2, g, mods, row, w1):
    t, d = x2.shape
    tm = ROW_TILE
    ffn = w1.shape[1] // 2
    chunk = 256
    assert ffn % chunk == 0
    return pl.pallas_call(
        functools.partial(_ffn_up_kernel, ffn=ffn, chunk=chunk),
        grid=(t // tm,),
        in_specs=[pl.BlockSpec((tm, d), lambda i: (i, 0)),
                  pl.BlockSpec((1, d), lambda i: (0, 0)),
                  _mod_spec(d, 3, row), _mod_spec(d, 4, row),
                  pl.BlockSpec(w1.shape, lambda i: (0, 0))],
        out_specs=pl.BlockSpec((tm, ffn), lambda i: (i, 0)),
        out_shape=jax.ShapeDtypeStruct((t, ffn), BF16),
        compiler_params=_cparams("arbitrary"),
        name="ffn_up",
    )(x2, g, mods, mods, w1)


X_HALO = 8


def _odd_proj_kernel(x_ref, xp_ref, xn_ref, g_ref, sh_ref, sc_ref, w_ref, cw_ref, qkv_ref, gate_ref, abt_ref,
                     *, chunk, n_ctx_tiles, ctx_tps, x_tps):
    i = pl.program_id(0)
    keep_prev, keep_next = _seq_edges(i, n_ctx_tiles, ctx_tps, x_tps)
    tm = x_ref.shape[0]
    xs = jnp.concatenate([x_ref[...], xp_ref[...], xn_ref[...]], axis=0)
    hb = (_rms(xs, g_ref[...]) * (1.0 + sc_ref[0]) + sh_ref[0]).astype(BF16)
    n_qkv = qkv_ref.shape[0] * LANES
    n_gate = gate_ref.shape[1]
    cw = cw_ref[...]
    for c0 in range(0, n_qkv, chunk):
        z_all = _dot(hb, w_ref[:, c0:c0 + chunk])
        z = z_all[:tm]
        zprev = z_all[tm + X_HALO - 1:tm + X_HALO] * keep_prev
        znext = z_all[tm + X_HALO:tm + X_HALO + 1] * keep_next
        zm, zp = _shifted(z, zprev, znext)
        y = _silu(zm * cw[0:1, c0:c0 + chunk] + z * cw[1:2, c0:c0 + chunk] + zp * cw[2:3, c0:c0 + chunk])
        for hh in range(chunk // LANES):
            head = c0 // LANES + hh
            yh = y[:, hh * LANES:(hh + 1) * LANES]
            if head < 2 * GDN_HEADS:
                nrm = lax.rsqrt(jnp.sum(yh * yh, axis=-1, keepdims=True) + NORM_EPS)
                if head < GDN_HEADS:
                    nrm = nrm * (GDN_DK ** -0.5)
                yh = yh * nrm
            qkv_ref[head] = yh.astype(qkv_ref.dtype)
    hb_cur = hb[:tm]
    for c0 in range(0, n_gate, chunk):
        gate_ref[:, c0:c0 + chunk] = _dot(hb_cur, w_ref[:, n_qkv + c0:n_qkv + c0 + chunk]).astype(gate_ref.dtype)
    abt_ref[...] = _dot(hb_cur, w_ref[:, n_qkv + n_gate:])


def _odd_proj(x2, g, mods, row, w_in, conv_w, n_ctx_tiles, ctx_tps, x_tps):
    t, d = x2.shape
    tm = ROW_TILE
    n_qkv = 3 * GDN_QK
    n_gate = GDN_QK
    r = tm // X_HALO
    nb = t // X_HALO
    return pl.pallas_call(
        functools.partial(_odd_proj_kernel, chunk=512, n_ctx_tiles=n_ctx_tiles, ctx_tps=ctx_tps, x_tps=x_tps),
        grid=(t // tm,),
        in_specs=[pl.BlockSpec((tm, d), lambda i: (i, 0)),
                  pl.BlockSpec((X_HALO, d), lambda i: (jnp.maximum(i * r - 1, 0), 0)),
                  pl.BlockSpec((X_HALO, d), lambda i: (jnp.minimum((i + 1) * r, nb - 1), 0)),
                  pl.BlockSpec((1, d), lambda i: (0, 0)),
                  _mod_spec(d, 0, row), _mod_spec(d, 1, row),
                  pl.BlockSpec(w_in.shape, lambda i: (0, 0)),
                  pl.BlockSpec(conv_w.shape, lambda i: (0, 0))],
        out_specs=[pl.BlockSpec((n_qkv // LANES, tm, LANES), lambda i: (0, i, 0)),
                   pl.BlockSpec((tm, n_gate), lambda i: (i, 0)),
                   pl.BlockSpec((tm, LANES), lambda i: (i, 0))],
        out_shape=[jax.ShapeDtypeStruct((n_qkv // LANES, t, LANES), BF16),
                   jax.ShapeDtypeStruct((t, n_gate), BF16),
                   jax.ShapeDtypeStruct((t, LANES), F32)],
        compiler_params=_cparams("arbitrary"),
        name="odd_proj",
    )(x2, x2, x2, g, mods, mods, w_in, conv_w)


def _gdn_kernel(qkvf_ref, abtf_ref, qkvb_ref, abtb_ref, alog_ref, dtb_ref, of_ref, ob_ref, *scratch):
    fwd, bwd = scratch[:len(scratch) // 2], scratch[len(scratch) // 2:]

    @pl.when(pl.program_id(1) == 0)
    def _():
        fwd[0][...] = jnp.zeros(fwd[0].shape, fwd[0].dtype)
        bwd[0][...] = jnp.zeros(bwd[0].shape, bwd[0].dtype)

    _gdn_body(0, qkvf_ref, abtf_ref, alog_ref, dtb_ref, of_ref, *fwd)
    _gdn_body(1, qkvb_ref, abtb_ref, alog_ref, dtb_ref, ob_ref, *bwd)


def _gdn_body(d, qkv_ref, abt_ref, alog_ref, dtb_ref, o_ref,
              s_ref, gc_s, beta_s, gtot_s, uw_s, qg_s, kd_s, attn_s, vnew_s, gtc_s):
    n = GDN_STEP
    c = GDN_CHUNK
    nchunk = n // c
    h_n = GDN_HEADS
    sgn = 1 - 2 * d
    ri = lax.broadcasted_iota(jnp.int32, (n, n), 0)
    ci = lax.broadcasted_iota(jnp.int32, (n, n), 1)
    same = (ri // c) == (ci // c)
    rel = (ri - ci) * sgn
    incl = same & (rel >= 0)
    strict = same & (rel > 0)
    eye = (ri == ci).astype(F32)

    hrow = pl.ds(d * h_n, h_n)
    a_all = abt_ref[0, hrow, :]
    bt_all = abt_ref[0, pl.ds(2 * h_n + d * h_n, h_n), :]
    sp_in = a_all + dtb_ref[hrow, :]
    softplus = jnp.maximum(sp_in, 0.0) + jnp.log(1.0 + jnp.exp(-jnp.abs(sp_in)))
    g_all = -jnp.exp(alog_ref[hrow, :]) * softplus
    beta_s[...] = 1.0 / (1.0 + jnp.exp(-bt_all))
    cum_m = jnp.where(same & ((ci - ri) * sgn >= 0), 1.0, 0.0).astype(BF16)
    gc_s[...] = _dot_f32_exact_rhs(g_all, cum_m)
    gtot_s[...] = _dot_f32_exact_rhs(g_all, same.astype(BF16))

    grp = uw_s.shape[0]

    def head_group(gi, carry):
        heads = [gi * grp + u for u in range(grp)]
        us = range(grp)
        q = [qkv_ref[h].astype(F32) for h in heads]
        k = [qkv_ref[h_n + h].astype(F32) for h in heads]
        v = [qkv_ref[2 * h_n + h].astype(F32) for h in heads]
        gc_row = [gc_s[pl.ds(h, 1), :] for h in heads]
        gc_col = [jnp.sum(eye * gc_row[u], axis=1, keepdims=True) for u in us]
        beta_col = [jnp.sum(eye * beta_s[pl.ds(h, 1), :], axis=1, keepdims=True) for h in heads]
        gtot_col = [jnp.sum(eye * gtot_s[pl.ds(h, 1), :], axis=1, keepdims=True) for h in heads]
        decay = [jnp.exp(jnp.where(incl, gc_col[u] - gc_row[u], -1e30)) for u in us]
        kb = [k[u] * beta_col[u] for u in us]
        k16 = [k[u].astype(BF16) for u in us]
        p = [jnp.where(strict, -(_dot_nt(kb[u].astype(BF16), k16[u]) * decay[u]), 0.0) for u in us]
        for u in us:
            attn_s[u] = (_dot_nt(q[u].astype(BF16), k16[u]) * decay[u]).astype(attn_s.dtype)
            gtc_s[u] = jnp.broadcast_to(gtot_col[u], gtc_s.shape[1:])
        base = GDN_INV_BASE
        in_base = (ri // base) == (ci // base)
        pb = [jnp.where(in_base, p[u], 0.0) for u in us]
        tinv = [eye + pb[u] for u in us]
        pw = [pb[u].astype(BF16) for u in us]
        pw = [_dot(pw[u], pw[u]) for u in us]
        for _ in range((base - 1).bit_length() - 2):
            p16 = [pw[u].astype(BF16) for u in us]
            tp = [_dot(jnp.concatenate([tinv[u].astype(BF16), p16[u]], axis=0), p16[u]) for u in us]
            tinv = [tinv[u] + tp[u][:n] for u in us]
            pw = [tp[u][n:] for u in us]
        tinv = [tinv[u] + _dot(tinv[u].astype(BF16), pw[u].astype(BF16)) for u in us]
        size = base
        while size < c:
            off = ((ri // (2 * size)) == (ci // (2 * size))) & ((ri // size) != (ci // size))
            t16 = [tinv[u].astype(BF16) for u in us]
            x = [_dot(jnp.where(off, p[u], 0.0).astype(BF16), t16[u]) for u in us]
            tinv = [tinv[u] + _dot(t16[u], x[u].astype(BF16)) for u in us]
            size *= 2
        egc = [jnp.exp(gc_col[u]) for u in us]
        for u in us:
            rhs = jnp.concatenate([v[u] * beta_col[u], kb[u] * egc[u]], axis=1).astype(BF16)
            uw_s[u] = _dot(tinv[u].astype(BF16), rhs)
            qg_s[u] = q[u] * egc[u]
            kd_s[u] = k[u] * jnp.exp(gtot_col[u] - gc_col[u])
            vnew_s[u] = jnp.zeros(vnew_s.shape[1:], vnew_s.dtype)
        s = [s_ref[h] for h in heads]
        for cstep in range(nchunk):
            r0 = (cstep if d == 0 else nchunk - 1 - cstep) * c
            rows = pl.ds(r0, c)
            ws_qs = [_dot(jnp.concatenate([uw_s[u, rows, LANES:], qg_s[u, rows, :]], axis=0).astype(BF16),
                          s[u].astype(BF16)) for u in us]
            v_new = [uw_s[u, rows, :LANES] - ws_qs[u][:c] for u in us]
            for u in us:
                vnew_s[u, rows, :] = v_new[u].astype(vnew_s.dtype)
            for u in us:
                o_ref[heads[u], rows, :] = ws_qs[u][c:] + _dot(attn_s[u, rows, :], vnew_s[u])
            s = [s[u] * jnp.exp(gtc_s[u, pl.ds(r0, 1), :])
                 + _dot_tn(kd_s[u, rows, :].astype(BF16), v_new[u].astype(BF16)) for u in us]
        for u in us:
            s_ref[heads[u]] = s[u]
        return carry

    lax.fori_loop(0, h_n // grp, head_group, 0)


def _gdn(qkvn, abt_chunks, alog_b, dtb_b, batch, ctx_len, seq_len):
    nh3, t, _ = qkvn.shape
    n = GDN_STEP
    nc = ctx_len // n
    nx = seq_len // n
    ctx_blocks = batch * nc

    def rowblk(b, d, s):
        sc = jnp.where(d == 0, s, nc - 1 - s)
        sx = jnp.where(d == 0, s - nc, nx - 1 - (s - nc))
        return jnp.where(s < nc, b * nc + sc, ctx_blocks + b * nx + sx)

    vm = pltpu.VMEM
    g = GDN_HEAD_GROUP
    per_dir = [vm((GDN_HEADS, GDN_DK, LANES), F32),
               vm((GDN_HEADS, n), F32),
               vm((GDN_HEADS, n), F32),
               vm((GDN_HEADS, n), F32),
               vm((g, n, 2 * LANES), F32),
               vm((g, n, LANES), F32),
               vm((g, n, LANES), F32),
               vm((g, n, n), BF16),
               vm((g, n, LANES), BF16),
               vm((g, n, LANES), F32)]
    qkv_spec = lambda d: pl.BlockSpec((nh3, n, LANES), lambda b, s: (0, rowblk(b, d, s), 0))
    abt_spec = lambda d: pl.BlockSpec((1, 4 * GDN_HEADS, n), lambda b, s: (rowblk(b, d, s), 0, 0))
    out_spec = lambda d: pl.BlockSpec((GDN_HEADS, n, LANES), lambda b, s: (0, rowblk(b, d, s), 0))
    out_sds = jax.ShapeDtypeStruct((GDN_HEADS, t, LANES), F32)
    return pl.pallas_call(
        _gdn_kernel,
        grid=(batch, nc + nx),
        in_specs=[qkv_spec(0), abt_spec(0), qkv_spec(1), abt_spec(1),
                  pl.BlockSpec(alog_b.shape, lambda b, s: (0, 0)),
                  pl.BlockSpec(dtb_b.shape, lambda b, s: (0, 0))],
        out_specs=[out_spec(0), out_spec(1)],
        out_shape=[out_sds, out_sds],
        scratch_shapes=per_dir + per_dir,
        compiler_params=_cparams("arbitrary", "arbitrary"),
        name="gated_delta",
    )(qkvn, abt_chunks, qkvn, abt_chunks, alog_b, dtb_b)


def _odd_out_kernel(x_ref, gate_ref, of_ref, ob_ref, z_ref, og_ref, w_ref, g2_ref, sh_ref, sc_ref, rw_ref,
                    out_ref, h_ref, info_ref, cnt_ref, carry_s):
    parts = []
    og = og_ref[...]
    for h in range(GDN_HEADS):
        o = of_ref[h] + ob_ref[h]
        y = _rms(o, og) * _silu(z_ref[:, h * LANES:(h + 1) * LANES].astype(F32))
        parts.append(y.astype(BF16))
    y_all = jnp.concatenate(parts, axis=1)
    x_new = x_ref[...] + gate_ref[0] * _dot(y_all, w_ref[...])
    out_ref[...] = x_new
    hn = _rms(x_new, g2_ref[...]) * (1.0 + sc_ref[0]) + sh_ref[0]
    h_ref[...] = hn
    _route(hn, rw_ref, info_ref, cnt_ref, carry_s)


def _odd_out(x2, mods, row, o_dirs, z, o_norm_g, w_out, norm2_g, router_w_pad, row_off, n_rows):
    d = x2.shape[1]
    tm = ROW_TILE
    return pl.pallas_call(
        _odd_out_kernel,
        grid=(n_rows // tm,),
        in_specs=[pl.BlockSpec((tm, d), lambda i: (i + row_off, 0)),
                  _mod_spec(d, 2, row, row_off),
                  pl.BlockSpec((GDN_HEADS, tm, LANES), lambda i: (0, i + row_off, 0)),
                  pl.BlockSpec((GDN_HEADS, tm, LANES), lambda i: (0, i + row_off, 0)),
                  pl.BlockSpec((tm, z.shape[1]), lambda i: (i + row_off, 0)),
                  pl.BlockSpec((1, LANES), lambda i: (0, 0)),
                  pl.BlockSpec(w_out.shape, lambda i: (0, 0)),
                  pl.BlockSpec((1, d), lambda i: (0, 0)),
                  _mod_spec(d, 3, row, row_off), _mod_spec(d, 4, row, row_off),
                  pl.BlockSpec(router_w_pad.shape, lambda i: (0, 0))],
        out_specs=[pl.BlockSpec((tm, d), lambda i: (i, 0)),
                   pl.BlockSpec((tm, d), lambda i: (i, 0)),
                   pl.BlockSpec((tm, LANES), lambda i: (i, 0)),
                   pl.BlockSpec((8, LANES), lambda i: (0, 0))],
        out_shape=[jax.ShapeDtypeStruct((n_rows, d), F32),
                   jax.ShapeDtypeStruct((n_rows, d), F32),
                   jax.ShapeDtypeStruct((n_rows, LANES), F32),
                   jax.ShapeDtypeStruct((8, LANES), F32)],
        scratch_shapes=[pltpu.VMEM((8, LANES), F32)],
        compiler_params=_cparams("arbitrary"),
        name="odd_out_router",
    )(x2, mods, o_dirs[0], o_dirs[1], z, o_norm_g, w_out, norm2_g, mods, mods, router_w_pad)


def _route(hn, rw_ref, info_ref, cnt_ref, carry_s):
    i = pl.program_id(0)

    @pl.when(i == 0)
    def _():
        carry_s[...] = jnp.zeros(carry_s.shape, carry_s.dtype)

    tm = hn.shape[0]
    lane = lax.broadcasted_iota(jnp.int32, (tm, LANES), 1)
    logits = jnp.where(lane < N_EXPERTS, _dot_f32(hn, rw_ref[...]), -jnp.inf)
    m1 = jnp.max(logits, axis=1, keepdims=True)
    i1 = jnp.min(jnp.where(logits == m1, lane, LANES), axis=1, keepdims=True)
    rest = jnp.where(lane == i1, -jnp.inf, logits)
    m2 = jnp.max(rest, axis=1, keepdims=True)
    i2 = jnp.min(jnp.where(rest == m2, lane, LANES), axis=1, keepdims=True)
    e2 = jnp.exp(m2 - m1)
    w1 = 1.0 / (1.0 + e2)
    w2 = e2 / (1.0 + e2)
    oh1 = (lane == i1).astype(F32)
    oh2 = (lane == i2).astype(F32)
    cnt = oh1 + oh2
    ri = lax.broadcasted_iota(jnp.int32, (tm, tm), 0)
    ci = lax.broadcasted_iota(jnp.int32, (tm, tm), 1)
    before = (ci < ri).astype(BF16)
    pre = _dot(before, cnt.astype(BF16)) + carry_s[0:1, :]
    r1 = jnp.sum(pre * oh1, axis=1, keepdims=True)
    r2 = jnp.sum(pre * oh2, axis=1, keepdims=True)
    info = jnp.where(lane == 0, i1.astype(F32), 0.0)
    info = jnp.where(lane == 1, i2.astype(F32), info)
    info = jnp.where(lane == 2, w1, info)
    info = jnp.where(lane == 3, w2, info)
    info = jnp.where(lane == 4, r1, info)
    info = jnp.where(lane == 5, r2, info)
    info_ref[...] = info
    total = carry_s[0:1, :] + jnp.sum(cnt, axis=0, keepdims=True)
    carry_s[...] = jnp.broadcast_to(total, carry_s.shape)
    cnt_ref[...] = jnp.broadcast_to(total, cnt_ref.shape)


def _row_copy(src_hbm, src_row, dst_ref, dst_row, sem):
    return pltpu.make_async_copy(src_hbm.at[pl.ds(src_row, 1)], dst_ref.at[pl.ds(dst_row, 1)], sem)


def _dispatch_kernel(dest_ref, h_ref, init_hbm, hs_hbm, sem, *, tm):
    del init_hbm

    def issue(r, carry):
        _row_copy(h_ref, r, hs_hbm, dest_ref[0, 0, r], sem.at[0]).start()
        _row_copy(h_ref, r, hs_hbm, dest_ref[0, 0, tm + r], sem.at[1]).start()
        return carry

    lax.fori_loop(0, tm, issue, 0, unroll=ROW_DMA_UNROLL)
    for k in range(2):
        pltpu.make_async_copy(h_ref, hs_hbm.at[pl.ds(0, tm)], sem.at[k]).wait()


def _dispatch(h, dest_tiles, n_sorted_rows):
    t, d = h.shape
    tm = ROW_TILE
    init = jnp.zeros((n_sorted_rows, d), h.dtype)
    return pl.pallas_call(
        functools.partial(_dispatch_kernel, tm=tm),
        grid=(t // tm,),
        in_specs=[pl.BlockSpec((1, 1, 2 * tm), lambda i: (i, 0, 0), memory_space=pltpu.SMEM),
                  pl.BlockSpec((tm, d), lambda i: (i, 0)),
                  pl.BlockSpec(memory_space=pl.ANY)],
        out_specs=pl.BlockSpec(memory_space=pl.ANY),
        out_shape=jax.ShapeDtypeStruct((n_sorted_rows, d), h.dtype),
        scratch_shapes=[pltpu.SemaphoreType.DMA((2,))],
        input_output_aliases={2: 0},
        compiler_params=_cparams("arbitrary"),
        name="moe_dispatch",
    )(dest_tiles, h, init)


def _expert_kernel(te_ref, nt_ref, hs_ref, w1_ref, w2_ref, ys_ref, *, ffn, chunk):
    del te_ref
    j = pl.program_id(0)

    @pl.when(j < nt_ref[0])
    def _():
        hb = hs_ref[...].astype(BF16)
        acc = None
        for c0 in range(0, ffn, chunk):
            gate = _dot(hb, w1_ref[:, c0:c0 + chunk])
            up = _dot(hb, w1_ref[:, ffn + c0:ffn + c0 + chunk])
            part = _dot((_silu(gate) * up).astype(BF16), w2_ref[c0:c0 + chunk, :])
            acc = part if acc is None else acc + part
        ys_ref[...] = acc

    @pl.when(j >= nt_ref[0])
    def _():
        ys_ref[...] = jnp.zeros(ys_ref.shape, ys_ref.dtype)


def _experts(hs, tile_expert, n_tiles_used, w1, w2):
    r, d = hs.shape
    tm = EXPERT_TILE
    ffn = w2.shape[1]
    assert ffn % EXPERT_FCHUNK == 0

    def tile(j, nt):
        return jnp.minimum(j, nt[0] - 1)

    grid_spec = pltpu.PrefetchScalarGridSpec(
        num_scalar_prefetch=2,
        grid=(r // tm,),
        in_specs=[pl.BlockSpec((tm, d), lambda j, te, nt: (tile(j, nt), 0)),
                  pl.BlockSpec((None, d, 2 * ffn), lambda j, te, nt: (te[tile(j, nt)], 0, 0)),
                  pl.BlockSpec((None, ffn, d), lambda j, te, nt: (te[tile(j, nt)], 0, 0))],
        out_specs=pl.BlockSpec((tm, d), lambda j, te, nt: (j, 0)),
    )
    return pl.pallas_call(
        functools.partial(_expert_kernel, ffn=ffn, chunk=EXPERT_FCHUNK),
        grid_spec=grid_spec,
        out_shape=jax.ShapeDtypeStruct((r, d), F32),
        compiler_params=pltpu.CompilerParams(dimension_semantics=("arbitrary",),
                                             vmem_limit_bytes=EXPERT_VMEM_LIMIT_BYTES),
        name="moe_experts",
    )(tile_expert, n_tiles_used, hs, w1, w2)


def _combine_kernel(dest_ref, x_ref, gate_ref, info_ref, fg_ref, ys_hbm, o_ref, y1_s, y2_s, sem, *, tm, final_norm):
    def issue(r, carry):
        _row_copy(ys_hbm, dest_ref[0, 0, r], y1_s, r, sem.at[0]).start()
        _row_copy(ys_hbm, dest_ref[0, 0, tm + r], y2_s, r, sem.at[1]).start()
        return carry

    lax.fori_loop(0, tm, issue, 0, unroll=ROW_DMA_UNROLL)
    pltpu.make_async_copy(ys_hbm.at[pl.ds(0, tm)], y1_s, sem.at[0]).wait()
    pltpu.make_async_copy(ys_hbm.at[pl.ds(0, tm)], y2_s, sem.at[1]).wait()
    info = info_ref[...]
    y = info[:, 2:3] * y1_s[...] + info[:, 3:4] * y2_s[...]
    out = x_ref[...] + gate_ref[0] * y
    if final_norm:
        out = _rms(out, fg_ref[...])
    o_ref[...] = out


def _combine(x2, mods, row, info, dest_tiles, ys, final_g, final_norm):
    t, d = x2.shape
    tm = ROW_TILE
    return pl.pallas_call(
        functools.partial(_combine_kernel, tm=tm, final_norm=final_norm),
        grid=(t // tm,),
        in_specs=[pl.BlockSpec((1, 1, 2 * tm), lambda i: (i, 0, 0), memory_space=pltpu.SMEM),
                  pl.BlockSpec((tm, d), lambda i: (i, 0)),
                  _mod_spec(d, 5, row),
                  pl.BlockSpec((tm, LANES), lambda i: (i, 0)),
                  pl.BlockSpec((1, d), lambda i: (0, 0)),
                  pl.BlockSpec(memory_space=pl.ANY)],
        out_specs=pl.BlockSpec((tm, d), lambda i: (i, 0)),
        out_shape=jax.ShapeDtypeStruct((t, d), F32),
        scratch_shapes=[pltpu.VMEM((tm, d), F32), pltpu.VMEM((tm, d), F32), pltpu.SemaphoreType.DMA((2,))],
        compiler_params=_cparams("arbitrary"),
        name="moe_combine",
    )(dest_tiles, x2, mods, info, final_g, ys)


def _moe(x2, h, info, counts, mods, row, w1, w2, final_g, final_norm):
    t, d = x2.shape
    tm = ROW_TILE
    te = EXPERT_TILE
    cnt = counts[0, :N_EXPERTS].astype(jnp.int32)
    gsz = ((cnt + te - 1) // te) * te
    ends = jnp.cumsum(gsz)
    offs = ends - gsz
    e1 = info[:, 0].astype(jnp.int32)
    e2 = info[:, 1].astype(jnp.int32)
    dest1 = offs[e1] + info[:, 4].astype(jnp.int32)
    dest2 = offs[e2] + info[:, 5].astype(jnp.int32)
    dest_tiles = jnp.concatenate([dest1.reshape(t // tm, 1, tm), dest2.reshape(t // tm, 1, tm)], axis=2)
    n_sorted = 2 * t + N_EXPERTS * te
    n_tiles_max = n_sorted // te
    starts = jnp.arange(n_tiles_max, dtype=jnp.int32) * te
    tile_expert = jnp.minimum(jnp.sum((starts[:, None] >= ends[None, :]).astype(jnp.int32), axis=1), N_EXPERTS - 1)
    n_used = (ends[-1] // te).astype(jnp.int32).reshape(1)
    hs = _dispatch(h, dest_tiles, n_sorted)
    ys = _experts(hs, tile_expert, n_used, w1, w2)
    return _combine(x2, mods, row, info, dest_tiles, ys, final_g, final_norm)


_ROPE_SWAP = tuple(list(range(8, 16)) + list(range(0, 8)) + list(range(24, 32)) + list(range(16, 24)))


def _even_weights(w_in, w_uq, w_ukv):
    d = w_in.shape[0]
    n0 = 3 * CONV_CH + Q_RANK + KV_RANK
    kr = w_in[:, n0:n0 + QK_ROPE]
    kr_sw = kr[:, jnp.array(_ROPE_SWAP)]
    z = lambda n: jnp.zeros((d, n), w_in.dtype)
    w_in_x = jnp.concatenate([w_in[:, :n0], z(QK_NOPE), kr, z(LANES - QK_NOPE - QK_ROPE),
                              z(QK_NOPE), kr_sw, z(LANES - QK_NOPE - QK_ROPE)], axis=1).astype(BF16)
    uq = w_uq.reshape(Q_RANK, MLA_HEADS, QK_NOPE + QK_ROPE)
    pad = LANES - QK_NOPE - QK_ROPE
    uq_plain = jnp.pad(uq, ((0, 0), (0, 0), (0, pad))).reshape(Q_RANK, MLA_HEADS * LANES)
    uq_rope_sw = uq[:, :, QK_NOPE:][:, :, jnp.array(_ROPE_SWAP)]
    uq_sw = jnp.pad(uq_rope_sw, ((0, 0), (0, 0), (QK_NOPE, pad))).reshape(Q_RANK, MLA_HEADS * LANES)
    wq = jnp.concatenate([uq_plain, uq_sw], axis=1).astype(BF16)
    ukv = w_ukv.reshape(KV_RANK, MLA_HEADS, QK_NOPE + V_HEAD)
    k_part = jnp.pad(ukv[:, :, :QK_NOPE], ((0, 0), (0, 0), (0, LANES - QK_NOPE))).reshape(KV_RANK, MLA_HEADS * LANES)
    v_part = ukv[:, :, QK_NOPE:].reshape(KV_RANK, MLA_HEADS * V_HEAD)
    wkv = jnp.concatenate([k_part, v_part], axis=1).astype(BF16)
    return w_in_x, wq, wkv


def _rope_tables(seq_len, ctx_rows):
    n = QK_ROPE // 4
    t = jnp.arange(seq_len)
    inv = ROPE_BASE ** (-jnp.arange(n, dtype=F32) / n)
    ang_r = (t // GRID_W).astype(F32)[:, None] * inv
    ang_c = (t % GRID_W).astype(F32)[:, None] * inv
    cr, sr, cc, sc = jnp.cos(ang_r), jnp.sin(ang_r), jnp.cos(ang_c), jnp.sin(ang_c)
    cos32 = jnp.concatenate([cr, cr, cc, cc], axis=1)
    sin32 = jnp.concatenate([-sr, sr, -sc, sc], axis=1)
    pad = LANES - QK_NOPE - QK_ROPE
    cos_x = jnp.concatenate([jnp.ones((seq_len, QK_NOPE), F32), cos32, jnp.zeros((seq_len, pad), F32)], axis=1)
    sin_x = jnp.concatenate([jnp.zeros((seq_len, QK_NOPE), F32), sin32, jnp.zeros((seq_len, pad), F32)], axis=1)
    cos_c = jnp.concatenate([jnp.ones((ctx_rows, QK_NOPE + QK_ROPE), F32), jnp.zeros((ctx_rows, pad), F32)], axis=1)
    sin_c = jnp.zeros((ctx_rows, LANES), F32)
    return jnp.concatenate([cos_c, cos_x], axis=0), jnp.concatenate([sin_c, sin_x], axis=0)


def _even_layer(xa, mods, dims, tabs, p):
    batch, ctx_len, seq_len = dims
    tm = ROW_TILE
    t = xa.shape[0]
    n_ctx_tiles = batch * ctx_len // tm
    x_tps = seq_len // tm
    row = _mod_row_fn(n_ctx_tiles, x_tps, batch)
    w_in_x, wq, wkv = _even_weights(p["w_in"], p["w_uq"], p["w_ukv"])
    cos_t, sin_t = tabs
    tab_idx = lambda i: jnp.where(i < n_ctx_tiles, 0, 1 + (i - n_ctx_tiles) % x_tps)
    conv_y, q, k, v = _even_proj(xa, p["norm1_g"].reshape(1, -1), mods, row, t // tm, w_in_x, p["conv_w"],
                                 p["q_norm_g"].reshape(1, -1), wq, p["kv_norm_g"].reshape(1, -1), wkv,
                                 cos_t, sin_t, tab_idx, n_ctx_tiles, ctx_len // tm, x_tps)
    att = _attention(q, k, v, batch, ctx_len, seq_len)
    w_out = p["w_out"].astype(BF16)
    xa = _resid_mm(xa, mods, 2, row, [conv_y, att], [w_out[:CONV_CH], w_out[CONV_CH:]])
    hmid = _ffn_up(xa, p["norm2_g"].reshape(1, -1), mods, row, p["ffn_w1"].astype(BF16))
    return _resid_mm(xa, mods, 5, row, [hmid], [p["ffn_w2"].astype(BF16)])


def _odd_layer(xa, mods, dims, p, ctx_out, final_g, final_norm):
    batch, ctx_len, seq_len = dims
    tm = ROW_TILE
    t = xa.shape[0]
    n_ctx_tiles = batch * ctx_len // tm
    x_tps = seq_len // tm
    row = _mod_row_fn(n_ctx_tiles, x_tps, batch)
    n_w = p["w_in"].shape[1]
    w_in = jnp.pad(p["w_in"], ((0, 0), (0, 4 * GDN_QK + LANES - n_w))).astype(BF16)
    qkvn, z, abt = _odd_proj(xa, p["norm1_g"].reshape(1, -1), mods, row, w_in, p["qkv_conv_w"],
                             n_ctx_tiles, ctx_len // tm, x_tps)
    abt_chunks = abt[:, :4 * GDN_HEADS].reshape(t // GDN_STEP, GDN_STEP, 4 * GDN_HEADS).transpose(0, 2, 1)
    alog_b = jnp.broadcast_to(p["a_log"].reshape(2 * GDN_HEADS, 1), (2 * GDN_HEADS, GDN_STEP))
    dtb_b = jnp.broadcast_to(p["dt_bias"].reshape(2 * GDN_HEADS, 1), (2 * GDN_HEADS, GDN_STEP))
    o_dirs = _gdn(qkvn, abt_chunks, alog_b, dtb_b, batch, ctx_len, seq_len)
    row_off = 0 if ctx_out else n_ctx_tiles
    n_rows = t - row_off * tm
    rw = jnp.pad(p["router_w"], ((0, 0), (0, LANES - N_EXPERTS)))
    xo, h, info, counts = _odd_out(xa, mods, row, o_dirs, z, p["o_norm_g"].reshape(1, -1), p["w_out"].astype(BF16),
                                   p["norm2_g"].reshape(1, -1), rw, row_off, n_rows)
    row_o = row if ctx_out else _mod_row_fn(0, x_tps, batch)
    return _moe(xo, h, info, counts, mods, row_o, p["moe_w1"].astype(BF16), p["moe_w2"].astype(BF16),
                final_g, final_norm)


_EVEN_NAMES = ("mod_w", "mod_b", "norm1_g", "w_in", "conv_w", "q_norm_g", "w_uq", "kv_norm_g", "w_ukv", "w_out",
               "norm2_g", "ffn_w1", "ffn_w2")
_ODD_NAMES = ("mod_w", "mod_b", "norm1_g", "w_in", "qkv_conv_w", "a_log", "dt_bias", "o_norm_g", "w_out",
              "norm2_g", "router_w", "moe_w1", "moe_w2")


def kernel(x, c, ctx, c_ctx, l0_mod_w, l0_mod_b, l0_norm1_g, l0_w_in, l0_conv_w, l0_q_norm_g, l0_w_uq, l0_kv_norm_g, l0_w_ukv, l0_w_out, l0_norm2_g, l0_ffn_w1, l0_ffn_w2, l1_mod_w, l1_mod_b, l1_norm1_g, l1_w_in, l1_qkv_conv_w, l1_a_log, l1_dt_bias, l1_o_norm_g, l1_w_out, l1_norm2_g, l1_router_w, l1_moe_w1, l1_moe_w2, l2_mod_w, l2_mod_b, l2_norm1_g, l2_w_in, l2_conv_w, l2_q_norm_g, l2_w_uq, l2_kv_norm_g, l2_w_ukv, l2_w_out, l2_norm2_g, l2_ffn_w1, l2_ffn_w2, l3_mod_w, l3_mod_b, l3_norm1_g, l3_w_in, l3_qkv_conv_w, l3_a_log, l3_dt_bias, l3_o_norm_g, l3_w_out, l3_norm2_g, l3_router_w, l3_moe_w1, l3_moe_w2, final_norm_g):
    batch, seq_len, d = x.shape
    ctx_len = ctx.shape[1]
    assert seq_len % ROW_TILE == 0 and ctx_len % ROW_TILE == 0 and seq_len % GRID_W == 0
    dims = (batch, ctx_len, seq_len)
    layers = (
        dict(zip(_EVEN_NAMES, (l0_mod_w, l0_mod_b, l0_norm1_g, l0_w_in, l0_conv_w, l0_q_norm_g, l0_w_uq,
                               l0_kv_norm_g, l0_w_ukv, l0_w_out, l0_norm2_g, l0_ffn_w1, l0_ffn_w2))),
        dict(zip(_ODD_NAMES, (l1_mod_w, l1_mod_b, l1_norm1_g, l1_w_in, l1_qkv_conv_w, l1_a_log, l1_dt_bias,
                              l1_o_norm_g, l1_w_out, l1_norm2_g, l1_router_w, l1_moe_w1, l1_moe_w2))),
        dict(zip(_EVEN_NAMES, (l2_mod_w, l2_mod_b, l2_norm1_g, l2_w_in, l2_conv_w, l2_q_norm_g, l2_w_uq,
                               l2_kv_norm_g, l2_w_ukv, l2_w_out, l2_norm2_g, l2_ffn_w1, l2_ffn_w2))),
        dict(zip(_ODD_NAMES, (l3_mod_w, l3_mod_b, l3_norm1_g, l3_w_in, l3_qkv_conv_w, l3_a_log, l3_dt_bias,
                              l3_o_norm_g, l3_w_out, l3_norm2_g, l3_router_w, l3_moe_w1, l3_moe_w2))),
    )
    xa = jnp.concatenate([ctx.reshape(batch * ctx_len, d), x.reshape(batch * seq_len, d)], axis=0)
    mod_rows = 8 * ((batch + 1 + 7) // 8)
    cc = jnp.zeros((mod_rows, d), F32).at[:batch].set(c).at[batch].set(c_ctx)
    tabs = _rope_tables(seq_len, ROW_TILE)
    n_layers = len(layers)
    assert n_layers % 2 == 0
    final_g = final_norm_g.reshape(1, -1)
    for li, p in enumerate(layers):
        mods = _mods(cc, p["mod_w"], p["mod_b"])
        last = li == n_layers - 1
        if li % 2 == 0:
            xa = _even_layer(xa, mods, dims, tabs, p)
        else:
            xa = _odd_layer(xa, mods, dims, p, not last, final_g, last)
    return xa.reshape(batch, seq_len, d)
```
